```python
import jax, jax.numpy as jnp
from jax import lax
import numpy as np

D_MODEL = 1024
BATCH = 4
SEQ = 4096
DEPTH = 2
DEC_BATCH = 32
DEC_SEQ = 1
PAST_LEN = 8192
PAGE_SIZE = 128

A_WIDTH = D_MODEL // 2
A_GROUPS = 4
A_GROUP_DIM = A_WIDTH // A_GROUPS
CHUNK = 128
B_WIDTH = D_MODEL // 2
B_BLOCKS = 8
B_BLOCK_DIM = B_WIDTH // B_BLOCKS
CONV_W = 4
LRU_C = 8.0
C_HEADS = 16
C_HEAD_DIM = D_MODEL // C_HEADS
C_WIDTH = C_HEADS * C_HEAD_DIM
Q_BLOCK = 128
ATTN_SCALE = C_HEAD_DIM ** -0.5
IN0 = 3 * A_WIDTH + 2 * B_WIDTH
IN1 = 4 * C_WIDTH + C_HEADS
EPS = 1e-6

kernel_name = "hybrid_gmlp_rglru_fox_step"


def _rmsnorm(x, g):
    xf = x.astype(jnp.float32)
    y = xf * lax.rsqrt(jnp.mean(xf * xf, axis=-1, keepdims=True) + EPS)
    return (y * g.astype(jnp.float32)).astype(x.dtype)


def _gmlp_mix(u, v, v_g, w_s, b_s):
    Bn, L, _ = v.shape
    u = jax.nn.gelu(u).reshape(Bn, L, A_GROUPS, A_GROUP_DIM)
    v = _rmsnorm(jax.nn.gelu(v).reshape(Bn, L, A_GROUPS, A_GROUP_DIM), v_g)
    pad = (-L) % CHUNK
    vp = jnp.pad(v, ((0, 0), (0, pad), (0, 0), (0, 0)))
    nc = (L + pad) // CHUNK
    vc = vp.reshape(Bn, nc, CHUNK, A_GROUPS, A_GROUP_DIM)
    causal = jnp.tril(jnp.ones((CHUNK, CHUNK), dtype=bool))
    w = jnp.where(causal[None], w_s, jnp.zeros_like(w_s))
    s = jnp.einsum('gts,bcsgd->bctgd', w, vc) + jnp.transpose(b_s)[:, :, None]
    s = s.reshape(Bn, nc * CHUNK, A_GROUPS, A_GROUP_DIM)[:, :L]
    return (u * s).reshape(Bn, L, A_WIDTH), v.reshape(Bn, L, A_WIDTH)


def _lin_combine(c1, c2):
    a1, b1 = c1
    a2, b2 = c2
    return a1 * a2, a2 * b1 + b2


def _rglru_mix(xb, conv_buf, h0, conv_w, conv_b, w_r, b_r, w_i, b_i, lam):
    Bn, L, _ = xb.shape
    xpad = jnp.concatenate([conv_buf.astype(xb.dtype), xb], axis=1)
    xc = conv_b
    for tap in range(CONV_W):
        xc = xc + conv_w[tap] * xpad[:, tap:tap + L]
    new_buf = xpad[:, L:]
    xr = xc.reshape(Bn, L, B_BLOCKS, B_BLOCK_DIM)
    r = jax.nn.sigmoid(jnp.einsum('blhi,hij->blhj', xr, w_r) + b_r).reshape(Bn, L, B_WIDTH)
    gi = jax.nn.sigmoid(jnp.einsum('blhi,hij->blhj', xr, w_i) + b_i).reshape(Bn, L, B_WIDTH)
    log_a = -LRU_C * r.astype(jnp.float32) * jax.nn.softplus(-lam.astype(jnp.float32))
    a = jnp.exp(log_a)
    bterm = jnp.sqrt(-jnp.expm1(2.0 * log_a)) * (gi * xc).astype(jnp.float32)
    bterm = bterm.at[:, 0].add(a[:, 0] * h0.astype(jnp.float32))
    _, h = lax.associative_scan(_lin_combine, (a, bterm), axis=1)
    return h.astype(xb.dtype), new_buf, h[:, -1].astype(h0.dtype)


def _layer_ab(x, conv_buf, h0, norm_g, w_in, v_g, w_s, b_s, conv_w, conv_b,
              w_r, b_r, w_i, b_i, lam, w_out):
    xn = _rmsnorm(x, norm_g)
    proj = xn @ w_in
    u, v, ga, xb, gb = jnp.split(
        proj, [A_WIDTH, 2 * A_WIDTH, 3 * A_WIDTH, 3 * A_WIDTH + B_WIDTH], axis=-1)
    a_out, v_rows = _gmlp_mix(u, v, v_g, w_s, b_s)
    b_out, new_buf, h_last = _rglru_mix(xb, conv_buf, h0, conv_w, conv_b, w_r, b_r, w_i, b_i, lam)
    mixed = jnp.concatenate([a_out * jax.nn.silu(ga), b_out * jax.nn.silu(gb)], axis=-1)
    return x + mixed @ w_out, v_rows, new_buf, h_last


def _fox_project(x, norm_g, w_in, b_f, q_g, k_g):
    Bn, L, _ = x.shape
    proj = _rmsnorm(x, norm_g) @ w_in
    q, k, v, g, f = jnp.split(proj, [C_WIDTH, 2 * C_WIDTH, 3 * C_WIDTH, 4 * C_WIDTH], axis=-1)
    q = _rmsnorm(q.reshape(Bn, L, C_HEADS, C_HEAD_DIM), q_g)
    k = _rmsnorm(k.reshape(Bn, L, C_HEADS, C_HEAD_DIM), k_g)
    v = v.reshape(Bn, L, C_HEADS, C_HEAD_DIM)
    logf = jax.nn.log_sigmoid((f + b_f).astype(jnp.float32))
    return q, k, v, g, logf


def _fox_prompt_attention(q, k, v, logf):
    Bn, L, H, hd = q.shape
    nb = L // Q_BLOCK
    ck = jnp.transpose(jnp.cumsum(logf, axis=1), (0, 2, 1))
    qb = q.reshape(Bn, nb, Q_BLOCK, H, hd).transpose(1, 0, 2, 3, 4)
    cqb = ck.reshape(Bn, H, nb, Q_BLOCK).transpose(2, 0, 1, 3)
    key_pos = jnp.arange(L)

    def one_block(args):
        q_i, cq_i, blk = args
        s = jnp.einsum('bqhd,bkhd->bhqk', q_i, k).astype(jnp.float32) * ATTN_SCALE
        s = s + cq_i[..., :, None] - ck[:, :, None, :]
        q_pos = blk * Q_BLOCK + jnp.arange(Q_BLOCK)
        mask = key_pos[None, :] <= q_pos[:, None]
        p = jax.nn.softmax(jnp.where(mask, s, -jnp.inf), axis=-1)
        return jnp.einsum('bhqk,bkhd->bqhd', p.astype(v.dtype), v)

    out = lax.map(one_block, (qb, cqb, jnp.arange(nb)))
    return out.transpose(1, 0, 2, 3, 4).reshape(Bn, L, H * hd)


def _fox_sample_attention(q, k, v, logf, past_k, past_v, past_logf):
    Bn, T, H, hd = q.shape
    P = past_k.shape[1]
    k_all = jnp.concatenate([past_k.astype(k.dtype), k], axis=1)
    v_all = jnp.concatenate([past_v.astype(v.dtype), v], axis=1)
    csum = jnp.cumsum(jnp.concatenate([past_logf.astype(jnp.float32), logf], axis=1), axis=1)
    ck = jnp.transpose(csum, (0, 2, 1))
    cq = ck[:, :, P:]
    s = jnp.einsum('bqhd,bkhd->bhqk', q, k_all).astype(jnp.float32) * ATTN_SCALE
    s = s + cq[..., :, None] - ck[:, :, None, :]
    mask = jnp.arange(P + T)[None, :] <= (P + jnp.arange(T))[:, None]
    p = jax.nn.softmax(jnp.where(mask, s, -jnp.inf), axis=-1)
    out = jnp.einsum('bhqk,bkhd->bqhd', p.astype(v_all.dtype), v_all)
    return out.reshape(Bn, T, H * hd)


def setup_inputs(seed: int = 0) -> dict:
    key = jax.random.key(seed)
    ks = jax.random.split(key, 32)
    n_pages = PAST_LEN // PAGE_SIZE
    used = DEC_BATCH * n_pages
    n_pool = used + max(1, used // 4)
    nrm = lambda k, shape, scale: jax.random.normal(k, shape, jnp.float32) * scale
    page_table = jax.random.permutation(ks[0], n_pool)[:used].reshape(DEC_BATCH, n_pages).astype(jnp.int32)
    a0 = jax.random.uniform(ks[1], (B_WIDTH,), jnp.float32, 0.9, 0.999)
    return {
        "x_prompt": nrm(ks[2], (BATCH, SEQ, D_MODEL), 1.0),
        "x_sample": nrm(ks[3], (DEC_BATCH, DEC_SEQ, D_MODEL), 1.0),
        "state_lru_conv": nrm(ks[4], (DEC_BATCH, CONV_W - 1, B_WIDTH), 1.0),
        "state_lru_h": nrm(ks[5], (DEC_BATCH, B_WIDTH), 0.5),
        "cache_k": nrm(ks[6], (n_pool, PAGE_SIZE, C_HEADS, C_HEAD_DIM), 1.0),
        "cache_v": nrm(ks[7], (n_pool, PAGE_SIZE, C_HEADS, C_HEAD_DIM), 1.0),
        "cache_logf": jax.nn.log_sigmoid(3.0 + nrm(ks[8], (n_pool, PAGE_SIZE, C_HEADS), 1.0)),
        "page_table": page_table,
        "norm0_g": 1.0 + nrm(ks[9], (D_MODEL,), 0.02),
        "w_in0": nrm(ks[10], (D_MODEL, IN0), D_MODEL ** -0.5),
        "gmlp_v_g": 1.0 + nrm(ks[11], (A_GROUPS, A_GROUP_DIM), 0.02),
        "gmlp_w_s": nrm(ks[12], (A_GROUPS, CHUNK, CHUNK), CHUNK ** -0.5),
        "gmlp_b_s": 1.0 + nrm(ks[13], (A_GROUPS, CHUNK), 0.02),
        "lru_conv_w": nrm(ks[14], (CONV_W, B_WIDTH), CONV_W ** -0.5),
        "lru_conv_b": nrm(ks[15], (B_WIDTH,), 0.02),
        "lru_w_r": nrm(ks[16], (B_BLOCKS, B_BLOCK_DIM, B_BLOCK_DIM), B_BLOCK_DIM ** -0.5),
        "lru_b_r": nrm(ks[17], (B_BLOCKS, B_BLOCK_DIM), 0.02),
        "lru_w_i": nrm(ks[18], (B_BLOCKS, B_BLOCK_DIM, B_BLOCK_DIM), B_BLOCK_DIM ** -0.5),
        "lru_b_i": nrm(ks[19], (B_BLOCKS, B_BLOCK_DIM), 0.02),
        "lru_lambda": jnp.log(a0) - jnp.log1p(-a0),
        "w_out0": nrm(ks[20], (A_WIDTH + B_WIDTH, D_MODEL), (A_WIDTH + B_WIDTH) ** -0.5),
        "norm1_g": 1.0 + nrm(ks[21], (D_MODEL,), 0.02),
        "w_in1": nrm(ks[22], (D_MODEL, IN1), D_MODEL ** -0.5),
        "fox_b_f": 3.0 + nrm(ks[23], (C_HEADS,), 0.1),
        "q_norm_g": 1.0 + nrm(ks[24], (C_HEAD_DIM,), 0.02),
        "k_norm_g": 1.0 + nrm(ks[25], (C_HEAD_DIM,), 0.02),
        "w_out1": nrm(ks[26], (C_WIDTH, D_MODEL), C_WIDTH ** -0.5),
    }


def reference(x_prompt, x_sample, state_lru_conv, state_lru_h, cache_k, cache_v, cache_logf,
              page_table, norm0_g, w_in0, gmlp_v_g, gmlp_w_s, gmlp_b_s, lru_conv_w, lru_conv_b,
              lru_w_r, lru_b_r, lru_w_i, lru_b_i, lru_lambda, w_out0, norm1_g, w_in1, fox_b_f,
              q_norm_g, k_norm_g, w_out1):
    yp, ys = x_prompt, x_sample
    Bp = x_prompt.shape[0]
    Bs = x_sample.shape[0]
    for layer in range(DEPTH):
        if layer % 2 == 0:
            ab_params = (norm0_g, w_in0, gmlp_v_g, gmlp_w_s, gmlp_b_s, lru_conv_w, lru_conv_b,
                         lru_w_r, lru_b_r, lru_w_i, lru_b_i, lru_lambda, w_out0)
            conv0 = jnp.zeros((Bp, CONV_W - 1, B_WIDTH), yp.dtype)
            h0 = jnp.zeros((Bp, B_WIDTH), yp.dtype)
            yp, _, lru_conv_p, lru_h_p = _layer_ab(yp, conv0, h0, *ab_params)
            ys, gmlp_v_s, lru_conv_s, lru_h_s = _layer_ab(ys, state_lru_conv, state_lru_h, *ab_params)
        else:
            q, k_p, v_p, g, logf_p = _fox_project(yp, norm1_g, w_in1, fox_b_f, q_norm_g, k_norm_g)
            att = _fox_prompt_attention(q, k_p, v_p, logf_p)
            yp = yp + (att * jax.nn.silu(g)) @ w_out1
            q, k_s, v_s, g, logf_s = _fox_project(ys, norm1_g, w_in1, fox_b_f, q_norm_g, k_norm_g)
            past_k = cache_k[page_table].reshape(Bs, -1, C_HEADS, C_HEAD_DIM)
            past_v = cache_v[page_table].reshape(Bs, -1, C_HEADS, C_HEAD_DIM)
            past_logf = cache_logf[page_table].reshape(Bs, -1, C_HEADS)
            att = _fox_sample_attention(q, k_s, v_s, logf_s, past_k, past_v, past_logf)
            ys = ys + (att * jax.nn.silu(g)) @ w_out1
    return (yp, ys, lru_conv_p, lru_h_p, k_p, v_p, logf_p,
            gmlp_v_s, lru_conv_s, lru_h_s, k_s, v_s, logf_s)
```

```python
import functools

import jax
import jax.numpy as jnp
import numpy as np
from jax import lax
from jax.experimental import pallas as pl
from jax.experimental.pallas import tpu as pltpu

D_MODEL = 1024
A_WIDTH = 512
A_GROUPS = 4
A_GROUP_DIM = 128
CHUNK = 128
B_WIDTH = 512
B_BLOCKS = 8
B_BLOCK_DIM = 64
CONV_W = 4
LRU_C = 8.0
C_HEADS = 16
C_HEAD_DIM = 64
C_WIDTH = 1024
PAGE_SIZE = 128
ATTN_SCALE = C_HEAD_DIM ** -0.5
EPS = 1e-6

LANES = 128
SUBLANES = 8
NEG_BIG = -1e30

F32 = jnp.float32
BF16 = jnp.bfloat16

L0_TILE = 256
PROJ_TILE = 256
ATT_TQ = 512
ATT_TK = 512
OUT_TILE = 512
PAGES_PER_STEP = 8
VMEM_LIMIT = 48 * 1024 * 1024


def _cparams(n_grid_dims):
    return pltpu.CompilerParams(
        dimension_semantics=("arbitrary",) * n_grid_dims,
        vmem_limit_bytes=VMEM_LIMIT,
    )


def _dot(a, b):
    return jnp.dot(a, b, preferred_element_type=F32)


def _dot_nt(a, b):
    return lax.dot_general(a, b, (((1,), (1,)), ((), ())), preferred_element_type=F32)


def _dot_tn(a, b):
    return lax.dot_general(a, b, (((0,), (0,)), ((), ())), preferred_element_type=F32)


def _split_bf16(x):
    hi = x.astype(BF16)
    lo = (x - hi.astype(F32)).astype(BF16)
    return hi, lo


def _rms_rows(x, gain):
    ms = jnp.mean(x * x, axis=-1, keepdims=True)
    return x * lax.rsqrt(ms + EPS) * gain


def _gmlp_v_rows(pv, vg_ref):
    v = jax.nn.gelu(pv)
    parts = []
    for g in range(A_GROUPS):
        sl = slice(g * A_GROUP_DIM, (g + 1) * A_GROUP_DIM)
        parts.append(_rms_rows(v[:, sl], vg_ref[:, sl]))
    return jnp.concatenate(parts, axis=-1)


def _lru_gates(xc, wr_ref, br_ref, wi_ref, bi_ref, lam_ref):
    xcb = xc.astype(BF16)
    r = jax.nn.sigmoid(_dot(xcb, wr_ref[...]) + br_ref[...])
    gi = jax.nn.sigmoid(_dot(xcb, wi_ref[...]) + bi_ref[...])
    log_a = -LRU_C * r * jax.nn.softplus(-lam_ref[...])
    a = jnp.exp(log_a)
    bterm = jnp.sqrt(-jnp.tanh(log_a) * (1.0 + a * a)) * (gi * xc)
    return a, bterm


def _layer0_prompt_kernel(x_ref, g_ref, win_ref, vg_ref, ws_ref, bst_ref, cw_ref, cb_ref,
                          wr_ref, br_ref, wi_ref, bi_ref, lam_ref, wout_ref, conv0_ref, h0_ref,
                          y_ref, convo_ref, ho_ref,
                          xbuf, hcar, a8, b8, s_scr):
    T = L0_TILE
    i = pl.program_id(1)

    @pl.when(i == 0)
    def _():
        xbuf[0:SUBLANES, :] = jnp.zeros((SUBLANES, B_WIDTH), F32)
        xbuf[SUBLANES - (CONV_W - 1):SUBLANES, :] = conv0_ref[0]
        hcar[...] = h0_ref[0]

    x = x_ref[0]
    xn = _rms_rows(x, g_ref[...]).astype(BF16)

    def proj(k):
        return _dot(xn, win_ref[:, k * 512:(k + 1) * 512])

    vn = _gmlp_v_rows(proj(1), vg_ref).astype(BF16)
    tri = (lax.broadcasted_iota(jnp.int32, (CHUNK, CHUNK), 0)
           >= lax.broadcasted_iota(jnp.int32, (CHUNK, CHUNK), 1))
    for g in range(A_GROUPS):
        wg = jnp.where(tri, ws_ref[g], 0.0).astype(BF16)
        bias = bst_ref[:, g:g + 1]
        for c in range(T // CHUNK):
            blk = vn[c * CHUNK:(c + 1) * CHUNK, g * A_GROUP_DIM:(g + 1) * A_GROUP_DIM]
            s_scr[c * CHUNK:(c + 1) * CHUNK, g * A_GROUP_DIM:(g + 1) * A_GROUP_DIM] = (
                _dot(wg, blk) + bias)
    u = jax.nn.gelu(proj(0))
    mix_a = (u * s_scr[...] * jax.nn.silu(proj(2))).astype(BF16)

    xb = proj(3)
    xbuf[SUBLANES:SUBLANES + T, :] = xb
    xc = cb_ref[...] + cw_ref[3:4, :] * xb
    for tap in range(CONV_W - 1):
        off = SUBLANES - (CONV_W - 1) + tap
        xc = xc + cw_ref[tap:tap + 1, :] * xbuf[off:off + T, :]
    convo_ref[0] = xbuf[T + SUBLANES - (CONV_W - 1):T + SUBLANES, :]
    xbuf[0:SUBLANES, :] = xbuf[T:T + SUBLANES, :]

    a, bt = _lru_gates(xc, wr_ref, br_ref, wi_ref, bi_ref, lam_ref)

    row = lax.broadcasted_iota(jnp.int32, (T, B_WIDTH), 0) & (SUBLANES - 1)
    shift = 1
    while shift < SUBLANES:
        a_sh = pltpu.roll(a, shift, axis=0)
        b_sh = pltpu.roll(bt, shift, axis=0)
        m = row >= shift
        bt = jnp.where(m, a * b_sh + bt, bt)
        a = jnp.where(m, a * a_sh, a)
        shift *= 2
    a8[...] = a
    b8[...] = bt

    def group_step(j, h):
        off = pl.multiple_of(j * SUBLANES, SUBLANES)
        rows = a8[pl.ds(off, SUBLANES), :] * h + b8[pl.ds(off, SUBLANES), :]
        b8[pl.ds(off, SUBLANES), :] = rows
        return rows[SUBLANES - 1:SUBLANES, :]

    h_last = lax.fori_loop(0, T // SUBLANES, group_step, hcar[...], unroll=True)
    hcar[...] = h_last
    ho_ref[0] = h_last

    mix_b = (b8[...] * jax.nn.silu(proj(4))).astype(BF16)

    y = _dot(mix_a, wout_ref[0:A_WIDTH, :]) + _dot(mix_b, wout_ref[A_WIDTH:, :]) + x
    y_ref[0] = y


def _layer0_prompt(x, conv0, h0, w):
    B, L, _ = x.shape
    T = L0_TILE
    assert L % T == 0 and T % CHUNK == 0
    nt = L // T
    full = lambda shape: pl.BlockSpec(shape, lambda b, i: (0,) * len(shape))
    in_specs = [
        pl.BlockSpec((1, T, D_MODEL), lambda b, i: (b, i, 0)),
        full((1, D_MODEL)),
        full(w["win0"].shape),
        full((1, A_WIDTH)),
        full((A_GROUPS, CHUNK, CHUNK)),
        full((CHUNK, A_GROUPS)),
        full((CONV_W, B_WIDTH)),
        full((1, B_WIDTH)),
        full((B_WIDTH, B_WIDTH)), full((1, B_WIDTH)),
        full((B_WIDTH, B_WIDTH)), full((1, B_WIDTH)),
        full((1, B_WIDTH)),
        full((D_MODEL, D_MODEL)),
        pl.BlockSpec((1, CONV_W - 1, B_WIDTH), lambda b, i: (b, 0, 0)),
        pl.BlockSpec((1, 1, B_WIDTH), lambda b, i: (b, 0, 0)),
    ]
    out_specs = [
        pl.BlockSpec((1, T, D_MODEL), lambda b, i: (b, i, 0)),
        pl.BlockSpec((1, CONV_W - 1, B_WIDTH), lambda b, i: (b, 0, 0)),
        pl.BlockSpec((1, 1, B_WIDTH), lambda b, i: (b, 0, 0)),
    ]
    out_shape = [
        jax.ShapeDtypeStruct((B, L, D_MODEL), F32),
        jax.ShapeDtypeStruct((B, CONV_W - 1, B_WIDTH), F32),
        jax.ShapeDtypeStruct((B, 1, B_WIDTH), F32),
    ]
    y, convo, ho = pl.pallas_call(
        _layer0_prompt_kernel,
        grid=(B, nt),
        in_specs=in_specs,
        out_specs=out_specs,
        out_shape=out_shape,
        scratch_shapes=[
            pltpu.VMEM((T + SUBLANES, B_WIDTH), F32),
            pltpu.VMEM((1, B_WIDTH), F32),
            pltpu.VMEM((T, B_WIDTH), F32),
            pltpu.VMEM((T, B_WIDTH), F32),
            pltpu.VMEM((T, A_WIDTH), F32),
        ],
        compiler_params=_cparams(2),
        name="layer0_prompt",
    )(x, w["norm0_g"], w["win0"], w["vg"], w["ws"], w["bst"], w["cw"], w["cb"],
      w["wr"], w["br"], w["wi"], w["bi"], w["lam"], w["wout0"], conv0, h0.reshape(B, 1, B_WIDTH))
    return y, convo, ho.reshape(B, B_WIDTH)


def _layer0_sample_kernel(x_ref, g_ref, win_ref, vg_ref, ws_ref, bst_ref, cw_ref, cb_ref,
                          wr_ref, br_ref, wi_ref, bi_ref, lam_ref, wout_ref, conv_ref, h_ref,
                          y_ref, v_ref, convo_ref, ho_ref):
    x = x_ref[...]
    xn = _rms_rows(x, g_ref[...]).astype(BF16)

    def proj(k):
        return _dot(xn, win_ref[:, k * 512:(k + 1) * 512])

    vn = _gmlp_v_rows(proj(1), vg_ref)
    v_ref[...] = vn
    s_parts = []
    for g in range(A_GROUPS):
        sl = slice(g * A_GROUP_DIM, (g + 1) * A_GROUP_DIM)
        s_parts.append(ws_ref[g, 0:1, 0:1] * vn[:, sl] + bst_ref[0:1, g:g + 1])
    s = jnp.concatenate(s_parts, axis=-1)
    mix_a = (jax.nn.gelu(proj(0)) * s * jax.nn.silu(proj(2))).astype(BF16)

    xb = proj(3)
    xc = cb_ref[...] + cw_ref[3:4, :] * xb
    for tap in range(CONV_W - 1):
        xc = xc + cw_ref[tap:tap + 1, :] * conv_ref[:, tap * B_WIDTH:(tap + 1) * B_WIDTH]
    for tap in range(CONV_W - 2):
        convo_ref[:, tap * B_WIDTH:(tap + 1) * B_WIDTH] = (
            conv_ref[:, (tap + 1) * B_WIDTH:(tap + 2) * B_WIDTH])
    convo_ref[:, (CONV_W - 2) * B_WIDTH:] = xb

    a, bt = _lru_gates(xc, wr_ref, br_ref, wi_ref, bi_ref, lam_ref)
    h = a * h_ref[...] + bt
    ho_ref[...] = h
    mix_b = (h * jax.nn.silu(proj(4))).astype(BF16)
    y_ref[...] = _dot(mix_a, wout_ref[0:A_WIDTH, :]) + _dot(mix_b, wout_ref[A_WIDTH:, :]) + x


def _layer0_sample(x, conv, h, w):
    n = x.shape[0]
    out_shape = [
        jax.ShapeDtypeStruct((n, D_MODEL), F32),
        jax.ShapeDtypeStruct((n, A_WIDTH), F32),
        jax.ShapeDtypeStruct((n, (CONV_W - 1) * B_WIDTH), F32),
        jax.ShapeDtypeStruct((n, B_WIDTH), F32),
    ]
    return pl.pallas_call(
        _layer0_sample_kernel,
        out_shape=out_shape,
        compiler_params=pltpu.CompilerParams(vmem_limit_bytes=VMEM_LIMIT),
        name="layer0_sample",
    )(x, w["norm0_g"], w["win0"], w["vg"], w["ws"], w["bst"], w["cw"], w["cb"],
      w["wr"], w["br"], w["wi"], w["bi"], w["lam"], w["wout0"],
      conv.reshape(n, (CONV_W - 1) * B_WIDTH), h)


def _fox_project_rows(x, g_ref, w_ref, wf_ref, bf_ref, qg_ref, kg_ref, seg_ref, exp_ref):
    xn = _rms_rows(x, g_ref[...]).astype(BF16)

    def head_norm(t, gain):
        ssq = _dot((t * t).astype(BF16), seg_ref[...])
        rs = lax.rsqrt(ssq * (1.0 / C_HEAD_DIM) + EPS)
        hi, lo = _split_bf16(rs)
        rs_full = _dot(hi, exp_ref[...]) + _dot(lo, exp_ref[...])
        return t * rs_full * gain

    q = head_norm(_dot(xn, w_ref[:, 0:C_WIDTH]), qg_ref[...]) * ATTN_SCALE
    k = head_norm(_dot(xn, w_ref[:, C_WIDTH:2 * C_WIDTH]), kg_ref[...])
    v = _dot(xn, w_ref[:, 2 * C_WIDTH:3 * C_WIDTH])
    gate = _dot(xn, w_ref[:, 3 * C_WIDTH:4 * C_WIDTH])
    f = _dot(xn, wf_ref[...]) + bf_ref[...]
    logf = jax.nn.log_sigmoid(f)
    return q, k, v, gate, logf


def _fox_proj_prompt_kernel(x_ref, g_ref, w_ref, wf_ref, bf_ref, qg_ref, kg_ref, seg_ref, exp_ref,
                            q_ref, k_ref, v_ref, gate_ref, lf_ref, ct_ref, carry):
    T = PROJ_TILE
    i = pl.program_id(1)

    @pl.when(i == 0)
    def _():
        carry[...] = jnp.zeros_like(carry)

    q, k, v, gate, logf = _fox_project_rows(
        x_ref[0], g_ref, w_ref, wf_ref, bf_ref, qg_ref, kg_ref, seg_ref, exp_ref)
    q_ref[0] = q.astype(BF16)
    k_ref[0] = k
    v_ref[0] = v
    gate_ref[0] = gate
    lf_ref[0] = logf[:, 0:C_HEADS]

    tri = (lax.broadcasted_iota(jnp.int32, (T, T), 0)
           >= lax.broadcasted_iota(jnp.int32, (T, T), 1)).astype(BF16)
    hi, lo = _split_bf16(logf)
    c = _dot(tri, hi) + _dot(tri, lo) + carry[...]
    carry[...] = c[T - 1:T, :]
    ct_ref[0] = c.T[0:C_HEADS, :]


def _fox_proj_prompt(x, w):
    B, L, _ = x.shape
    T = PROJ_TILE
    assert L % T == 0
    full = lambda shape: pl.BlockSpec(shape, lambda b, i: (0,) * len(shape))
    row_spec = lambda width: pl.BlockSpec((1, T, width), lambda b, i: (b, i, 0))
    in_specs = [
        row_spec(D_MODEL), full((1, D_MODEL)), full((D_MODEL, 4 * C_WIDTH)),
        full((D_MODEL, LANES)), full((1, LANES)), full((1, C_WIDTH)), full((1, C_WIDTH)),
        full((C_WIDTH, LANES)), full((LANES, C_WIDTH)),
    ]
    out_specs = [
        row_spec(C_WIDTH), row_spec(C_WIDTH), row_spec(C_WIDTH), row_spec(C_WIDTH),
        row_spec(C_HEADS),
        pl.BlockSpec((1, C_HEADS, T), lambda b, i: (b, 0, i)),
    ]
    out_shape = [
        jax.ShapeDtypeStruct((B, L, C_WIDTH), BF16),
        jax.ShapeDtypeStruct((B, L, C_WIDTH), F32),
        jax.ShapeDtypeStruct((B, L, C_WIDTH), F32),
        jax.ShapeDtypeStruct((B, L, C_WIDTH), F32),
        jax.ShapeDtypeStruct((B, L, C_HEADS), F32),
        jax.ShapeDtypeStruct((B, C_HEADS, L), F32),
    ]
    return pl.pallas_call(
        _fox_proj_prompt_kernel,
        grid=(B, L // T),
        in_specs=in_specs,
        out_specs=out_specs,
        out_shape=out_shape,
        scratch_shapes=[pltpu.VMEM((1, LANES), F32)],
        compiler_params=_cparams(2),
        name="fox_proj_prompt",
    )(x, w["norm1_g"], w["win1"], w["wf"], w["bf"], w["qg"], w["kg"], w["seg"], w["exp"])


def _fox_proj_sample_kernel(x_ref, g_ref, w_ref, wf_ref, bf_ref, qg_ref, kg_ref, seg_ref, exp_ref,
                            q_ref, k_ref, v_ref, gate_ref, lf_ref):
    q, k, v, gate, logf = _fox_project_rows(
        x_ref[...], g_ref, w_ref, wf_ref, bf_ref, qg_ref, kg_ref, seg_ref, exp_ref)
    q_ref[...] = q.astype(BF16)
    k_ref[...] = k
    v_ref[...] = v
    gate_ref[...] = gate
    lf_ref[...] = logf[:, 0:C_HEADS]


def _fox_proj_sample(x, w):
    n = x.shape[0]
    out_shape = [
        jax.ShapeDtypeStruct((n, C_WIDTH), BF16),
        jax.ShapeDtypeStruct((n, C_WIDTH), F32),
        jax.ShapeDtypeStruct((n, C_WIDTH), F32),
        jax.ShapeDtypeStruct((n, C_WIDTH), F32),
        jax.ShapeDtypeStruct((n, C_HEADS), F32),
    ]
    return pl.pallas_call(
        _fox_proj_sample_kernel,
        out_shape=out_shape,
        compiler_params=pltpu.CompilerParams(vmem_limit_bytes=VMEM_LIMIT),
        name="fox_proj_sample",
    )(x, w["norm1_g"], w["win1"], w["wf"], w["bf"], w["qg"], w["kg"], w["seg"], w["exp"])


def _fox_attn_kernel(qi_tab, ki_tab, q_ref, k_ref, v_ref, ct_ref, o_ref, m_scr, l_scr, acc_scr):
    hp = pl.program_id(1)
    step = pl.program_id(2)
    qi = qi_tab[step]
    ki = ki_tab[step]
    tq, tk = ATT_TQ, ATT_TK

    @pl.when(ki == 0)
    def _():
        m_scr[...] = jnp.full(m_scr.shape, NEG_BIG, F32)
        l_scr[...] = jnp.zeros(l_scr.shape, F32)
        acc_scr[...] = jnp.zeros(acc_scr.shape, F32)

    def block(masked):
        q = q_ref[0]
        k = k_ref[0].astype(BF16)
        v = v_ref[0].astype(BF16)
        lane_hi = lax.broadcasted_iota(jnp.int32, (tq, LANES), 1) >= C_HEAD_DIM
        grp = ct_ref[0, pl.ds(pl.multiple_of((hp // 4) * SUBLANES, SUBLANES), SUBLANES), :]
        sub = lax.broadcasted_iota(jnp.int32, (SUBLANES, tk), 0)
        ck = [jnp.sum(jnp.where(sub == 2 * (hp % 4) + h, grp, 0.0), axis=0, keepdims=True)
              for h in range(2)]
        if masked:
            causal = (lax.broadcasted_iota(jnp.int32, (tq, tk), 0)
                      >= lax.broadcasted_iota(jnp.int32, (tq, tk), 1))
        outs, alphas = [], []
        for h in range(2):
            head_lanes = lane_hi if h == 1 else jnp.logical_not(lane_hi)
            qh = jnp.where(head_lanes, q, jnp.zeros_like(q))
            s = _dot_nt(qh, k) - ck[h]
            if masked:
                s = jnp.where(causal, s, NEG_BIG)
            m_old = m_scr[h]
            m_new = jnp.maximum(m_old, jnp.max(s, axis=1, keepdims=True))
            alpha = jnp.exp(m_old - m_new)
            p = jnp.exp(s - m_new)
            l_scr[h] = alpha * l_scr[h] + jnp.sum(p, axis=1, keepdims=True)
            m_scr[h] = m_new
            outs.append(_dot(p.astype(BF16), v))
            alphas.append(alpha)
        acc = acc_scr[...]
        acc_scr[...] = jnp.where(lane_hi, alphas[1] * acc + outs[1], alphas[0] * acc + outs[0])

    @pl.when(ki < qi)
    def _():
        block(False)

    @pl.when(ki == qi)
    def _():
        block(True)
        lane_hi = lax.broadcasted_iota(jnp.int32, (tq, LANES), 1) >= C_HEAD_DIM
        denom = jnp.where(lane_hi, l_scr[1], l_scr[0])
        o_ref[0] = acc_scr[...] / denom


def _fox_attn_prompt(q, k, v, ct):
    B, L, _ = q.shape
    tq, tk = ATT_TQ, ATT_TK
    assert tq == tk and L % tq == 0
    nq = L // tq
    pairs = [(a, b) for a in range(nq) for b in range(a + 1)]
    qi_tab = jnp.asarray(np.array([p[0] for p in pairs], np.int32))
    ki_tab = jnp.asarray(np.array([p[1] for p in pairs], np.int32))
    grid_spec = pltpu.PrefetchScalarGridSpec(
        num_scalar_prefetch=2,
        grid=(B, C_HEADS // 2, len(pairs)),
        in_specs=[
            pl.BlockSpec((1, tq, LANES), lambda b, hp, s, qt, kt: (b, qt[s], hp)),
            pl.BlockSpec((1, tk, LANES), lambda b, hp, s, qt, kt: (b, kt[s], hp)),
            pl.BlockSpec((1, tk, LANES), lambda b, hp, s, qt, kt: (b, kt[s], hp)),
            pl.BlockSpec((1, C_HEADS, tk), lambda b, hp, s, qt, kt: (b, 0, kt[s])),
        ],
        out_specs=pl.BlockSpec((1, tq, LANES), lambda b, hp, s, qt, kt: (b, qt[s], hp)),
        scratch_shapes=[
            pltpu.VMEM((2, tq, 1), F32),
            pltpu.VMEM((2, tq, 1), F32),
            pltpu.VMEM((tq, LANES), F32),
        ],
    )
    return pl.pallas_call(
        _fox_attn_kernel,
        grid_spec=grid_spec,
        out_shape=jax.ShapeDtypeStruct((B, L, C_WIDTH), F32),
        compiler_params=_cparams(3),
        name="fox_attn_prompt",
    )(qi_tab, ki_tab, q, k, v, ct)


def _fox_decode_kernel(pt_ref, q_ref, kn_ref, vn_ref, lfn_ref, *refs):
    PP = PAGES_PER_STEP
    k_refs = refs[0:PP]
    v_refs = refs[PP:2 * PP]
    lf_refs = refs[2 * PP:3 * PP]
    o_ref = refs[3 * PP]
    qbd, m_scr, l_scr, r_scr, acc = refs[3 * PP + 1:]
    step = pl.program_id(1)
    nsteps = pl.num_programs(1)

    head_of_lane = lax.broadcasted_iota(jnp.int32, (C_HEADS, C_WIDTH), 1) // C_HEAD_DIM
    head_of_row = lax.broadcasted_iota(jnp.int32, (C_HEADS, C_WIDTH), 0)
    blockdiag = head_of_lane == head_of_row
    eye = (lax.broadcasted_iota(jnp.int32, (C_HEADS, C_HEADS), 0)
           == lax.broadcasted_iota(jnp.int32, (C_HEADS, C_HEADS), 1))

    def to_col(row):
        return jnp.sum(jnp.where(eye, row, 0.0), axis=1, keepdims=True)

    @pl.when(step == 0)
    def _():
        qrow = q_ref[0].astype(F32)
        qb = jnp.where(blockdiag, qrow, 0.0).astype(BF16)
        qbd[...] = qb
        kn = jnp.broadcast_to(kn_ref[0], (SUBLANES, C_WIDTH)).astype(BF16)
        s_self = _dot_nt(kn, qb)[0:1, :]
        m_scr[...] = s_self
        l_scr[...] = jnp.ones_like(l_scr)
        r_scr[...] = lfn_ref[0]
        acc[...] = jnp.broadcast_to(vn_ref[0], (C_HEADS, C_WIDTH))

    upper = (lax.broadcasted_iota(jnp.int32, (PAGE_SIZE, PAGE_SIZE), 1)
             > lax.broadcasted_iota(jnp.int32, (PAGE_SIZE, PAGE_SIZE), 0)).astype(BF16)
    qb = qbd[...]
    s_list, r = [], r_scr[...]
    for i in range(PP):
        kp = k_refs[i][0].astype(BF16)
        lf = lf_refs[i][0]
        hi, lo = _split_bf16(lf)
        suffix = _dot(upper, hi) + _dot(upper, lo)
        s_list.append(_dot_nt(kp, qb) + suffix + r)
        r = r + suffix[0:1, :] + lf[0:1, :]
    r_scr[...] = r
    s = jnp.concatenate(s_list, axis=0)
    m_old = m_scr[...]
    m_new = jnp.maximum(m_old, jnp.max(s, axis=0, keepdims=True))
    alpha = jnp.exp(m_old - m_new)
    p = jnp.exp(s - m_new)
    l_scr[...] = alpha * l_scr[...] + jnp.sum(p, axis=0, keepdims=True)
    m_scr[...] = m_new
    pb = p.astype(BF16)
    pv = jnp.zeros((C_HEADS, C_WIDTH), F32)
    for i in range(PP):
        vp = v_refs[i][0].astype(BF16)
        pv = pv + _dot_tn(pb[i * PAGE_SIZE:(i + 1) * PAGE_SIZE, :], vp)
    acc[...] = to_col(alpha) * acc[...] + pv

    @pl.when(step == nsteps - 1)
    def _():
        full = acc[...] / to_col(l_scr[...])
        o_ref[0] = jnp.sum(jnp.where(blockdiag, full, 0.0), axis=0, keepdims=True)


def _fox_attn_sample(q, k_new, v_new, lf_new, cache_k, cache_v, cache_logf, page_table):
    n, n_pages = page_table.shape
    n_pool = cache_k.shape[0]
    PP = PAGES_PER_STEP
    assert n_pages % PP == 0
    ck = cache_k.reshape(n_pool, PAGE_SIZE, C_WIDTH)
    cv = cache_v.reshape(n_pool, PAGE_SIZE, C_WIDTH)

    def page_map(i):
        def index_map(b, s, pt):
            return (pt[b, n_pages - 1 - (s * PP + i)], 0, 0)
        return index_map

    row = lambda width: pl.BlockSpec((1, 1, width), lambda b, s, pt: (b, 0, 0))
    in_specs = [row(C_WIDTH), row(C_WIDTH), row(C_WIDTH), row(C_HEADS)]
    in_specs += [pl.BlockSpec((1, PAGE_SIZE, C_WIDTH), page_map(i)) for i in range(PP)]
    in_specs += [pl.BlockSpec((1, PAGE_SIZE, C_WIDTH), page_map(i)) for i in range(PP)]
    in_specs += [pl.BlockSpec((1, PAGE_SIZE, C_HEADS), page_map(i)) for i in range(PP)]
    grid_spec = pltpu.PrefetchScalarGridSpec(
        num_scalar_prefetch=1,
        grid=(n, n_pages // PP),
        in_specs=in_specs,
        out_specs=pl.BlockSpec((1, 1, C_WIDTH), lambda b, s, pt: (b, 0, 0)),
        scratch_shapes=[
            pltpu.VMEM((C_HEADS, C_WIDTH), BF16),
            pltpu.VMEM((1, C_HEADS), F32),
            pltpu.VMEM((1, C_HEADS), F32),
            pltpu.VMEM((1, C_HEADS), F32),
            pltpu.VMEM((C_HEADS, C_WIDTH), F32),
        ],
    )
    out = pl.pallas_call(
        _fox_decode_kernel,
        grid_spec=grid_spec,
        out_shape=jax.ShapeDtypeStruct((n, 1, C_WIDTH), F32),
        compiler_params=_cparams(2),
        name="fox_attn_sample",
    )(page_table, q.reshape(n, 1, C_WIDTH), k_new.reshape(n, 1, C_WIDTH),
      v_new.reshape(n, 1, C_WIDTH), lf_new.reshape(n, 1, C_HEADS),
      *([ck] * PP), *([cv] * PP), *([cache_logf] * PP))
    return out.reshape(n, C_WIDTH)


def _gate_out_kernel(att_ref, gate_ref, x_ref, w_ref, y_ref):
    mixed = (att_ref[...] * jax.nn.silu(gate_ref[...])).astype(BF16)
    y_ref[...] = _dot(mixed, w_ref[...]) + x_ref[...]


def _gate_out(att, gate, x, wout, tile):
    n = att.shape[0]
    assert n % tile == 0
    row = pl.BlockSpec((tile, D_MODEL), lambda i: (i, 0))
    return pl.pallas_call(
        _gate_out_kernel,
        grid=(n // tile,),
        in_specs=[row, row, row, pl.BlockSpec((C_WIDTH, D_MODEL), lambda i: (0, 0))],
        out_specs=row,
        out_shape=jax.ShapeDtypeStruct((n, D_MODEL), F32),
        compiler_params=_cparams(1),
        name="gate_out",
    )(att, gate, x, wout)


def _block_diag(w):
    nb, d, _ = w.shape
    eye = jnp.eye(nb, dtype=w.dtype)
    return (eye[:, None, :, None] * w[:, :, None, :]).reshape(nb * d, nb * d)


def _prepare_weights(norm0_g, w_in0, gmlp_v_g, gmlp_w_s, gmlp_b_s, lru_conv_w, lru_conv_b,
                     lru_w_r, lru_b_r, lru_w_i, lru_b_i, lru_lambda, w_out0, norm1_g, w_in1,
                     fox_b_f, q_norm_g, k_norm_g, w_out1):
    lane = np.arange(C_WIDTH) // C_HEAD_DIM
    seg = (lane[:, None] == np.arange(LANES)[None, :]).astype(np.float32)
    return {
        "norm0_g": norm0_g.reshape(1, D_MODEL),
        "win0": w_in0.astype(BF16),
        "vg": gmlp_v_g.reshape(1, A_WIDTH),
        "ws": gmlp_w_s,
        "bst": gmlp_b_s.T,
        "cw": lru_conv_w,
        "cb": lru_conv_b.reshape(1, B_WIDTH),
        "wr": _block_diag(lru_w_r).astype(BF16),
        "br": lru_b_r.reshape(1, B_WIDTH),
        "wi": _block_diag(lru_w_i).astype(BF16),
        "bi": lru_b_i.reshape(1, B_WIDTH),
        "lam": lru_lambda.reshape(1, B_WIDTH),
        "wout0": w_out0.astype(BF16),
        "norm1_g": norm1_g.reshape(1, D_MODEL),
        "win1": w_in1[:, 0:4 * C_WIDTH].astype(BF16),
        "wf": jnp.pad(w_in1[:, 4 * C_WIDTH:], ((0, 0), (0, LANES - C_HEADS))).astype(BF16),
        "bf": jnp.pad(fox_b_f, (0, LANES - C_HEADS)).reshape(1, LANES),
        "qg": jnp.tile(q_norm_g, C_HEADS).reshape(1, C_WIDTH),
        "kg": jnp.tile(k_norm_g, C_HEADS).reshape(1, C_WIDTH),
        "seg": jnp.asarray(seg, BF16),
        "exp": jnp.asarray(seg.T, BF16),
        "wout1": w_out1.astype(BF16),
    }


def kernel(x_prompt, x_sample, state_lru_conv, state_lru_h, cache_k, cache_v, cache_logf, page_table, norm0_g, w_in0, gmlp_v_g, gmlp_w_s, gmlp_b_s, lru_conv_w, lru_conv_b, lru_w_r, lru_b_r, lru_w_i, lru_b_i, lru_lambda, w_out0, norm1_g, w_in1, fox_b_f, q_norm_g, k_norm_g, w_out1):
    Bp, L, _ = x_prompt.shape
    Bs = x_sample.shape[0]
    w = _prepare_weights(norm0_g, w_in0, gmlp_v_g, gmlp_w_s, gmlp_b_s, lru_conv_w, lru_conv_b,
                         lru_w_r, lru_b_r, lru_w_i, lru_b_i, lru_lambda, w_out0, norm1_g, w_in1,
                         fox_b_f, q_norm_g, k_norm_g, w_out1)

    conv0 = jnp.zeros((Bp, CONV_W - 1, B_WIDTH), F32)
    h0 = jnp.zeros((Bp, B_WIDTH), F32)
    yp0, lru_conv_p, lru_h_p = _layer0_prompt(x_prompt, conv0, h0, w)
    ys0, gmlp_v_s, conv_s, lru_h_s = _layer0_sample(
        x_sample.reshape(Bs, D_MODEL), state_lru_conv, state_lru_h, w)

    q_p, k_p, v_p, g_p, logf_p, ct_p = _fox_proj_prompt(yp0, w)
    att_p = _fox_attn_prompt(q_p, k_p, v_p, ct_p)
    yp = _gate_out(att_p.reshape(Bp * L, C_WIDTH), g_p.reshape(Bp * L, C_WIDTH),
                   yp0.reshape(Bp * L, D_MODEL), w["wout1"], OUT_TILE).reshape(Bp, L, D_MODEL)

    q_s, k_s, v_s, g_s, logf_s = _fox_proj_sample(ys0, w)
    att_s = _fox_attn_sample(q_s, k_s, v_s, logf_s, cache_k, cache_v, cache_logf, page_table)
    ys = _gate_out(att_s, g_s, ys0, w["wout1"], Bs)

    return (yp, ys.reshape(Bs, 1, D_MODEL), lru_conv_p, lru_h_p,
            k_p.reshape(Bp, L, C_HEADS, C_HEAD_DIM), v_p.reshape(Bp, L, C_HEADS, C_HEAD_DIM), logf_p,
            gmlp_v_s.reshape(Bs, 1, A_WIDTH), conv_s.reshape(Bs, CONV_W - 1, B_WIDTH), lru_h_s,
            k_s.reshape(Bs, 1, C_HEADS, C_HEAD_DIM), v_s.reshape(Bs, 1, C_HEADS, C_HEAD_DIM),
            logf_s.reshape(Bs, 1, C_HEADS))
```

```python
import jax
import jax.numpy as jnp
import numpy as np
from jax import lax
from jax.experimental import pallas as pl
from jax.experimental.pallas import tpu as pltpu

D_MODEL = 1024
A_WIDTH = 512
A_GROUPS = 4
A_GROUP_DIM = 128
CHUNK = 128
B_WIDTH = 512
B_BLOCKS = 8
B_BLOCK_DIM = 64
CONV_W = 4
LRU_C = 8.0
C_HEADS = 16
C_HEAD_DIM = 64
C_WIDTH = 1024
PAGE_SIZE = 128
ATTN_SCALE = C_HEAD_DIM ** -0.5
EPS = 1e-6

LANES = 128
SUBLANES = 8
NEG_BIG = -1e30

F32 = jnp.float32
BF16 = jnp.bfloat16

L0_TILE = 256
PROJ_TILE = 256
ATT_T = 512
OUT_TILE = 256
PAGES_PER_STEP = 8
AUG_ROWS = 128
N_BIAS = 3
VMEM_LIMIT = 48 * 1024 * 1024


def _cparams(n_grid_dims):
    return pltpu.CompilerParams(
        dimension_semantics=("arbitrary",) * n_grid_dims,
        vmem_limit_bytes=VMEM_LIMIT,
    )


def _dot(a, b):
    return jnp.dot(a, b, preferred_element_type=F32)


def _dot_nt(a, b):
    return lax.dot_general(a, b, (((1,), (1,)), ((), ())), preferred_element_type=F32)


def _split_bf16(x):
    hi = x.astype(BF16)
    lo = (x - hi.astype(F32)).astype(BF16)
    return hi, lo


def _rms_rows(x, gain):
    ms = jnp.mean(x * x, axis=-1, keepdims=True)
    return x * lax.rsqrt(ms + EPS) * gain


def _gmlp_v_rows(pv, vg_ref):
    v = jax.nn.gelu(pv)
    parts = []
    for g in range(A_GROUPS):
        sl = slice(g * A_GROUP_DIM, (g + 1) * A_GROUP_DIM)
        parts.append(_rms_rows(v[:, sl], vg_ref[:, sl]))
    return jnp.concatenate(parts, axis=-1)


def _lru_gates(xc, wr_ref, br_ref, wi_ref, bi_ref, lam_ref):
    xcb = xc.astype(BF16)
    r = jax.nn.sigmoid(_dot(xcb, wr_ref[...]) + br_ref[...])
    gi = jax.nn.sigmoid(_dot(xcb, wi_ref[...]) + bi_ref[...])
    log_a = -LRU_C * r * jax.nn.softplus(-lam_ref[...])
    a = jnp.exp(log_a)
    bterm = jnp.sqrt(-jnp.tanh(log_a) * (1.0 + a * a)) * (gi * xc)
    return a, bterm


def _layer0_prompt_kernel(x_ref, g_ref, win_ref, vg_ref, ws_ref, bst_ref, cw_ref, cb_ref,
                          wr_ref, br_ref, wi_ref, bi_ref, lam_ref, wout_ref, conv0_ref, h0_ref,
                          y_ref, convo_ref, ho_ref,
                          xbuf, hcar, a8, b8, s_scr):
    T = L0_TILE
    i = pl.program_id(1)

    @pl.when(i == 0)
    def _():
        xbuf[0:SUBLANES, :] = jnp.zeros((SUBLANES, B_WIDTH), F32)
        xbuf[SUBLANES - (CONV_W - 1):SUBLANES, :] = conv0_ref[0]
        hcar[...] = h0_ref[0]

    x = x_ref[0]
    xn = _rms_rows(x, g_ref[...]).astype(BF16)

    def proj(k):
        return _dot(xn, win_ref[:, k * 512:(k + 1) * 512])

    vn = _gmlp_v_rows(proj(1), vg_ref).astype(BF16)
    tri = (lax.broadcasted_iota(jnp.int32, (CHUNK, CHUNK), 0)
           >= lax.broadcasted_iota(jnp.int32, (CHUNK, CHUNK), 1))
    for g in range(A_GROUPS):
        wg = jnp.where(tri, ws_ref[g], 0.0).astype(BF16)
        bias = bst_ref[:, g:g + 1]
        for c in range(T // CHUNK):
            blk = vn[c * CHUNK:(c + 1) * CHUNK, g * A_GROUP_DIM:(g + 1) * A_GROUP_DIM]
            s_scr[c * CHUNK:(c + 1) * CHUNK, g * A_GROUP_DIM:(g + 1) * A_GROUP_DIM] = (
                _dot(wg, blk) + bias)
    u = jax.nn.gelu(proj(0))
    mix_a = (u * s_scr[...] * jax.nn.silu(proj(2))).astype(BF16)

    xb = proj(3)
    xbuf[SUBLANES:SUBLANES + T, :] = xb
    xc = cb_ref[...] + cw_ref[3:4, :] * xb
    for tap in range(CONV_W - 1):
        off = SUBLANES - (CONV_W - 1) + tap
        xc = xc + cw_ref[tap:tap + 1, :] * xbuf[off:off + T, :]
    convo_ref[0] = xbuf[T + SUBLANES - (CONV_W - 1):T + SUBLANES, :]
    xbuf[0:SUBLANES, :] = xbuf[T:T + SUBLANES, :]

    a, bt = _lru_gates(xc, wr_ref, br_ref, wi_ref, bi_ref, lam_ref)

    row = lax.broadcasted_iota(jnp.int32, (T, B_WIDTH), 0) & (SUBLANES - 1)
    shift = 1
    while shift < SUBLANES:
        a_sh = pltpu.roll(a, shift, axis=0)
        b_sh = pltpu.roll(bt, shift, axis=0)
        m = row >= shift
        bt = jnp.where(m, a * b_sh + bt, bt)
        a = jnp.where(m, a * a_sh, a)
        shift *= 2
    a8[...] = a
    b8[...] = bt

    def group_step(j, h):
        off = pl.multiple_of(j * SUBLANES, SUBLANES)
        rows = a8[pl.ds(off, SUBLANES), :] * h + b8[pl.ds(off, SUBLANES), :]
        b8[pl.ds(off, SUBLANES), :] = rows
        return rows[SUBLANES - 1:SUBLANES, :]

    h_last = lax.fori_loop(0, T // SUBLANES, group_step, hcar[...], unroll=True)
    hcar[...] = h_last
    ho_ref[0] = h_last

    mix_b = (b8[...] * jax.nn.silu(proj(4))).astype(BF16)

    y = _dot(mix_a, wout_ref[0:A_WIDTH, :]) + _dot(mix_b, wout_ref[A_WIDTH:, :]) + x
    y_ref[0] = y


def _layer0_prompt(x, conv0, h0, w):
    B, L, _ = x.shape
    T = L0_TILE
    assert L % T == 0 and T % CHUNK == 0
    nt = L // T
    full = lambda shape: pl.BlockSpec(shape, lambda b, i: (0,) * len(shape))
    in_specs = [
        pl.BlockSpec((1, T, D_MODEL), lambda b, i: (b, i, 0)),
        full((1, D_MODEL)),
        full(w["win0"].shape),
        full((1, A_WIDTH)),
        full((A_GROUPS, CHUNK, CHUNK)),
        full((CHUNK, A_GROUPS)),
        full((CONV_W, B_WIDTH)),
        full((1, B_WIDTH)),
        full((B_WIDTH, B_WIDTH)), full((1, B_WIDTH)),
        full((B_WIDTH, B_WIDTH)), full((1, B_WIDTH)),
        full((1, B_WIDTH)),
        full((D_MODEL, D_MODEL)),
        pl.BlockSpec((1, CONV_W - 1, B_WIDTH), lambda b, i: (b, 0, 0)),
        pl.BlockSpec((1, 1, B_WIDTH), lambda b, i: (b, 0, 0)),
    ]
    out_specs = [
        pl.BlockSpec((1, T, D_MODEL), lambda b, i: (b, i, 0)),
        pl.BlockSpec((1, CONV_W - 1, B_WIDTH), lambda b, i: (b, 0, 0)),
        pl.BlockSpec((1, 1, B_WIDTH), lambda b, i: (b, 0, 0)),
    ]
    out_shape = [
        jax.ShapeDtypeStruct((B, L, D_MODEL), F32),
        jax.ShapeDtypeStruct((B, CONV_W - 1, B_WIDTH), F32),
        jax.ShapeDtypeStruct((B, 1, B_WIDTH), F32),
    ]
    y, convo, ho = pl.pallas_call(
        _layer0_prompt_kernel,
        grid=(B, nt),
        in_specs=in_specs,
        out_specs=out_specs,
        out_shape=out_shape,
        scratch_shapes=[
            pltpu.VMEM((T + SUBLANES, B_WIDTH), F32),
            pltpu.VMEM((1, B_WIDTH), F32),
            pltpu.VMEM((T, B_WIDTH), F32),
            pltpu.VMEM((T, B_WIDTH), F32),
            pltpu.VMEM((T, A_WIDTH), F32),
        ],
        compiler_params=_cparams(2),
        name="layer0_prompt",
    )(x, w["norm0_g"], w["win0"], w["vg"], w["ws"], w["bst"], w["cw"], w["cb"],
      w["wr"], w["br"], w["wi"], w["bi"], w["lam"], w["wout0"], conv0, h0.reshape(B, 1, B_WIDTH))
    return y, convo, ho.reshape(B, B_WIDTH)


def _layer0_sample_kernel(x_ref, g_ref, win_ref, vg_ref, ws_ref, bst_ref, cw_ref, cb_ref,
                          wr_ref, br_ref, wi_ref, bi_ref, lam_ref, wout_ref, conv_ref, h_ref,
                          y_ref, v_ref, convo_ref, ho_ref):
    x = x_ref[...]
    xn = _rms_rows(x, g_ref[...]).astype(BF16)

    def proj(k):
        return _dot(xn, win_ref[:, k * 512:(k + 1) * 512])

    vn = _gmlp_v_rows(proj(1), vg_ref)
    v_ref[...] = vn
    s_parts = []
    for g in range(A_GROUPS):
        sl = slice(g * A_GROUP_DIM, (g + 1) * A_GROUP_DIM)
        s_parts.append(ws_ref[g, 0:1, 0:1] * vn[:, sl] + bst_ref[0:1, g:g + 1])
    s = jnp.concatenate(s_parts, axis=-1)
    mix_a = (jax.nn.gelu(proj(0)) * s * jax.nn.silu(proj(2))).astype(BF16)

    xb = proj(3)
    xc = cb_ref[...] + cw_ref[3:4, :] * xb
    for tap in range(CONV_W - 1):
        xc = xc + cw_ref[tap:tap + 1, :] * conv_ref[:, tap * B_WIDTH:(tap + 1) * B_WIDTH]
    for tap in range(CONV_W - 2):
        convo_ref[:, tap * B_WIDTH:(tap + 1) * B_WIDTH] = (
            conv_ref[:, (tap + 1) * B_WIDTH:(tap + 2) * B_WIDTH])
    convo_ref[:, (CONV_W - 2) * B_WIDTH:] = xb

    a, bt = _lru_gates(xc, wr_ref, br_ref, wi_ref, bi_ref, lam_ref)
    h = a * h_ref[...] + bt
    ho_ref[...] = h
    mix_b = (h * jax.nn.silu(proj(4))).astype(BF16)
    y_ref[...] = _dot(mix_a, wout_ref[0:A_WIDTH, :]) + _dot(mix_b, wout_ref[A_WIDTH:, :]) + x


def _layer0_sample(x, conv, h, w):
    n = x.shape[0]
    out_shape = [
        jax.ShapeDtypeStruct((n, D_MODEL), F32),
        jax.ShapeDtypeStruct((n, A_WIDTH), F32),
        jax.ShapeDtypeStruct((n, (CONV_W - 1) * B_WIDTH), F32),
        jax.ShapeDtypeStruct((n, B_WIDTH), F32),
    ]
    return pl.pallas_call(
        _layer0_sample_kernel,
        out_shape=out_shape,
        compiler_params=pltpu.CompilerParams(vmem_limit_bytes=VMEM_LIMIT),
        name="layer0_sample",
    )(x, w["norm0_g"], w["win0"], w["vg"], w["ws"], w["bst"], w["cw"], w["cb"],
      w["wr"], w["br"], w["wi"], w["bi"], w["lam"], w["wout0"],
      conv.reshape(n, (CONV_W - 1) * B_WIDTH), h)


def _head_norm_cols(t, gain_col):
    ms = jnp.mean(t * t, axis=0, keepdims=True)
    return t * lax.rsqrt(ms + EPS) * gain_col


def _fox_proj_prompt_kernel(x_ref, g_ref, wt_ref, wft_ref, bf_ref, qg_ref, kg_ref,
                            qaug_ref, kaug_ref, kt_ref, vt_ref, gt_ref, lf_ref, carry):
    T = PROJ_TILE
    i = pl.program_id(1)

    @pl.when(i == 0)
    def _():
        carry[...] = jnp.zeros_like(carry)

    xn = _rms_rows(x_ref[0], g_ref[...])
    xnt = xn.T.astype(BF16)

    def proj_t(k):
        return _dot(wt_ref[k * C_WIDTH:(k + 1) * C_WIDTH, :], xnt)

    lf = jax.nn.log_sigmoid(_dot(wft_ref[...], xnt) + bf_ref[...])
    lf_ref[0] = lf
    upper = (lax.broadcasted_iota(jnp.int32, (T, T), 0)
             <= lax.broadcasted_iota(jnp.int32, (T, T), 1)).astype(BF16)
    hi, lo = _split_bf16(lf)
    c = _dot(hi, upper) + _dot(lo, upper) + carry[...]
    carry[...] = c[:, T - 1:T]
    pieces, rest = [], -c
    for _ in range(N_BIAS):
        p = rest.astype(BF16).astype(F32)
        pieces.append(p)
        rest = rest - p

    sub = lax.broadcasted_iota(jnp.int32, (SUBLANES, T), 0)
    ones_rows = jnp.where(sub < N_BIAS, 1.0, 0.0)
    pad = jnp.zeros((AUG_ROWS - C_HEAD_DIM - SUBLANES, T), F32)
    qt = proj_t(0)
    kt = proj_t(1)
    for h in range(C_HEADS):
        sl = slice(h * C_HEAD_DIM, (h + 1) * C_HEAD_DIM)
        qn = _head_norm_cols(qt[sl], qg_ref[...]) * ATTN_SCALE
        kn = _head_norm_cols(kt[sl], kg_ref[...])
        kt_ref[0, h] = kn
        qaug_ref[0, h, 0] = jnp.concatenate([qn, ones_rows, pad], axis=0).astype(BF16)
        bias_rows = jnp.zeros((SUBLANES, T), F32)
        for j in range(N_BIAS):
            bias_rows = jnp.where(sub == j, pieces[j][h:h + 1, :], bias_rows)
        kaug_t = jnp.concatenate([kn, bias_rows, pad], axis=0)
        kaug_ref[0, h] = kaug_t.T.astype(BF16)
    vt_ref[0] = proj_t(2).reshape(C_HEADS, C_HEAD_DIM, T)
    gt_ref[0] = proj_t(3)


def _fox_proj_prompt(x, w):
    B, L, _ = x.shape
    T = PROJ_TILE
    assert L % ATT_T == 0 and ATT_T % T == 0
    per_q = ATT_T // T
    full = lambda shape: pl.BlockSpec(shape, lambda b, i: (0,) * len(shape))
    in_specs = [
        pl.BlockSpec((1, T, D_MODEL), lambda b, i: (b, i, 0)),
        full((1, D_MODEL)), full((4 * C_WIDTH, D_MODEL)), full((C_HEADS, D_MODEL)),
        full((C_HEADS, 1)), full((C_HEAD_DIM, 1)), full((C_HEAD_DIM, 1)),
    ]
    out_specs = [
        pl.BlockSpec((1, C_HEADS, 1, AUG_ROWS, T), lambda b, i: (b, 0, i // per_q, 0, i % per_q)),
        pl.BlockSpec((1, C_HEADS, T, AUG_ROWS), lambda b, i: (b, 0, i, 0)),
        pl.BlockSpec((1, C_HEADS, C_HEAD_DIM, T), lambda b, i: (b, 0, 0, i)),
        pl.BlockSpec((1, C_HEADS, C_HEAD_DIM, T), lambda b, i: (b, 0, 0, i)),
        pl.BlockSpec((1, C_WIDTH, T), lambda b, i: (b, 0, i)),
        pl.BlockSpec((1, C_HEADS, T), lambda b, i: (b, 0, i)),
    ]
    out_shape = [
        jax.ShapeDtypeStruct((B, C_HEADS, L // ATT_T, AUG_ROWS, ATT_T), BF16),
        jax.ShapeDtypeStruct((B, C_HEADS, L, AUG_ROWS), BF16),
        jax.ShapeDtypeStruct((B, C_HEADS, C_HEAD_DIM, L), F32),
        jax.ShapeDtypeStruct((B, C_HEADS, C_HEAD_DIM, L), F32),
        jax.ShapeDtypeStruct((B, C_WIDTH, L), F32),
        jax.ShapeDtypeStruct((B, C_HEADS, L), F32),
    ]
    return pl.pallas_call(
        _fox_proj_prompt_kernel,
        grid=(B, L // T),
        in_specs=in_specs,
        out_specs=out_specs,
        out_shape=out_shape,
        scratch_shapes=[pltpu.VMEM((C_HEADS, 1), F32)],
        compiler_params=_cparams(2),
        name="fox_proj_prompt",
    )(x, w["norm1_g"], w["win1_t"], w["wf_t"], w["bf_col"], w["qg_col"], w["kg_col"])


def _fox_attn_kernel(qaug_ref, kaug_ref, vt_ref, o_ref, m_scr, l_scr, acc_scr):
    ki = pl.program_id(2)
    nk = pl.num_programs(2)
    nq = qaug_ref.shape[2]
    t = ATT_T

    @pl.when(ki == 0)
    def _():
        m_scr[...] = jnp.full(m_scr.shape, NEG_BIG, F32)
        l_scr[...] = jnp.zeros(l_scr.shape, F32)
        acc_scr[...] = jnp.zeros(acc_scr.shape, F32)

    kb = kaug_ref[0, 0]
    vb = vt_ref[0, 0].astype(BF16)

    def block(qi, masked):
        s = _dot(kb, qaug_ref[0, 0, qi])
        if masked:
            causal = (lax.broadcasted_iota(jnp.int32, (t, t), 0)
                      <= lax.broadcasted_iota(jnp.int32, (t, t), 1))
            s = jnp.where(causal, s, NEG_BIG)
        m_old = m_scr[qi]
        m_new = jnp.maximum(m_old, jnp.max(s, axis=0, keepdims=True))
        alpha = jnp.exp(m_old - m_new)
        p = jnp.exp(s - m_new)
        l_scr[qi] = alpha * l_scr[qi] + jnp.sum(p, axis=0, keepdims=True)
        m_scr[qi] = m_new
        acc_scr[qi] = alpha * acc_scr[qi] + _dot(vb, p.astype(BF16))

    block(ki, True)

    def later(qi, carry):
        block(qi, False)
        return carry

    lax.fori_loop(ki + 1, nq, later, 0)

    @pl.when(ki == nk - 1)
    def _():
        for qi in range(nq):
            o_ref[0, 0, :, qi * t:(qi + 1) * t] = acc_scr[qi] / l_scr[qi]


def _fox_attn_prompt(qaug, kaug, vt):
    B, H, nq, _, t = qaug.shape
    L = nq * t
    return pl.pallas_call(
        _fox_attn_kernel,
        grid=(B, H, nq),
        in_specs=[
            pl.BlockSpec((1, 1, nq, AUG_ROWS, t), lambda b, h, k: (b, h, 0, 0, 0)),
            pl.BlockSpec((1, 1, t, AUG_ROWS), lambda b, h, k: (b, h, k, 0)),
            pl.BlockSpec((1, 1, C_HEAD_DIM, t), lambda b, h, k: (b, h, 0, k)),
        ],
        out_specs=pl.BlockSpec((1, 1, C_HEAD_DIM, L), lambda b, h, k: (b, h, 0, 0)),
        out_shape=jax.ShapeDtypeStruct((B, H, C_HEAD_DIM, L), F32),
        scratch_shapes=[
            pltpu.VMEM((nq, 1, t), F32),
            pltpu.VMEM((nq, 1, t), F32),
            pltpu.VMEM((nq, C_HEAD_DIM, t), F32),
        ],
        compiler_params=_cparams(3),
        name="fox_attn_prompt",
    )(qaug, kaug, vt)


def _gate_out_t_kernel(att_ref, gate_ref, x_ref, wt_ref, y_ref):
    mixed = (att_ref[0] * jax.nn.silu(gate_ref[0])).astype(BF16)
    y_ref[0] = _dot(wt_ref[...], mixed).T + x_ref[0]


def _gate_out_t(att_t, gate_t, x, wout_t):
    B, L, _ = x.shape
    T = OUT_TILE
    assert L % T == 0
    col = pl.BlockSpec((1, C_WIDTH, T), lambda b, i: (b, 0, i))
    row = pl.BlockSpec((1, T, D_MODEL), lambda b, i: (b, i, 0))
    return pl.pallas_call(
        _gate_out_t_kernel,
        grid=(B, L // T),
        in_specs=[col, col, row, pl.BlockSpec((D_MODEL, C_WIDTH), lambda b, i: (0, 0))],
        out_specs=row,
        out_shape=jax.ShapeDtypeStruct((B, L, D_MODEL), F32),
        compiler_params=_cparams(2),
        name="gate_out_prompt",
    )(att_t, gate_t, x, wout_t)


def _fox_proj_sample_kernel(x_ref, g_ref, w_ref, wf_ref, bf_ref, qg_ref, kg_ref, seg_ref, exp_ref,
                            q_ref, k_ref, v_ref, gate_ref, lf_ref):
    xn = _rms_rows(x_ref[...], g_ref[...]).astype(BF16)

    def head_norm(t, gain):
        ssq = _dot((t * t).astype(BF16), seg_ref[...])
        rs = lax.rsqrt(ssq * (1.0 / C_HEAD_DIM) + EPS)
        hi, lo = _split_bf16(rs)
        rs_full = _dot(hi, exp_ref[...]) + _dot(lo, exp_ref[...])
        return t * rs_full * gain

    q_ref[...] = head_norm(_dot(xn, w_ref[:, 0:C_WIDTH]), qg_ref[...]) * ATTN_SCALE
    k_ref[...] = head_norm(_dot(xn, w_ref[:, C_WIDTH:2 * C_WIDTH]), kg_ref[...])
    v_ref[...] = _dot(xn, w_ref[:, 2 * C_WIDTH:3 * C_WIDTH])
    gate_ref[...] = _dot(xn, w_ref[:, 3 * C_WIDTH:4 * C_WIDTH])
    logf = jax.nn.log_sigmoid(_dot(xn, wf_ref[...]) + bf_ref[...])
    lf_ref[...] = logf[:, 0:C_HEADS]


def _fox_proj_sample(x, w):
    n = x.shape[0]
    out_shape = [
        jax.ShapeDtypeStruct((n, C_WIDTH), F32),
        jax.ShapeDtypeStruct((n, C_WIDTH), F32),
        jax.ShapeDtypeStruct((n, C_WIDTH), F32),
        jax.ShapeDtypeStruct((n, C_WIDTH), F32),
        jax.ShapeDtypeStruct((n, C_HEADS), F32),
    ]
    return pl.pallas_call(
        _fox_proj_sample_kernel,
        out_shape=out_shape,
        compiler_params=pltpu.CompilerParams(vmem_limit_bytes=VMEM_LIMIT),
        name="fox_proj_sample",
    )(x, w["norm1_g"], w["win1"], w["wf"], w["bf"], w["qg"], w["kg"], w["seg"], w["exp"])


def _fox_decode_kernel(pt_ref, qcol_ref, kcol_ref, vnew_ref, lfn_ref, *refs):
    PP = PAGES_PER_STEP
    k_refs = refs[0:PP]
    v_refs = refs[PP:2 * PP]
    lf_refs = refs[2 * PP:3 * PP]
    o_ref = refs[3 * PP]
    qb, s_scr, r_scr, ps_scr, l_scr, acc = refs[3 * PP + 1:]
    phase = pl.program_id(1)
    step = pl.program_id(2)
    nsteps = pl.num_programs(2)
    head_row = lax.broadcasted_iota(jnp.int32, (C_HEADS, PAGE_SIZE), 0)

    @pl.when(phase == 0)
    def _():
        @pl.when(step == 0)
        def _():
            qc = qcol_ref[0]
            for h in range(C_HEADS):
                qb[h] = jnp.broadcast_to(qc[:, h:h + 1], (C_HEAD_DIM, PAGE_SIZE))
            r_scr[...] = lfn_ref[0]

        lane = lax.broadcasted_iota(jnp.int32, (C_HEADS, PAGE_SIZE), 1)
        r = r_scr[...]
        for i in range(PP):
            s = jnp.zeros((C_HEADS, PAGE_SIZE), F32)
            for h in range(C_HEADS):
                sh = jnp.sum(k_refs[i][0, h] * qb[h], axis=0, keepdims=True)
                s = jnp.where(head_row == h, sh, s)
            lf = lf_refs[i][0]
            incl = lf
            shift = 1
            while shift < PAGE_SIZE:
                nxt = pltpu.roll(incl, PAGE_SIZE - shift, axis=1)
                incl = incl + jnp.where(lane + shift < PAGE_SIZE, nxt, 0.0)
                shift *= 2
            s_scr[step * PP + i] = s + (incl - lf) + r
            r = r + incl[:, 0:1]
        r_scr[...] = r

    @pl.when(phase == 1)
    def _():
        @pl.when(step == 0)
        def _():
            eye = (lax.broadcasted_iota(jnp.int32, (C_HEADS, C_HEADS), 0)
                   == lax.broadcasted_iota(jnp.int32, (C_HEADS, C_HEADS), 1))
            self_row = jnp.sum(qcol_ref[0] * kcol_ref[0], axis=0, keepdims=True)
            s_self = jnp.sum(jnp.where(eye, self_row, 0.0), axis=1, keepdims=True)
            s_all = s_scr[...]
            m = jnp.max(jnp.max(s_all, axis=0), axis=1, keepdims=True)
            m = jnp.maximum(m, s_self)
            p_all = jnp.exp(s_all - m)
            s_scr[...] = p_all
            p_self = jnp.exp(s_self - m)
            ps_scr[...] = p_self
            l_scr[...] = jnp.sum(jnp.sum(p_all, axis=0), axis=1, keepdims=True) + p_self
            acc[...] = jnp.zeros(acc.shape, F32)

        for i in range(PP):
            p = s_scr[step * PP + i]
            for h in range(C_HEADS):
                acc[h] = acc[h] + p[h:h + 1, :] * v_refs[i][0, h]

        @pl.when(step == nsteps - 1)
        def _():
            ones = jnp.ones((SUBLANES, PAGE_SIZE), BF16)
            rows = lax.broadcasted_iota(jnp.int32, (C_HEADS, C_HEAD_DIM), 0)
            out = jnp.zeros((C_HEADS, C_HEAD_DIM), F32)
            for h in range(C_HEADS):
                hi, lo = _split_bf16(acc[h])
                tot = _dot_nt(ones, hi) + _dot_nt(ones, lo)
                out = jnp.where(rows == h, tot[0:1, :], out)
            o_ref[0] = (out + ps_scr[...] * vnew_ref[0]) / l_scr[...]


def _fox_attn_sample(q, k_new, v_new, lf_new, cache_k, cache_v, cache_logf, page_table):
    n, n_pages = page_table.shape
    PP = PAGES_PER_STEP
    assert n_pages % PP == 0
    nsteps = n_pages // PP
    kt = jnp.transpose(cache_k, (0, 2, 3, 1))
    vt = jnp.transpose(cache_v, (0, 2, 3, 1))
    lft = jnp.transpose(cache_logf, (0, 2, 1))
    to_cols = lambda a: jnp.transpose(a.reshape(n, C_HEADS, C_HEAD_DIM), (0, 2, 1))

    def page_map(i, active_phase, idle_step):
        def index_map(b, ph, s, pt):
            s_eff = jnp.where(ph == active_phase, s, idle_step)
            return (pt[b, n_pages - 1 - (s_eff * PP + i)],) + (0,) * 3
        return index_map

    def lf_map(i):
        def index_map(b, ph, s, pt):
            s_eff = jnp.where(ph == 0, s, nsteps - 1)
            return (pt[b, n_pages - 1 - (s_eff * PP + i)], 0, 0)
        return index_map

    small = lambda shape: pl.BlockSpec((1,) + shape, lambda b, ph, s, pt: (b, 0, 0))
    page_block = (1, C_HEADS, C_HEAD_DIM, PAGE_SIZE)
    in_specs = [small((C_HEAD_DIM, C_HEADS)), small((C_HEAD_DIM, C_HEADS)),
                small((C_HEADS, C_HEAD_DIM)), small((C_HEADS, 1))]
    in_specs += [pl.BlockSpec(page_block, page_map(i, 0, nsteps - 1)) for i in range(PP)]
    in_specs += [pl.BlockSpec(page_block, page_map(i, 1, 0)) for i in range(PP)]
    in_specs += [pl.BlockSpec((1, C_HEADS, PAGE_SIZE), lf_map(i)) for i in range(PP)]
    grid_spec = pltpu.PrefetchScalarGridSpec(
        num_scalar_prefetch=1,
        grid=(n, 2, nsteps),
        in_specs=in_specs,
        out_specs=pl.BlockSpec((1, C_HEADS, C_HEAD_DIM), lambda b, ph, s, pt: (b, 0, 0)),
        scratch_shapes=[
            pltpu.VMEM((C_HEADS, C_HEAD_DIM, PAGE_SIZE), F32),
            pltpu.VMEM((n_pages, C_HEADS, PAGE_SIZE), F32),
            pltpu.VMEM((C_HEADS, 1), F32),
            pltpu.VMEM((C_HEADS, 1), F32),
            pltpu.VMEM((C_HEADS, 1), F32),
            pltpu.VMEM((C_HEADS, C_HEAD_DIM, PAGE_SIZE), F32),
        ],
    )
    out = pl.pallas_call(
        _fox_decode_kernel,
        grid_spec=grid_spec,
        out_shape=jax.ShapeDtypeStruct((n, C_HEADS, C_HEAD_DIM), F32),
        compiler_params=_cparams(3),
        name="fox_attn_sample",
    )(page_table, to_cols(q), to_cols(k_new), v_new.reshape(n, C_HEADS, C_HEAD_DIM),
      lf_new.reshape(n, C_HEADS, 1), *([kt] * PP), *([vt] * PP), *([lft] * PP))
    return out.reshape(n, C_WIDTH)


def _gate_out_kernel(att_ref, gate_ref, x_ref, w_ref, y_ref):
    mixed = (att_ref[...] * jax.nn.silu(gate_ref[...])).astype(BF16)
    y_ref[...] = _dot(mixed, w_ref[...]) + x_ref[...]


def _gate_out(att, gate, x, wout):
    return pl.pallas_call(
        _gate_out_kernel,
        out_shape=jax.ShapeDtypeStruct(x.shape, F32),
        compiler_params=pltpu.CompilerParams(vmem_limit_bytes=VMEM_LIMIT),
        name="gate_out_sample",
    )(att, gate, x, wout)


def _block_diag(w):
    nb, d, _ = w.shape
    eye = jnp.eye(nb, dtype=w.dtype)
    return (eye[:, None, :, None] * w[:, :, None, :]).reshape(nb * d, nb * d)


def _prepare_weights(norm0_g, w_in0, gmlp_v_g, gmlp_w_s, gmlp_b_s, lru_conv_w, lru_conv_b,
                     lru_w_r, lru_b_r, lru_w_i, lru_b_i, lru_lambda, w_out0, norm1_g, w_in1,
                     fox_b_f, q_norm_g, k_norm_g, w_out1):
    lane = np.arange(C_WIDTH) // C_HEAD_DIM
    seg = (lane[:, None] == np.arange(LANES)[None, :]).astype(np.float32)
    win1 = w_in1[:, 0:4 * C_WIDTH].astype(BF16)
    wf = w_in1[:, 4 * C_WIDTH:].astype(BF16)
    wout1 = w_out1.astype(BF16)
    return {
        "norm0_g": norm0_g.reshape(1, D_MODEL),
        "win0": w_in0.astype(BF16),
        "vg": gmlp_v_g.reshape(1, A_WIDTH),
        "ws": gmlp_w_s,
        "bst": gmlp_b_s.T,
        "cw": lru_conv_w,
        "cb": lru_conv_b.reshape(1, B_WIDTH),
        "wr": _block_diag(lru_w_r).astype(BF16),
        "br": lru_b_r.reshape(1, B_WIDTH),
        "wi": _block_diag(lru_w_i).astype(BF16),
        "bi": lru_b_i.reshape(1, B_WIDTH),
        "lam": lru_lambda.reshape(1, B_WIDTH),
        "wout0": w_out0.astype(BF16),
        "norm1_g": norm1_g.reshape(1, D_MODEL),
        "win1": win1,
        "win1_t": win1.T,
        "wf": jnp.pad(wf, ((0, 0), (0, LANES - C_HEADS))),
        "wf_t": wf.T,
        "bf": jnp.pad(fox_b_f, (0, LANES - C_HEADS)).reshape(1, LANES),
        "bf_col": fox_b_f.reshape(C_HEADS, 1),
        "qg": jnp.tile(q_norm_g, C_HEADS).reshape(1, C_WIDTH),
        "kg": jnp.tile(k_norm_g, C_HEADS).reshape(1, C_WIDTH),
        "qg_col": q_norm_g.reshape(C_HEAD_DIM, 1),
        "kg_col": k_norm_g.reshape(C_HEAD_DIM, 1),
        "seg": jnp.asarray(seg, BF16),
        "exp": jnp.asarray(seg.T, BF16),
        "wout1": wout1,
        "wout1_t": wout1.T,
    }


def kernel(x_prompt, x_sample, state_lru_conv, state_lru_h, cache_k, cache_v, cache_logf, page_table, norm0_g, w_in0, gmlp_v_g, gmlp_w_s, gmlp_b_s, lru_conv_w, lru_conv_b, lru_w_r, lru_b_r, lru_w_i, lru_b_i, lru_lambda, w_out0, norm1_g, w_in1, fox_b_f, q_norm_g, k_norm_g, w_out1):
    Bp, L, _ = x_prompt.shape
    Bs = x_sample.shape[0]
    w = _prepare_weights(norm0_g, w_in0, gmlp_v_g, gmlp_w_s, gmlp_b_s, lru_conv_w, lru_conv_b,
                         lru_w_r, lru_b_r, lru_w_i, lru_b_i, lru_lambda, w_out0, norm1_g, w_in1,
                         fox_b_f, q_norm_g, k_norm_g, w_out1)

    conv0 = jnp.zeros((Bp, CONV_W - 1, B_WIDTH), F32)
    h0 = jnp.zeros((Bp, B_WIDTH), F32)
    yp0, lru_conv_p, lru_h_p = _layer0_prompt(x_prompt, conv0, h0, w)
    ys0, gmlp_v_s, conv_s, lru_h_s = _layer0_sample(
        x_sample.reshape(Bs, D_MODEL), state_lru_conv, state_lru_h, w)

    qaug, kaug, kt_p, vt_p, gt_p, lft_p = _fox_proj_prompt(yp0, w)
    att_t = _fox_attn_prompt(qaug, kaug, vt_p)
    yp = _gate_out_t(att_t.reshape(Bp, C_WIDTH, L), gt_p, yp0, w["wout1_t"])
    k_p = jnp.transpose(kt_p, (0, 3, 1, 2))
    v_p = jnp.transpose(vt_p, (0, 3, 1, 2))
    logf_p = jnp.transpose(lft_p, (0, 2, 1))

    q_s, k_s, v_s, g_s, logf_s = _fox_proj_sample(ys0, w)
    att_s = _fox_attn_sample(q_s, k_s, v_s, logf_s, cache_k, cache_v, cache_logf, page_table)
    ys = _gate_out(att_s, g_s, ys0, w["wout1"])

    return (yp, ys.reshape(Bs, 1, D_MODEL), lru_conv_p, lru_h_p, k_p, v_p, logf_p,
            gmlp_v_s.reshape(Bs, 1, A_WIDTH), conv_s.reshape(Bs, CONV_W - 1, B_WIDTH), lru_h_s,
            k_s.reshape(Bs, 1, C_HEADS, C_HEAD_DIM), v_s.reshape(Bs, 1, C_HEADS, C_HEAD_DIM),
            logf_s.reshape(Bs, 1, C_HEADS))
```

```python
import jax
import jax.numpy as jnp
import numpy as np
from jax import lax
from jax.experimental import pallas as pl
from jax.experimental.pallas import tpu as pltpu

D_MODEL = 1024
A_WIDTH = 512
A_GROUPS = 4
A_GROUP_DIM = 128
CHUNK = 128
B_WIDTH = 512
B_BLOCKS = 8
B_BLOCK_DIM = 64
CONV_W = 4
LRU_C = 8.0
C_HEADS = 16
C_HEAD_DIM = 64
C_WIDTH = 1024
PAGE_SIZE = 128
ATTN_SCALE = C_HEAD_DIM ** -0.5
EPS = 1e-6

LANES = 128
SUBLANES = 8
NEG_BIG = -1e30

F32 = jnp.float32
BF16 = jnp.bfloat16

L0_TILE = 256
PROJ_TILE = 256
ATT_T = 512
OUT_TILE = 256
PAGES_PER_STEP = 8
MAX_SLABS = 8
AUG_ROWS = 128
V_AUG_ROWS = 80
LOG2E = 1.4426950408889634
N_BIAS = 3
VMEM_LIMIT = 48 * 1024 * 1024


def _cparams(n_grid_dims):
    return pltpu.CompilerParams(
        dimension_semantics=("arbitrary",) * n_grid_dims,
        vmem_limit_bytes=VMEM_LIMIT,
    )


def _dot(a, b):
    return jnp.dot(a, b, preferred_element_type=F32)


def _dot_nt(a, b):
    return lax.dot_general(a, b, (((1,), (1,)), ((), ())), preferred_element_type=F32)


def _split_bf16(x):
    hi = x.astype(BF16)
    lo = (x - hi.astype(F32)).astype(BF16)
    return hi, lo


def _rms_rows(x, gain):
    ms = jnp.mean(x * x, axis=-1, keepdims=True)
    return x * lax.rsqrt(ms + EPS) * gain


def _gmlp_v_rows(pv, vg_ref):
    v = jax.nn.gelu(pv)
    parts = []
    for g in range(A_GROUPS):
        sl = slice(g * A_GROUP_DIM, (g + 1) * A_GROUP_DIM)
        parts.append(_rms_rows(v[:, sl], vg_ref[:, sl]))
    return jnp.concatenate(parts, axis=-1)


def _lru_gates(xc, wr_ref, br_ref, wi_ref, bi_ref, lam_ref):
    xcb = xc.astype(BF16)
    r = jax.nn.sigmoid(_dot(xcb, wr_ref[...]) + br_ref[...])
    gi = jax.nn.sigmoid(_dot(xcb, wi_ref[...]) + bi_ref[...])
    log_a = -LRU_C * r * jax.nn.softplus(-lam_ref[...])
    a = jnp.exp(log_a)
    bterm = jnp.sqrt(-jnp.tanh(log_a) * (1.0 + a * a)) * (gi * xc)
    return a, bterm


def _layer0_prompt_kernel(x_ref, g_ref, win_ref, vg_ref, ws_ref, bst_ref, cw_ref, cb_ref,
                          wr_ref, br_ref, wi_ref, bi_ref, lam_ref, wout_ref, conv0_ref, h0_ref,
                          y_ref, convo_ref, ho_ref,
                          xbuf, hcar, a8, b8, s_scr):
    T = L0_TILE
    i = pl.program_id(1)

    @pl.when(i == 0)
    def _():
        xbuf[0:SUBLANES, :] = jnp.zeros((SUBLANES, B_WIDTH), F32)
        xbuf[SUBLANES - (CONV_W - 1):SUBLANES, :] = conv0_ref[0]
        hcar[...] = h0_ref[0]

    x = x_ref[0]
    xn = _rms_rows(x, g_ref[...]).astype(BF16)

    def proj(k):
        return _dot(xn, win_ref[:, k * 512:(k + 1) * 512])

    vn = _gmlp_v_rows(proj(1), vg_ref).astype(BF16)
    tri = (lax.broadcasted_iota(jnp.int32, (CHUNK, CHUNK), 0)
           >= lax.broadcasted_iota(jnp.int32, (CHUNK, CHUNK), 1))
    for g in range(A_GROUPS):
        wg = jnp.where(tri, ws_ref[g], 0.0).astype(BF16)
        bias = bst_ref[:, g:g + 1]
        for c in range(T // CHUNK):
            blk = vn[c * CHUNK:(c + 1) * CHUNK, g * A_GROUP_DIM:(g + 1) * A_GROUP_DIM]
            s_scr[c * CHUNK:(c + 1) * CHUNK, g * A_GROUP_DIM:(g + 1) * A_GROUP_DIM] = (
                _dot(wg, blk) + bias)
    u = jax.nn.gelu(proj(0))
    mix_a = (u * s_scr[...] * jax.nn.silu(proj(2))).astype(BF16)

    xb = proj(3)
    xbuf[SUBLANES:SUBLANES + T, :] = xb
    xc = cb_ref[...] + cw_ref[3:4, :] * xb
    for tap in range(CONV_W - 1):
        off = SUBLANES - (CONV_W - 1) + tap
        xc = xc + cw_ref[tap:tap + 1, :] * xbuf[off:off + T, :]
    convo_ref[0] = xbuf[T + SUBLANES - (CONV_W - 1):T + SUBLANES, :]
    xbuf[0:SUBLANES, :] = xbuf[T:T + SUBLANES, :]

    a, bt = _lru_gates(xc, wr_ref, br_ref, wi_ref, bi_ref, lam_ref)

    row = lax.broadcasted_iota(jnp.int32, (T, B_WIDTH), 0) & (SUBLANES - 1)
    shift = 1
    while shift < SUBLANES:
        a_sh = pltpu.roll(a, shift, axis=0)
        b_sh = pltpu.roll(bt, shift, axis=0)
        m = row >= shift
        bt = jnp.where(m, a * b_sh + bt, bt)
        a = jnp.where(m, a * a_sh, a)
        shift *= 2
    a8[...] = a
    b8[...] = bt

    def group_step(j, h):
        off = pl.multiple_of(j * SUBLANES, SUBLANES)
        rows = a8[pl.ds(off, SUBLANES), :] * h + b8[pl.ds(off, SUBLANES), :]
        b8[pl.ds(off, SUBLANES), :] = rows
        return rows[SUBLANES - 1:SUBLANES, :]

    h_last = lax.fori_loop(0, T // SUBLANES, group_step, hcar[...], unroll=True)
    hcar[...] = h_last
    ho_ref[0] = h_last

    mix_b = (b8[...] * jax.nn.silu(proj(4))).astype(BF16)

    y = _dot(mix_a, wout_ref[0:A_WIDTH, :]) + _dot(mix_b, wout_ref[A_WIDTH:, :]) + x
    y_ref[0] = y


def _layer0_prompt(x, conv0, h0, w):
    B, L, _ = x.shape
    T = L0_TILE
    assert L % T == 0 and T % CHUNK == 0
    nt = L // T
    full = lambda shape: pl.BlockSpec(shape, lambda b, i: (0,) * len(shape))
    in_specs = [
        pl.BlockSpec((1, T, D_MODEL), lambda b, i: (b, i, 0)),
        full((1, D_MODEL)),
        full(w["win0"].shape),
        full((1, A_WIDTH)),
        full((A_GROUPS, CHUNK, CHUNK)),
        full((CHUNK, A_GROUPS)),
        full((CONV_W, B_WIDTH)),
        full((1, B_WIDTH)),
        full((B_WIDTH, B_WIDTH)), full((1, B_WIDTH)),
        full((B_WIDTH, B_WIDTH)), full((1, B_WIDTH)),
        full((1, B_WIDTH)),
        full((D_MODEL, D_MODEL)),
        pl.BlockSpec((1, CONV_W - 1, B_WIDTH), lambda b, i: (b, 0, 0)),
        pl.BlockSpec((1, 1, B_WIDTH), lambda b, i: (b, 0, 0)),
    ]
    out_specs = [
        pl.BlockSpec((1, T, D_MODEL), lambda b, i: (b, i, 0)),
        pl.BlockSpec((1, CONV_W - 1, B_WIDTH), lambda b, i: (b, 0, 0)),
        pl.BlockSpec((1, 1, B_WIDTH), lambda b, i: (b, 0, 0)),
    ]
    out_shape = [
        jax.ShapeDtypeStruct((B, L, D_MODEL), F32),
        jax.ShapeDtypeStruct((B, CONV_W - 1, B_WIDTH), F32),
        jax.ShapeDtypeStruct((B, 1, B_WIDTH), F32),
    ]
    y, convo, ho = pl.pallas_call(
        _layer0_prompt_kernel,
        grid=(B, nt),
        in_specs=in_specs,
        out_specs=out_specs,
        out_shape=out_shape,
        scratch_shapes=[
            pltpu.VMEM((T + SUBLANES, B_WIDTH), F32),
            pltpu.VMEM((1, B_WIDTH), F32),
            pltpu.VMEM((T, B_WIDTH), F32),
            pltpu.VMEM((T, B_WIDTH), F32),
            pltpu.VMEM((T, A_WIDTH), F32),
        ],
        compiler_params=_cparams(2),
        name="layer0_prompt",
    )(x, w["norm0_g"], w["win0"], w["vg"], w["ws"], w["bst"], w["cw"], w["cb"],
      w["wr"], w["br"], w["wi"], w["bi"], w["lam"], w["wout0"], conv0, h0.reshape(B, 1, B_WIDTH))
    return y, convo, ho.reshape(B, B_WIDTH)


def _layer0_sample_kernel(x_ref, g_ref, win_ref, vg_ref, ws_ref, bst_ref, cw_ref, cb_ref,
                          wr_ref, br_ref, wi_ref, bi_ref, lam_ref, wout_ref, conv_ref, h_ref,
                          y_ref, v_ref, convo_ref, ho_ref):
    x = x_ref[...]
    xn = _rms_rows(x, g_ref[...]).astype(BF16)

    def proj(k):
        return _dot(xn, win_ref[:, k * 512:(k + 1) * 512])

    vn = _gmlp_v_rows(proj(1), vg_ref)
    v_ref[...] = vn
    s_parts = []
    for g in range(A_GROUPS):
        sl = slice(g * A_GROUP_DIM, (g + 1) * A_GROUP_DIM)
        s_parts.append(ws_ref[g, 0:1, 0:1] * vn[:, sl] + bst_ref[0:1, g:g + 1])
    s = jnp.concatenate(s_parts, axis=-1)
    mix_a = (jax.nn.gelu(proj(0)) * s * jax.nn.silu(proj(2))).astype(BF16)

    xb = proj(3)
    xc = cb_ref[...] + cw_ref[3:4, :] * xb
    for tap in range(CONV_W - 1):
        xc = xc + cw_ref[tap:tap + 1, :] * conv_ref[:, tap * B_WIDTH:(tap + 1) * B_WIDTH]
    for tap in range(CONV_W - 2):
        convo_ref[:, tap * B_WIDTH:(tap + 1) * B_WIDTH] = (
            conv_ref[:, (tap + 1) * B_WIDTH:(tap + 2) * B_WIDTH])
    convo_ref[:, (CONV_W - 2) * B_WIDTH:] = xb

    a, bt = _lru_gates(xc, wr_ref, br_ref, wi_ref, bi_ref, lam_ref)
    h = a * h_ref[...] + bt
    ho_ref[...] = h
    mix_b = (h * jax.nn.silu(proj(4))).astype(BF16)
    y_ref[...] = _dot(mix_a, wout_ref[0:A_WIDTH, :]) + _dot(mix_b, wout_ref[A_WIDTH:, :]) + x


def _layer0_sample(x, conv, h, w):
    n = x.shape[0]
    out_shape = [
        jax.ShapeDtypeStruct((n, D_MODEL), F32),
        jax.ShapeDtypeStruct((n, A_WIDTH), F32),
        jax.ShapeDtypeStruct((n, (CONV_W - 1) * B_WIDTH), F32),
        jax.ShapeDtypeStruct((n, B_WIDTH), F32),
    ]
    return pl.pallas_call(
        _layer0_sample_kernel,
        out_shape=out_shape,
        compiler_params=pltpu.CompilerParams(vmem_limit_bytes=VMEM_LIMIT),
        name="layer0_sample",
    )(x, w["norm0_g"], w["win0"], w["vg"], w["ws"], w["bst"], w["cw"], w["cb"],
      w["wr"], w["br"], w["wi"], w["bi"], w["lam"], w["wout0"],
      conv.reshape(n, (CONV_W - 1) * B_WIDTH), h)


def _head_norm_cols(t, gain_col):
    ms = jnp.mean(t * t, axis=0, keepdims=True)
    return t * lax.rsqrt(ms + EPS) * gain_col


def _fox_proj_prompt_kernel(x_ref, g_ref, wt_ref, wft_ref, bf_ref, qg_ref, kg_ref,
                            qaug_ref, kaug_ref, kt_ref, vt_ref, gt_ref, lf_ref, carry):
    T = PROJ_TILE
    i = pl.program_id(1)

    @pl.when(i == 0)
    def _():
        carry[...] = jnp.zeros_like(carry)

    xn = _rms_rows(x_ref[0], g_ref[...])
    xnt = xn.T.astype(BF16)

    def proj_t(k):
        return _dot(wt_ref[k * C_WIDTH:(k + 1) * C_WIDTH, :], xnt)

    lf = jax.nn.log_sigmoid(_dot(wft_ref[...], xnt) + bf_ref[...])
    lf_ref[0] = lf
    upper = (lax.broadcasted_iota(jnp.int32, (T, T), 0)
             <= lax.broadcasted_iota(jnp.int32, (T, T), 1)).astype(BF16)
    hi, lo = _split_bf16(lf)
    c = _dot(hi, upper) + _dot(lo, upper) + carry[...]
    carry[...] = c[:, T - 1:T]
    pieces, rest = [], c * (-LOG2E)
    for _ in range(N_BIAS):
        p = rest.astype(BF16).astype(F32)
        pieces.append(p)
        rest = rest - p

    sub = lax.broadcasted_iota(jnp.int32, (SUBLANES, T), 0)
    ones_rows = jnp.where(sub < N_BIAS, 1.0, 0.0)
    pad = jnp.zeros((AUG_ROWS - C_HEAD_DIM - SUBLANES, T), F32)
    qt = proj_t(0)
    kt = proj_t(1)
    for h in range(C_HEADS):
        sl = slice(h * C_HEAD_DIM, (h + 1) * C_HEAD_DIM)
        qn = _head_norm_cols(qt[sl], qg_ref[...]) * (ATTN_SCALE * LOG2E)
        kn = _head_norm_cols(kt[sl], kg_ref[...])
        kt_ref[0, h] = kn
        qaug_ref[0, h, 0] = jnp.concatenate([qn, ones_rows, pad], axis=0).astype(BF16)
        bias_rows = jnp.zeros((SUBLANES, T), F32)
        for j in range(N_BIAS):
            bias_rows = jnp.where(sub == j, pieces[j][h:h + 1, :], bias_rows)
        kaug_t = jnp.concatenate([kn, bias_rows, pad], axis=0)
        kaug_ref[0, h] = kaug_t.T.astype(BF16)
    vt_ref[0] = proj_t(2).reshape(C_HEADS, C_HEAD_DIM, T)
    gt_ref[0] = proj_t(3)


def _fox_proj_prompt(x, w):
    B, L, _ = x.shape
    T = PROJ_TILE
    assert L % ATT_T == 0 and ATT_T % T == 0
    per_q = ATT_T // T
    full = lambda shape: pl.BlockSpec(shape, lambda b, i: (0,) * len(shape))
    in_specs = [
        pl.BlockSpec((1, T, D_MODEL), lambda b, i: (b, i, 0)),
        full((1, D_MODEL)), full((4 * C_WIDTH, D_MODEL)), full((C_HEADS, D_MODEL)),
        full((C_HEADS, 1)), full((C_HEAD_DIM, 1)), full((C_HEAD_DIM, 1)),
    ]
    out_specs = [
        pl.BlockSpec((1, C_HEADS, 1, AUG_ROWS, T), lambda b, i: (b, 0, i // per_q, 0, i % per_q)),
        pl.BlockSpec((1, C_HEADS, T, AUG_ROWS), lambda b, i: (b, 0, i, 0)),
        pl.BlockSpec((1, C_HEADS, C_HEAD_DIM, T), lambda b, i: (b, 0, 0, i)),
        pl.BlockSpec((1, C_HEADS, C_HEAD_DIM, T), lambda b, i: (b, 0, 0, i)),
        pl.BlockSpec((1, C_WIDTH, T), lambda b, i: (b, 0, i)),
        pl.BlockSpec((1, C_HEADS, T), lambda b, i: (b, 0, i)),
    ]
    out_shape = [
        jax.ShapeDtypeStruct((B, C_HEADS, L // ATT_T, AUG_ROWS, ATT_T), BF16),
        jax.ShapeDtypeStruct((B, C_HEADS, L, AUG_ROWS), BF16),
        jax.ShapeDtypeStruct((B, C_HEADS, C_HEAD_DIM, L), F32),
        jax.ShapeDtypeStruct((B, C_HEADS, C_HEAD_DIM, L), F32),
        jax.ShapeDtypeStruct((B, C_WIDTH, L), F32),
        jax.ShapeDtypeStruct((B, C_HEADS, L), F32),
    ]
    return pl.pallas_call(
        _fox_proj_prompt_kernel,
        grid=(B, L // T),
        in_specs=in_specs,
        out_specs=out_specs,
        out_shape=out_shape,
        scratch_shapes=[pltpu.VMEM((C_HEADS, 1), F32)],
        compiler_params=_cparams(2),
        name="fox_proj_prompt",
    )(x, w["norm1_g"], w["win1_t"], w["wf_t"], w["bf_col"], w["qg_col"], w["kg_col"])


def _fox_attn_kernel(ki_tab, qi_tab, qaug_ref, kaug_ref, vt_ref, o_ref,
                     m_scr, acc_scr, vaug_scr, sa_scr, sb_scr, ma_scr, mb_scr, alpha_a, alpha_b):
    nq = qaug_ref.shape[2]
    t = ATT_T
    n_off = nq * (nq - 1) // 2

    m_scr[...] = jnp.full(m_scr.shape, NEG_BIG, F32)
    acc_scr[...] = jnp.zeros(acc_scr.shape, F32)
    extra = lax.broadcasted_iota(jnp.int32, (V_AUG_ROWS - C_HEAD_DIM, t), 0)
    one_row = jnp.where(extra == 0, 1.0, 0.0)
    for kb in range(nq):
        vaug_scr[kb] = jnp.concatenate(
            [vt_ref[0, 0, :, kb * t:(kb + 1) * t], one_row], axis=0).astype(BF16)

    def start(idx, buf, masked):
        s_ref, m_ref, alpha_ref = buf
        ki, qi = ki_tab[idx], qi_tab[idx]
        keys = kaug_ref[0, 0, pl.ds(pl.multiple_of(ki * t, t), t), :]
        s = _dot(keys, qaug_ref[0, 0, qi])
        if masked:
            causal = (lax.broadcasted_iota(jnp.int32, (t, t), 0)
                      <= lax.broadcasted_iota(jnp.int32, (t, t), 1))
            s = jnp.where(causal, s, NEG_BIG)
        s_ref[...] = s
        slab = jnp.max(s.reshape(MAX_SLABS, t // MAX_SLABS, t), axis=0)
        m_old = m_scr[qi]
        m_new = jnp.maximum(m_old, jnp.max(slab, axis=0, keepdims=True))
        alpha_ref[...] = jnp.exp2(m_old - m_new)
        m_ref[...] = m_new
        m_scr[qi] = m_new

    def finish(idx, buf):
        s_ref, m_ref, alpha_ref = buf
        ki, qi = ki_tab[idx], qi_tab[idx]
        p = jnp.exp2(s_ref[...] - m_ref[...]).astype(BF16)
        acc_scr[qi] = alpha_ref[...] * acc_scr[qi] + _dot(vaug_scr[ki], p)

    buf_a = (sa_scr, ma_scr, alpha_a)
    buf_b = (sb_scr, mb_scr, alpha_b)

    def run(first, count, masked):
        assert count % 2 == 0 and count >= 2
        start(first, buf_a, masked)

        def pair(j, carry):
            idx = first + 2 * j
            start(idx + 1, buf_b, masked)
            finish(idx, buf_a)
            start(idx + 2, buf_a, masked)
            finish(idx + 1, buf_b)
            return carry

        lax.fori_loop(0, count // 2 - 1, pair, 0)
        start(first + count - 1, buf_b, masked)
        finish(first + count - 2, buf_a)
        finish(first + count - 1, buf_b)

    run(0, n_off, False)
    run(n_off, nq, True)

    for qi in range(nq):
        o_ref[0, 0, :, qi * t:(qi + 1) * t] = (
            acc_scr[qi, 0:C_HEAD_DIM, :] / acc_scr[qi, C_HEAD_DIM:C_HEAD_DIM + 1, :])


def _fox_attn_prompt(qaug, kaug, vt):
    B, H, nq, _, t = qaug.shape
    L = nq * t
    assert nq % 2 == 0
    off_diag = [(k, q) for k in range(nq) for q in range(k + 1, nq)]
    blocks = off_diag + [(k, k) for k in range(nq)]
    ki_tab = jnp.asarray(np.array([b[0] for b in blocks], np.int32))
    qi_tab = jnp.asarray(np.array([b[1] for b in blocks], np.int32))
    grid_spec = pltpu.PrefetchScalarGridSpec(
        num_scalar_prefetch=2,
        grid=(B, H),
        in_specs=[
            pl.BlockSpec((1, 1, nq, AUG_ROWS, t), lambda b, h, kt, qt: (b, h, 0, 0, 0)),
            pl.BlockSpec((1, 1, L, AUG_ROWS), lambda b, h, kt, qt: (b, h, 0, 0)),
            pl.BlockSpec((1, 1, C_HEAD_DIM, L), lambda b, h, kt, qt: (b, h, 0, 0)),
        ],
        out_specs=pl.BlockSpec((1, 1, C_HEAD_DIM, L), lambda b, h, kt, qt: (b, h, 0, 0)),
        scratch_shapes=[
            pltpu.VMEM((nq, 1, t), F32),
            pltpu.VMEM((nq, V_AUG_ROWS, t), F32),
            pltpu.VMEM((nq, V_AUG_ROWS, t), BF16),
            pltpu.VMEM((t, t), F32),
            pltpu.VMEM((t, t), F32),
            pltpu.VMEM((1, t), F32),
            pltpu.VMEM((1, t), F32),
            pltpu.VMEM((1, t), F32),
            pltpu.VMEM((1, t), F32),
        ],
    )
    return pl.pallas_call(
        _fox_attn_kernel,
        grid_spec=grid_spec,
        out_shape=jax.ShapeDtypeStruct((B, H, C_HEAD_DIM, L), F32),
        compiler_params=_cparams(2),
        name="fox_attn_prompt",
    )(ki_tab, qi_tab, qaug, kaug, vt)


def _gate_out_t_kernel(att_ref, gate_ref, x_ref, wt_ref, y_ref):
    mixed = (att_ref[0] * jax.nn.silu(gate_ref[0])).astype(BF16)
    y_ref[0] = _dot(wt_ref[...], mixed).T + x_ref[0]


def _gate_out_t(att_t, gate_t, x, wout_t):
    B, L, _ = x.shape
    T = OUT_TILE
    assert L % T == 0
    col = pl.BlockSpec((1, C_WIDTH, T), lambda b, i: (b, 0, i))
    row = pl.BlockSpec((1, T, D_MODEL), lambda b, i: (b, i, 0))
    return pl.pallas_call(
        _gate_out_t_kernel,
        grid=(B, L // T),
        in_specs=[col, col, row, pl.BlockSpec((D_MODEL, C_WIDTH), lambda b, i: (0, 0))],
        out_specs=row,
        out_shape=jax.ShapeDtypeStruct((B, L, D_MODEL), F32),
        compiler_params=_cparams(2),
        name="gate_out_prompt",
    )(att_t, gate_t, x, wout_t)


def _fox_proj_sample_kernel(x_ref, g_ref, w_ref, wf_ref, bf_ref, qg_ref, kg_ref, seg_ref, exp_ref,
                            q_ref, k_ref, v_ref, gate_ref, lf_ref):
    xn = _rms_rows(x_ref[...], g_ref[...]).astype(BF16)

    def head_norm(t, gain):
        ssq = _dot((t * t).astype(BF16), seg_ref[...])
        rs = lax.rsqrt(ssq * (1.0 / C_HEAD_DIM) + EPS)
        hi, lo = _split_bf16(rs)
        rs_full = _dot(hi, exp_ref[...]) + _dot(lo, exp_ref[...])
        return t * rs_full * gain

    q_ref[...] = head_norm(_dot(xn, w_ref[:, 0:C_WIDTH]), qg_ref[...]) * ATTN_SCALE
    k_ref[...] = head_norm(_dot(xn, w_ref[:, C_WIDTH:2 * C_WIDTH]), kg_ref[...])
    v_ref[...] = _dot(xn, w_ref[:, 2 * C_WIDTH:3 * C_WIDTH])
    gate_ref[...] = _dot(xn, w_ref[:, 3 * C_WIDTH:4 * C_WIDTH])
    logf = jax.nn.log_sigmoid(_dot(xn, wf_ref[...]) + bf_ref[...])
    lf_ref[...] = logf[:, 0:C_HEADS]


def _fox_proj_sample(x, w):
    n = x.shape[0]
    out_shape = [
        jax.ShapeDtypeStruct((n, C_WIDTH), F32),
        jax.ShapeDtypeStruct((n, C_WIDTH), F32),
        jax.ShapeDtypeStruct((n, C_WIDTH), F32),
        jax.ShapeDtypeStruct((n, C_WIDTH), F32),
        jax.ShapeDtypeStruct((n, C_HEADS), F32),
    ]
    return pl.pallas_call(
        _fox_proj_sample_kernel,
        out_shape=out_shape,
        compiler_params=pltpu.CompilerParams(vmem_limit_bytes=VMEM_LIMIT),
        name="fox_proj_sample",
    )(x, w["norm1_g"], w["win1"], w["wf"], w["bf"], w["qg"], w["kg"], w["seg"], w["exp"])


def _sublane_total(x):
    acc = x[0:SUBLANES]
    for r in range(1, x.shape[0] // SUBLANES):
        acc = acc + x[r * SUBLANES:(r + 1) * SUBLANES]
    shift = SUBLANES // 2
    while shift >= 1:
        acc = acc + pltpu.roll(acc, shift, axis=0)
        shift //= 2
    return acc


def _fox_decode_kernel(pt_ref, qcol_ref, kcol_ref, vnew_ref, lfn_ref, *refs):
    PP = PAGES_PER_STEP
    k_refs = refs[0:PP]
    v_refs = refs[PP:2 * PP]
    lf_refs = refs[2 * PP:3 * PP]
    o_ref = refs[3 * PP]
    qb, s_scr, r_scr, ps_scr, l_scr, acc = refs[3 * PP + 1:]
    phase = pl.program_id(1)
    step = pl.program_id(2)
    nsteps = pl.num_programs(2)

    @pl.when(phase == 0)
    def _():
        @pl.when(step == 0)
        def _():
            qc = qcol_ref[0]
            for h in range(C_HEADS):
                qb[h] = jnp.broadcast_to(qc[:, h:h + 1], (C_HEAD_DIM, PAGE_SIZE))
            r_scr[...] = jnp.broadcast_to(lfn_ref[0], (C_HEADS, PAGE_SIZE))

        slot_i = lax.broadcasted_iota(jnp.int32, (PAGE_SIZE, 2 * PAGE_SIZE), 0)
        slot_j = lax.broadcasted_iota(jnp.int32, (PAGE_SIZE, 2 * PAGE_SIZE), 1)
        sum_mat = jnp.where((slot_j >= PAGE_SIZE) | (slot_i > slot_j), 1.0, 0.0).astype(BF16)
        lf_all = jnp.concatenate([lf_refs[i][0] for i in range(PP)], axis=0)
        hi, lo = _split_bf16(lf_all)
        sums = _dot(hi, sum_mat) + _dot(lo, sum_mat)

        sub = lax.broadcasted_iota(jnp.int32, (SUBLANES, PAGE_SIZE), 0)
        groups = C_HEADS // SUBLANES
        s_parts = [[jnp.zeros((SUBLANES, PAGE_SIZE), F32) for _ in range(groups)] for _ in range(PP)]
        for h in range(C_HEADS):
            qh = qb[h]
            for i in range(PP):
                tot = _sublane_total(k_refs[i][0, h] * qh)
                g = h // SUBLANES
                s_parts[i][g] = jnp.where(sub == h % SUBLANES, tot, s_parts[i][g])
        r = r_scr[...]
        for i in range(PP):
            s = jnp.concatenate(s_parts[i], axis=0)
            page = sums[i * C_HEADS:(i + 1) * C_HEADS]
            s_scr[step * PP + i] = s + page[:, 0:PAGE_SIZE] + r
            r = r + page[:, PAGE_SIZE:]
        r_scr[...] = r

    @pl.when(phase == 1)
    def _():
        @pl.when(step == 0)
        def _():
            eye = (lax.broadcasted_iota(jnp.int32, (C_HEADS, C_HEADS), 0)
                   == lax.broadcasted_iota(jnp.int32, (C_HEADS, C_HEADS), 1))
            self_row = jnp.sum(qcol_ref[0] * kcol_ref[0], axis=0, keepdims=True)
            s_self = jnp.sum(jnp.where(eye, self_row, 0.0), axis=1, keepdims=True)
            s_all = s_scr[...]
            m = jnp.max(jnp.max(s_all, axis=0), axis=1, keepdims=True)
            m = jnp.maximum(m, s_self)
            p_all = jnp.exp(s_all - m)
            s_scr[...] = p_all
            p_self = jnp.exp(s_self - m)
            ps_scr[...] = p_self
            l_scr[...] = jnp.sum(jnp.sum(p_all, axis=0), axis=1, keepdims=True) + p_self
            acc[...] = jnp.zeros(acc.shape, F32)

        p_pages = [s_scr[step * PP + i] for i in range(PP)]
        for h in range(C_HEADS):
            a = acc[h]
            for i in range(PP):
                a = a + p_pages[i][h:h + 1, :] * v_refs[i][0, h]
            acc[h] = a

        @pl.when(step == nsteps - 1)
        def _():
            ones = jnp.ones((SUBLANES, PAGE_SIZE), BF16)
            rows = lax.broadcasted_iota(jnp.int32, (C_HEADS, C_HEAD_DIM), 0)
            out = jnp.zeros((C_HEADS, C_HEAD_DIM), F32)
            for h in range(C_HEADS):
                hi, lo = _split_bf16(acc[h])
                tot = _dot_nt(ones, hi) + _dot_nt(ones, lo)
                out = jnp.where(rows == h, tot[0:1, :], out)
            o_ref[0] = (out + ps_scr[...] * vnew_ref[0]) / l_scr[...]


def _fox_attn_sample(q, k_new, v_new, lf_new, cache_k, cache_v, cache_logf, page_table):
    n, n_pages = page_table.shape
    PP = PAGES_PER_STEP
    assert n_pages % PP == 0
    nsteps = n_pages // PP
    kt = jnp.transpose(cache_k, (0, 2, 3, 1))
    vt = jnp.transpose(cache_v, (0, 2, 3, 1))
    lft = jnp.transpose(cache_logf, (0, 2, 1))
    to_cols = lambda a: jnp.transpose(a.reshape(n, C_HEADS, C_HEAD_DIM), (0, 2, 1))

    def page_map(i, active_phase, idle_step):
        def index_map(b, ph, s, pt):
            s_eff = jnp.where(ph == active_phase, s, idle_step)
            return (pt[b, n_pages - 1 - (s_eff * PP + i)],) + (0,) * 3
        return index_map

    def lf_map(i):
        def index_map(b, ph, s, pt):
            s_eff = jnp.where(ph == 0, s, nsteps - 1)
            return (pt[b, n_pages - 1 - (s_eff * PP + i)], 0, 0)
        return index_map

    small = lambda shape: pl.BlockSpec((1,) + shape, lambda b, ph, s, pt: (b, 0, 0))
    page_block = (1, C_HEADS, C_HEAD_DIM, PAGE_SIZE)
    in_specs = [small((C_HEAD_DIM, C_HEADS)), small((C_HEAD_DIM, C_HEADS)),
                small((C_HEADS, C_HEAD_DIM)), small((C_HEADS, 1))]
    in_specs += [pl.BlockSpec(page_block, page_map(i, 0, nsteps - 1)) for i in range(PP)]
    in_specs += [pl.BlockSpec(page_block, page_map(i, 1, 0)) for i in range(PP)]
    in_specs += [pl.BlockSpec((1, C_HEADS, PAGE_SIZE), lf_map(i)) for i in range(PP)]
    grid_spec = pltpu.PrefetchScalarGridSpec(
        num_scalar_prefetch=1,
        grid=(n, 2, nsteps),
        in_specs=in_specs,
        out_specs=pl.BlockSpec((1, C_HEADS, C_HEAD_DIM), lambda b, ph, s, pt: (b, 0, 0)),
        scratch_shapes=[
            pltpu.VMEM((C_HEADS, C_HEAD_DIM, PAGE_SIZE), F32),
            pltpu.VMEM((n_pages, C_HEADS, PAGE_SIZE), F32),
            pltpu.VMEM((C_HEADS, PAGE_SIZE), F32),
            pltpu.VMEM((C_HEADS, 1), F32),
            pltpu.VMEM((C_HEADS, 1), F32),
            pltpu.VMEM((C_HEADS, C_HEAD_DIM, PAGE_SIZE), F32),
        ],
    )
    out = pl.pallas_call(
        _fox_decode_kernel,
        grid_spec=grid_spec,
        out_shape=jax.ShapeDtypeStruct((n, C_HEADS, C_HEAD_DIM), F32),
        compiler_params=_cparams(3),
        name="fox_attn_sample",
    )(page_table, to_cols(q), to_cols(k_new), v_new.reshape(n, C_HEADS, C_HEAD_DIM),
      lf_new.reshape(n, C_HEADS, 1), *([kt] * PP), *([vt] * PP), *([lft] * PP))
    return out.reshape(n, C_WIDTH)


def _gate_out_kernel(att_ref, gate_ref, x_ref, w_ref, y_ref):
    mixed = (att_ref[...] * jax.nn.silu(gate_ref[...])).astype(BF16)
    y_ref[...] = _dot(mixed, w_ref[...]) + x_ref[...]


def _gate_out(att, gate, x, wout):
    return pl.pallas_call(
        _gate_out_kernel,
        out_shape=jax.ShapeDtypeStruct(x.shape, F32),
        compiler_params=pltpu.CompilerParams(vmem_limit_bytes=VMEM_LIMIT),
        name="gate_out_sample",
    )(att, gate, x, wout)


def _block_diag(w):
    nb, d, _ = w.shape
    eye = jnp.eye(nb, dtype=w.dtype)
    return (eye[:, None, :, None] * w[:, :, None, :]).reshape(nb * d, nb * d)


def _prepare_weights(norm0_g, w_in0, gmlp_v_g, gmlp_w_s, gmlp_b_s, lru_conv_w, lru_conv_b,
                     lru_w_r, lru_b_r, lru_w_i, lru_b_i, lru_lambda, w_out0, norm1_g, w_in1,
                     fox_b_f, q_norm_g, k_norm_g, w_out1):
    lane = np.arange(C_WIDTH) // C_HEAD_DIM
    seg = (lane[:, None] == np.arange(LANES)[None, :]).astype(np.float32)
    win1 = w_in1[:, 0:4 * C_WIDTH].astype(BF16)
    wf = w_in1[:, 4 * C_WIDTH:].astype(BF16)
    wout1 = w_out1.astype(BF16)
    return {
        "norm0_g": norm0_g.reshape(1, D_MODEL),
        "win0": w_in0.astype(BF16),
        "vg": gmlp_v_g.reshape(1, A_WIDTH),
        "ws": gmlp_w_s,
        "bst": gmlp_b_s.T,
        "cw": lru_conv_w,
        "cb": lru_conv_b.reshape(1, B_WIDTH),
        "wr": _block_diag(lru_w_r).astype(BF16),
        "br": lru_b_r.reshape(1, B_WIDTH),
        "wi": _block_diag(lru_w_i).astype(BF16),
        "bi": lru_b_i.reshape(1, B_WIDTH),
        "lam": lru_lambda.reshape(1, B_WIDTH),
        "wout0": w_out0.astype(BF16),
        "norm1_g": norm1_g.reshape(1, D_MODEL),
        "win1": win1,
        "win1_t": win1.T,
        "wf": jnp.pad(wf, ((0, 0), (0, LANES - C_HEADS))),
        "wf_t": wf.T,
        "bf": jnp.pad(fox_b_f, (0, LANES - C_HEADS)).reshape(1, LANES),
        "bf_col": fox_b_f.reshape(C_HEADS, 1),
        "qg": jnp.tile(q_norm_g, C_HEADS).reshape(1, C_WIDTH),
        "kg": jnp.tile(k_norm_g, C_HEADS).reshape(1, C_WIDTH),
        "qg_col": q_norm_g.reshape(C_HEAD_DIM, 1),
        "kg_col": k_norm_g.reshape(C_HEAD_DIM, 1),
        "seg": jnp.asarray(seg, BF16),
        "exp": jnp.asarray(seg.T, BF16),
        "wout1": wout1,
        "wout1_t": wout1.T,
    }


def kernel(x_prompt, x_sample, state_lru_conv, state_lru_h, cache_k, cache_v, cache_logf, page_table, norm0_g, w_in0, gmlp_v_g, gmlp_w_s, gmlp_b_s, lru_conv_w, lru_conv_b, lru_w_r, lru_b_r, lru_w_i, lru_b_i, lru_lambda, w_out0, norm1_g, w_in1, fox_b_f, q_norm_g, k_norm_g, w_out1):
    Bp, L, _ = x_prompt.shape
    Bs = x_sample.shape[0]
    w = _prepare_weights(norm0_g, w_in0, gmlp_v_g, gmlp_w_s, gmlp_b_s, lru_conv_w, lru_conv_b,
                         lru_w_r, lru_b_r, lru_w_i, lru_b_i, lru_lambda, w_out0, norm1_g, w_in1,
                         fox_b_f, q_norm_g, k_norm_g, w_out1)

    conv0 = jnp.zeros((Bp, CONV_W - 1, B_WIDTH), F32)
    h0 = jnp.zeros((Bp, B_WIDTH), F32)
    yp0, lru_conv_p, lru_h_p = _layer0_prompt(x_prompt, conv0, h0, w)
    ys0, gmlp_v_s, conv_s, lru_h_s = _layer0_sample(
        x_sample.reshape(Bs, D_MODEL), state_lru_conv, state_lru_h, w)

    qaug, kaug, kt_p, vt_p, gt_p, lft_p = _fox_proj_prompt(yp0, w)
    att_t = _fox_attn_prompt(qaug, kaug, vt_p)
    yp = _gate_out_t(att_t.reshape(Bp, C_WIDTH, L), gt_p, yp0, w["wout1_t"])
    k_p = jnp.transpose(kt_p, (0, 3, 1, 2))
    v_p = jnp.transpose(vt_p, (0, 3, 1, 2))
    logf_p = jnp.transpose(lft_p, (0, 2, 1))

    q_s, k_s, v_s, g_s, logf_s = _fox_proj_sample(ys0, w)
    att_s = _fox_attn_sample(q_s, k_s, v_s, logf_s, cache_k, cache_v, cache_logf, page_table)
    ys = _gate_out(att_s, g_s, ys0, w["wout1"])

    return (yp, ys.reshape(Bs, 1, D_MODEL), lru_conv_p, lru_h_p, k_p, v_p, logf_p,
            gmlp_v_s.reshape(Bs, 1, A_WIDTH), conv_s.reshape(Bs, CONV_W - 1, B_WIDTH), lru_h_s,
            k_s.reshape(Bs, 1, C_HEADS, C_HEAD_DIM), v_s.reshape(Bs, 1, C_HEADS, C_HEAD_DIM),
            logf_s.reshape(Bs, 1, C_HEADS))
```

```python
import jax
import jax.numpy as jnp
import numpy as np
from jax import lax
from jax.experimental import pallas as pl
from jax.experimental.pallas import tpu as pltpu

D_MODEL = 1024
A_WIDTH = 512
A_GROUPS = 4
A_GROUP_DIM = 128
CHUNK = 128
B_WIDTH = 512
B_BLOCKS = 8
B_BLOCK_DIM = 64
CONV_W = 4
LRU_C = 8.0
C_HEADS = 16
C_HEAD_DIM = 64
C_WIDTH = 1024
PAGE_SIZE = 128
ATTN_SCALE = C_HEAD_DIM ** -0.5
EPS = 1e-6

LANES = 128
SUBLANES = 8
NEG_BIG = -1e30

F32 = jnp.float32
BF16 = jnp.bfloat16

L0_TILE = 256
PROJ_TILE = 256
ATT_T = 512
OUT_TILE = 256
PAGES_PER_STEP = 16
MAX_SLABS = 8
AUG_ROWS = 128
V_AUG_ROWS = 80
LOG2E = 1.4426950408889634
N_BIAS = 3
VMEM_LIMIT = 56 * 1024 * 1024


def _cparams(n_grid_dims):
    return pltpu.CompilerParams(
        dimension_semantics=("arbitrary",) * n_grid_dims,
        vmem_limit_bytes=VMEM_LIMIT,
    )


def _dot(a, b):
    return jnp.dot(a, b, preferred_element_type=F32)


def _dot_nt(a, b):
    return lax.dot_general(a, b, (((1,), (1,)), ((), ())), preferred_element_type=F32)


def _split_bf16(x):
    hi = x.astype(BF16)
    lo = (x - hi.astype(F32)).astype(BF16)
    return hi, lo


def _rms_rows(x, gain):
    ms = jnp.mean(x * x, axis=-1, keepdims=True)
    return x * lax.rsqrt(ms + EPS) * gain


def _gmlp_v_rows(pv, vg_ref):
    v = jax.nn.gelu(pv)
    parts = []
    for g in range(A_GROUPS):
        sl = slice(g * A_GROUP_DIM, (g + 1) * A_GROUP_DIM)
        parts.append(_rms_rows(v[:, sl], vg_ref[:, sl]))
    return jnp.concatenate(parts, axis=-1)


def _lru_gates(xc, wr_ref, br_ref, wi_ref, bi_ref, lam_ref):
    xcb = xc.astype(BF16)
    r = jax.nn.sigmoid(_dot(xcb, wr_ref[...]) + br_ref[...])
    gi = jax.nn.sigmoid(_dot(xcb, wi_ref[...]) + bi_ref[...])
    log_a = -LRU_C * r * jax.nn.softplus(-lam_ref[...])
    a = jnp.exp(log_a)
    bterm = jnp.sqrt(-jnp.tanh(log_a) * (1.0 + a * a)) * (gi * xc)
    return a, bterm


def _layer0_prompt_kernel(x_ref, g_ref, win_ref, vg_ref, ws_ref, bst_ref, cw_ref, cb_ref,
                          wr_ref, br_ref, wi_ref, bi_ref, lam_ref, wout_ref, conv0_ref, h0_ref,
                          y_ref, convo_ref, ho_ref,
                          xbuf, hcar, a8, b8, s_scr):
    T = L0_TILE
    i = pl.program_id(1)

    @pl.when(i == 0)
    def _():
        xbuf[0:SUBLANES, :] = jnp.zeros((SUBLANES, B_WIDTH), F32)
        xbuf[SUBLANES - (CONV_W - 1):SUBLANES, :] = conv0_ref[0]
        hcar[...] = h0_ref[0]

    x = x_ref[0]
    xn = _rms_rows(x, g_ref[...]).astype(BF16)

    def proj(k):
        return _dot(xn, win_ref[:, k * 512:(k + 1) * 512])

    vn = _gmlp_v_rows(proj(1), vg_ref).astype(BF16)
    tri = (lax.broadcasted_iota(jnp.int32, (CHUNK, CHUNK), 0)
           >= lax.broadcasted_iota(jnp.int32, (CHUNK, CHUNK), 1))
    for g in range(A_GROUPS):
        wg = jnp.where(tri, ws_ref[g], 0.0).astype(BF16)
        bias = bst_ref[:, g:g + 1]
        for c in range(T // CHUNK):
            blk = vn[c * CHUNK:(c + 1) * CHUNK, g * A_GROUP_DIM:(g + 1) * A_GROUP_DIM]
            s_scr[c * CHUNK:(c + 1) * CHUNK, g * A_GROUP_DIM:(g + 1) * A_GROUP_DIM] = (
                _dot(wg, blk) + bias)
    u = jax.nn.gelu(proj(0))
    mix_a = (u * s_scr[...] * jax.nn.silu(proj(2))).astype(BF16)

    xb = proj(3)
    xbuf[SUBLANES:SUBLANES + T, :] = xb
    xc = cb_ref[...] + cw_ref[3:4, :] * xb
    for tap in range(CONV_W - 1):
        off = SUBLANES - (CONV_W - 1) + tap
        xc = xc + cw_ref[tap:tap + 1, :] * xbuf[off:off + T, :]
    convo_ref[0] = xbuf[T + SUBLANES - (CONV_W - 1):T + SUBLANES, :]
    xbuf[0:SUBLANES, :] = xbuf[T:T + SUBLANES, :]

    a, bt = _lru_gates(xc, wr_ref, br_ref, wi_ref, bi_ref, lam_ref)

    row = lax.broadcasted_iota(jnp.int32, (T, B_WIDTH), 0) & (SUBLANES - 1)
    shift = 1
    while shift < SUBLANES:
        a_sh = pltpu.roll(a, shift, axis=0)
        b_sh = pltpu.roll(bt, shift, axis=0)
        m = row >= shift
        bt = jnp.where(m, a * b_sh + bt, bt)
        a = jnp.where(m, a * a_sh, a)
        shift *= 2
    a8[...] = a
    b8[...] = bt

    def group_step(j, h):
        off = pl.multiple_of(j * SUBLANES, SUBLANES)
        rows = a8[pl.ds(off, SUBLANES), :] * h + b8[pl.ds(off, SUBLANES), :]
        b8[pl.ds(off, SUBLANES), :] = rows
        return rows[SUBLANES - 1:SUBLANES, :]

    h_last = lax.fori_loop(0, T // SUBLANES, group_step, hcar[...], unroll=True)
    hcar[...] = h_last
    ho_ref[0] = h_last

    mix_b = (b8[...] * jax.nn.silu(proj(4))).astype(BF16)

    y = _dot(mix_a, wout_ref[0:A_WIDTH, :]) + _dot(mix_b, wout_ref[A_WIDTH:, :]) + x
    y_ref[0] = y


def _layer0_prompt(x, conv0, h0, w):
    B, L, _ = x.shape
    T = L0_TILE
    assert L % T == 0 and T % CHUNK == 0
    nt = L // T
    full = lambda shape: pl.BlockSpec(shape, lambda b, i: (0,) * len(shape))
    in_specs = [
        pl.BlockSpec((1, T, D_MODEL), lambda b, i: (b, i, 0)),
        full((1, D_MODEL)),
        full(w["win0"].shape),
        full((1, A_WIDTH)),
        full((A_GROUPS, CHUNK, CHUNK)),
        full((CHUNK, A_GROUPS)),
        full((CONV_W, B_WIDTH)),
        full((1, B_WIDTH)),
        full((B_WIDTH, B_WIDTH)), full((1, B_WIDTH)),
        full((B_WIDTH, B_WIDTH)), full((1, B_WIDTH)),
        full((1, B_WIDTH)),
        full((D_MODEL, D_MODEL)),
        pl.BlockSpec((1, CONV_W - 1, B_WIDTH), lambda b, i: (b, 0, 0)),
        pl.BlockSpec((1, 1, B_WIDTH), lambda b, i: (b, 0, 0)),
    ]
    out_specs = [
        pl.BlockSpec((1, T, D_MODEL), lambda b, i: (b, i, 0)),
        pl.BlockSpec((1, CONV_W - 1, B_WIDTH), lambda b, i: (b, 0, 0)),
        pl.BlockSpec((1, 1, B_WIDTH), lambda b, i: (b, 0, 0)),
    ]
    out_shape = [
        jax.ShapeDtypeStruct((B, L, D_MODEL), F32),
        jax.ShapeDtypeStruct((B, CONV_W - 1, B_WIDTH), F32),
        jax.ShapeDtypeStruct((B, 1, B_WIDTH), F32),
    ]
    y, convo, ho = pl.pallas_call(
        _layer0_prompt_kernel,
        grid=(B, nt),
        in_specs=in_specs,
        out_specs=out_specs,
        out_shape=out_shape,
        scratch_shapes=[
            pltpu.VMEM((T + SUBLANES, B_WIDTH), F32),
            pltpu.VMEM((1, B_WIDTH), F32),
            pltpu.VMEM((T, B_WIDTH), F32),
            pltpu.VMEM((T, B_WIDTH), F32),
            pltpu.VMEM((T, A_WIDTH), F32),
        ],
        compiler_params=_cparams(2),
        name="layer0_prompt",
    )(x, w["norm0_g"], w["win0"], w["vg"], w["ws"], w["bst"], w["cw"], w["cb"],
      w["wr"], w["br"], w["wi"], w["bi"], w["lam"], w["wout0"], conv0, h0.reshape(B, 1, B_WIDTH))
    return y, convo, ho.reshape(B, B_WIDTH)


def _layer0_sample_kernel(x_ref, g_ref, win_ref, vg_ref, ws_ref, bst_ref, cw_ref, cb_ref,
                          wr_ref, br_ref, wi_ref, bi_ref, lam_ref, wout_ref, conv_ref, h_ref,
                          y_ref, v_ref, convo_ref, ho_ref):
    x = x_ref[...]
    xn = _rms_rows(x, g_ref[...]).astype(BF16)

    def proj(k):
        return _dot(xn, win_ref[:, k * 512:(k + 1) * 512])

    vn = _gmlp_v_rows(proj(1), vg_ref)
    v_ref[...] = vn
    s_parts = []
    for g in range(A_GROUPS):
        sl = slice(g * A_GROUP_DIM, (g + 1) * A_GROUP_DIM)
        s_parts.append(ws_ref[g, 0:1, 0:1] * vn[:, sl] + bst_ref[0:1, g:g + 1])
    s = jnp.concatenate(s_parts, axis=-1)
    mix_a = (jax.nn.gelu(proj(0)) * s * jax.nn.silu(proj(2))).astype(BF16)

    xb = proj(3)
    xc = cb_ref[...] + cw_ref[3:4, :] * xb
    for tap in range(CONV_W - 1):
        xc = xc + cw_ref[tap:tap + 1, :] * conv_ref[:, tap * B_WIDTH:(tap + 1) * B_WIDTH]
    for tap in range(CONV_W - 2):
        convo_ref[:, tap * B_WIDTH:(tap + 1) * B_WIDTH] = (
            conv_ref[:, (tap + 1) * B_WIDTH:(tap + 2) * B_WIDTH])
    convo_ref[:, (CONV_W - 2) * B_WIDTH:] = xb

    a, bt = _lru_gates(xc, wr_ref, br_ref, wi_ref, bi_ref, lam_ref)
    h = a * h_ref[...] + bt
    ho_ref[...] = h
    mix_b = (h * jax.nn.silu(proj(4))).astype(BF16)
    y_ref[...] = _dot(mix_a, wout_ref[0:A_WIDTH, :]) + _dot(mix_b, wout_ref[A_WIDTH:, :]) + x


def _layer0_sample(x, conv, h, w):
    n = x.shape[0]
    out_shape = [
        jax.ShapeDtypeStruct((n, D_MODEL), F32),
        jax.ShapeDtypeStruct((n, A_WIDTH), F32),
        jax.ShapeDtypeStruct((n, (CONV_W - 1) * B_WIDTH), F32),
        jax.ShapeDtypeStruct((n, B_WIDTH), F32),
    ]
    return pl.pallas_call(
        _layer0_sample_kernel,
        out_shape=out_shape,
        compiler_params=pltpu.CompilerParams(vmem_limit_bytes=VMEM_LIMIT),
        name="layer0_sample",
    )(x, w["norm0_g"], w["win0"], w["vg"], w["ws"], w["bst"], w["cw"], w["cb"],
      w["wr"], w["br"], w["wi"], w["bi"], w["lam"], w["wout0"],
      conv.reshape(n, (CONV_W - 1) * B_WIDTH), h)


def _head_norm_cols(t, gain_col):
    ms = jnp.mean(t * t, axis=0, keepdims=True)
    return t * lax.rsqrt(ms + EPS) * gain_col


def _fox_proj_prompt_kernel(x_ref, g_ref, wt_ref, wft_ref, bf_ref, qg_ref, kg_ref,
                            qaug_ref, kaug_ref, kt_ref, vt_ref, gt_ref, lf_ref, carry):
    T = PROJ_TILE
    i = pl.program_id(1)

    @pl.when(i == 0)
    def _():
        carry[...] = jnp.zeros_like(carry)

    xn = _rms_rows(x_ref[0], g_ref[...])
    xnt = xn.T.astype(BF16)

    def proj_t(k):
        return _dot(wt_ref[k * C_WIDTH:(k + 1) * C_WIDTH, :], xnt)

    lf = jax.nn.log_sigmoid(_dot(wft_ref[...], xnt) + bf_ref[...])
    lf_ref[0] = lf
    upper = (lax.broadcasted_iota(jnp.int32, (T, T), 0)
             <= lax.broadcasted_iota(jnp.int32, (T, T), 1)).astype(BF16)
    hi, lo = _split_bf16(lf)
    c = _dot(hi, upper) + _dot(lo, upper) + carry[...]
    carry[...] = c[:, T - 1:T]
    pieces, rest = [], c * (-LOG2E)
    for _ in range(N_BIAS):
        p = rest.astype(BF16).astype(F32)
        pieces.append(p)
        rest = rest - p

    sub = lax.broadcasted_iota(jnp.int32, (SUBLANES, T), 0)
    ones_rows = jnp.where(sub < N_BIAS, 1.0, 0.0)
    pad = jnp.zeros((AUG_ROWS - C_HEAD_DIM - SUBLANES, T), F32)
    qt = proj_t(0)
    kt = proj_t(1)
    for h in range(C_HEADS):
        sl = slice(h * C_HEAD_DIM, (h + 1) * C_HEAD_DIM)
        qn = _head_norm_cols(qt[sl], qg_ref[...]) * (ATTN_SCALE * LOG2E)
        kn = _head_norm_cols(kt[sl], kg_ref[...])
        kt_ref[0, h] = kn
        qaug_ref[0, h, 0] = jnp.concatenate([qn, ones_rows, pad], axis=0).astype(BF16)
        bias_rows = jnp.zeros((SUBLANES, T), F32)
        for j in range(N_BIAS):
            bias_rows = jnp.where(sub == j, pieces[j][h:h + 1, :], bias_rows)
        kaug_t = jnp.concatenate([kn, bias_rows, pad], axis=0)
        kaug_ref[0, h] = kaug_t.T.astype(BF16)
    vt_ref[0] = proj_t(2).reshape(C_HEADS, C_HEAD_DIM, T)
    gt_ref[0] = proj_t(3)


def _fox_proj_prompt(x, w):
    B, L, _ = x.shape
    T = PROJ_TILE
    assert L % ATT_T == 0 and ATT_T % T == 0
    per_q = ATT_T // T
    full = lambda shape: pl.BlockSpec(shape, lambda b, i: (0,) * len(shape))
    in_specs = [
        pl.BlockSpec((1, T, D_MODEL), lambda b, i: (b, i, 0)),
        full((1, D_MODEL)), full((4 * C_WIDTH, D_MODEL)), full((C_HEADS, D_MODEL)),
        full((C_HEADS, 1)), full((C_HEAD_DIM, 1)), full((C_HEAD_DIM, 1)),
    ]
    out_specs = [
        pl.BlockSpec((1, C_HEADS, 1, AUG_ROWS, T), lambda b, i: (b, 0, i // per_q, 0, i % per_q)),
        pl.BlockSpec((1, C_HEADS, T, AUG_ROWS), lambda b, i: (b, 0, i, 0)),
        pl.BlockSpec((1, C_HEADS, C_HEAD_DIM, T), lambda b, i: (b, 0, 0, i)),
        pl.BlockSpec((1, C_HEADS, C_HEAD_DIM, T), lambda b, i: (b, 0, 0, i)),
        pl.BlockSpec((1, C_WIDTH, T), lambda b, i: (b, 0, i)),
        pl.BlockSpec((1, C_HEADS, T), lambda b, i: (b, 0, i)),
    ]
    out_shape = [
        jax.ShapeDtypeStruct((B, C_HEADS, L // ATT_T, AUG_ROWS, ATT_T), BF16),
        jax.ShapeDtypeStruct((B, C_HEADS, L, AUG_ROWS), BF16),
        jax.ShapeDtypeStruct((B, C_HEADS, C_HEAD_DIM, L), F32),
        jax.ShapeDtypeStruct((B, C_HEADS, C_HEAD_DIM, L), F32),
        jax.ShapeDtypeStruct((B, C_WIDTH, L), F32),
        jax.ShapeDtypeStruct((B, C_HEADS, L), F32),
    ]
    return pl.pallas_call(
        _fox_proj_prompt_kernel,
        grid=(B, L // T),
        in_specs=in_specs,
        out_specs=out_specs,
        out_shape=out_shape,
        scratch_shapes=[pltpu.VMEM((C_HEADS, 1), F32)],
        compiler_params=_cparams(2),
        name="fox_proj_prompt",
    )(x, w["norm1_g"], w["win1_t"], w["wf_t"], w["bf_col"], w["qg_col"], w["kg_col"])


def _fox_attn_kernel(ki_tab, qi_tab, pt_ref,
                     qaug_ref, kaug_ref, vt_ref, qcol_ref, kcol_ref, vnew_ref, lfn_ref,
                     kcache, vcache, lfcache,
                     o_ref, dec_ref,
                     m_scr, acc_scr, vaug_scr, sa_scr, sb_scr, ma_scr, mb_scr, alpha_a, alpha_b,
                     pages, lfpages, page_sem, lf_sem, qb, s_scr, r_scr, ps_scr, l_scr, dacc):
    nq = qaug_ref.shape[2]
    t = ATT_T
    n_off = nq * (nq - 1) // 2

    d = _DecodeRefs(pt_ref, qcol_ref, kcol_ref, vnew_ref, lfn_ref, kcache, vcache, lfcache, dec_ref,
                    pages, lfpages, page_sem, lf_sem, qb, s_scr, r_scr, ps_scr, l_scr, dacc)
    step = pl.program_id(0) * pl.num_programs(1) + pl.program_id(1)
    n_steps = pl.num_programs(0) * pl.num_programs(1)
    n_off_pairs = n_off // 2 - 1
    unit_stride = n_off_pairs // d.n_units
    assert unit_stride >= 1 and d.n_units % 2 == 0

    @pl.when(step == 0)
    def _():
        _decode_dma(d, 0, 0, 0, 0, start=True)

    m_scr[...] = jnp.full(m_scr.shape, NEG_BIG, F32)
    acc_scr[...] = jnp.zeros(acc_scr.shape, F32)
    extra = lax.broadcasted_iota(jnp.int32, (V_AUG_ROWS - C_HEAD_DIM, t), 0)
    one_row = jnp.where(extra == 0, 1.0, 0.0)
    for kb in range(nq):
        vaug_scr[kb] = jnp.concatenate(
            [vt_ref[0, 0, :, kb * t:(kb + 1) * t], one_row], axis=0).astype(BF16)

    def start(idx, buf, masked):
        s_ref, m_ref, alpha_ref = buf
        ki, qi = ki_tab[idx], qi_tab[idx]
        keys = kaug_ref[0, 0, pl.ds(pl.multiple_of(ki * t, t), t), :]
        s = _dot(keys, qaug_ref[0, 0, qi])
        if masked:
            causal = (lax.broadcasted_iota(jnp.int32, (t, t), 0)
                      <= lax.broadcasted_iota(jnp.int32, (t, t), 1))
            s = jnp.where(causal, s, NEG_BIG)
        s_ref[...] = s
        slab = jnp.max(s.reshape(MAX_SLABS, t // MAX_SLABS, t), axis=0)
        m_old = m_scr[qi]
        m_new = jnp.maximum(m_old, jnp.max(slab, axis=0, keepdims=True))
        alpha_ref[...] = jnp.exp2(m_old - m_new)
        m_ref[...] = m_new
        m_scr[qi] = m_new

    def finish(idx, buf):
        s_ref, m_ref, alpha_ref = buf
        ki, qi = ki_tab[idx], qi_tab[idx]
        p = jnp.exp2(s_ref[...] - m_ref[...]).astype(BF16)
        acc_scr[qi] = alpha_ref[...] * acc_scr[qi] + _dot(vaug_scr[ki], p)

    buf_a = (sa_scr, ma_scr, alpha_a)
    buf_b = (sb_scr, mb_scr, alpha_b)

    def run(first, count, masked, host_decode):
        assert count % 2 == 0 and count >= 2
        start(first, buf_a, masked)

        def pair(j, carry):
            if host_decode:
                @pl.when((j % unit_stride == 0) & (j // unit_stride < d.n_units))
                def _():
                    _decode_unit(d, step, n_steps, j // unit_stride)
            idx = first + 2 * j
            start(idx + 1, buf_b, masked)
            finish(idx, buf_a)
            start(idx + 2, buf_a, masked)
            finish(idx + 1, buf_b)
            return carry

        lax.fori_loop(0, count // 2 - 1, pair, 0)
        start(first + count - 1, buf_b, masked)
        finish(first + count - 2, buf_a)
        finish(first + count - 1, buf_b)

    run(0, n_off, False, True)
    run(n_off, nq, True, False)

    for qi in range(nq):
        o_ref[0, 0, :, qi * t:(qi + 1) * t] = (
            acc_scr[qi, 0:C_HEAD_DIM, :] / acc_scr[qi, C_HEAD_DIM:C_HEAD_DIM + 1, :])


def _fox_attention(qaug, kaug, vt, q_s, k_s, v_s, lf_s, cache_k, cache_v, cache_logf, page_table):
    B, H, nq, _, t = qaug.shape
    L = nq * t
    n, n_pages = page_table.shape
    PP = PAGES_PER_STEP
    assert nq % 2 == 0 and n_pages % PP == 0
    assert B * H == 2 * n, "one decode (batch, phase) per prompt (batch, head) grid step"
    off_diag = [(k, q) for k in range(nq) for q in range(k + 1, nq)]
    blocks = off_diag + [(k, k) for k in range(nq)]
    ki_tab = jnp.asarray(np.array([b[0] for b in blocks], np.int32))
    qi_tab = jnp.asarray(np.array([b[1] for b in blocks], np.int32))
    kt_cache = jnp.transpose(cache_k, (0, 2, 3, 1))
    vt_cache = jnp.transpose(cache_v, (0, 2, 3, 1))
    lft_cache = jnp.transpose(cache_logf, (0, 2, 1))
    to_cols = lambda a: jnp.transpose(a.reshape(n, C_HEADS, C_HEAD_DIM), (0, 2, 1))
    dec_block = lambda shape: pl.BlockSpec(
        (1,) + shape, lambda b, h, kt, qt, pt: ((b * H + h) // 2, 0, 0))
    hbm = pl.BlockSpec(memory_space=pl.ANY)
    grid_spec = pltpu.PrefetchScalarGridSpec(
        num_scalar_prefetch=3,
        grid=(B, H),
        in_specs=[
            pl.BlockSpec((1, 1, nq, AUG_ROWS, t), lambda b, h, kt, qt, pt: (b, h, 0, 0, 0)),
            pl.BlockSpec((1, 1, L, AUG_ROWS), lambda b, h, kt, qt, pt: (b, h, 0, 0)),
            pl.BlockSpec((1, 1, C_HEAD_DIM, L), lambda b, h, kt, qt, pt: (b, h, 0, 0)),
            dec_block((C_HEAD_DIM, C_HEADS)), dec_block((C_HEAD_DIM, C_HEADS)),
            dec_block((C_HEADS, C_HEAD_DIM)), dec_block((C_HEADS, 1)),
            hbm, hbm, hbm,
        ],
        out_specs=[
            pl.BlockSpec((1, 1, C_HEAD_DIM, L), lambda b, h, kt, qt, pt: (b, h, 0, 0)),
            dec_block((C_HEADS, C_HEAD_DIM)),
        ],
        scratch_shapes=[
            pltpu.VMEM((nq, 1, t), F32),
            pltpu.VMEM((nq, V_AUG_ROWS, t), F32),
            pltpu.VMEM((nq, V_AUG_ROWS, t), BF16),
            pltpu.VMEM((t, t), F32),
            pltpu.VMEM((t, t), F32),
            pltpu.VMEM((1, t), F32),
            pltpu.VMEM((1, t), F32),
            pltpu.VMEM((1, t), F32),
            pltpu.VMEM((1, t), F32),
            pltpu.VMEM((2, PP, C_HEADS, C_HEAD_DIM, PAGE_SIZE), F32),
            pltpu.VMEM((2, PP, C_HEADS, PAGE_SIZE), F32),
            pltpu.SemaphoreType.DMA((2,)),
            pltpu.SemaphoreType.DMA((2,)),
            pltpu.VMEM((C_HEADS, C_HEAD_DIM, PAGE_SIZE), F32),
            pltpu.VMEM((n_pages, C_HEADS, PAGE_SIZE), F32),
            pltpu.VMEM((C_HEADS, PAGE_SIZE), F32),
            pltpu.VMEM((C_HEADS, 1), F32),
            pltpu.VMEM((C_HEADS, 1), F32),
            pltpu.VMEM((C_HEADS, C_HEAD_DIM, PAGE_SIZE), F32),
        ],
    )
    att_t, att_s = pl.pallas_call(
        _fox_attn_kernel,
        grid_spec=grid_spec,
        out_shape=[jax.ShapeDtypeStruct((B, H, C_HEAD_DIM, L), F32),
                   jax.ShapeDtypeStruct((n, C_HEADS, C_HEAD_DIM), F32)],
        compiler_params=_cparams(2),
        name="fox_attention",
    )(ki_tab, qi_tab, page_table, qaug, kaug, vt,
      to_cols(q_s), to_cols(k_s), v_s.reshape(n, C_HEADS, C_HEAD_DIM), lf_s.reshape(n, C_HEADS, 1),
      kt_cache, vt_cache, lft_cache)
    return att_t, att_s.reshape(n, C_WIDTH)


def _gate_out_t_kernel(att_ref, gate_ref, x_ref, wt_ref, y_ref):
    mixed = (att_ref[0] * jax.nn.silu(gate_ref[0])).astype(BF16)
    y_ref[0] = _dot(wt_ref[...], mixed).T + x_ref[0]


def _gate_out_t(att_t, gate_t, x, wout_t):
    B, L, _ = x.shape
    T = OUT_TILE
    assert L % T == 0
    col = pl.BlockSpec((1, C_WIDTH, T), lambda b, i: (b, 0, i))
    row = pl.BlockSpec((1, T, D_MODEL), lambda b, i: (b, i, 0))
    return pl.pallas_call(
        _gate_out_t_kernel,
        grid=(B, L // T),
        in_specs=[col, col, row, pl.BlockSpec((D_MODEL, C_WIDTH), lambda b, i: (0, 0))],
        out_specs=row,
        out_shape=jax.ShapeDtypeStruct((B, L, D_MODEL), F32),
        compiler_params=_cparams(2),
        name="gate_out_prompt",
    )(att_t, gate_t, x, wout_t)


def _fox_proj_sample_kernel(x_ref, g_ref, w_ref, wf_ref, bf_ref, qg_ref, kg_ref, seg_ref, exp_ref,
                            q_ref, k_ref, v_ref, gate_ref, lf_ref):
    xn = _rms_rows(x_ref[...], g_ref[...]).astype(BF16)

    def head_norm(t, gain):
        ssq = _dot((t * t).astype(BF16), seg_ref[...])
        rs = lax.rsqrt(ssq * (1.0 / C_HEAD_DIM) + EPS)
        hi, lo = _split_bf16(rs)
        rs_full = _dot(hi, exp_ref[...]) + _dot(lo, exp_ref[...])
        return t * rs_full * gain

    q_ref[...] = head_norm(_dot(xn, w_ref[:, 0:C_WIDTH]), qg_ref[...]) * ATTN_SCALE
    k_ref[...] = head_norm(_dot(xn, w_ref[:, C_WIDTH:2 * C_WIDTH]), kg_ref[...])
    v_ref[...] = _dot(xn, w_ref[:, 2 * C_WIDTH:3 * C_WIDTH])
    gate_ref[...] = _dot(xn, w_ref[:, 3 * C_WIDTH:4 * C_WIDTH])
    logf = jax.nn.log_sigmoid(_dot(xn, wf_ref[...]) + bf_ref[...])
    lf_ref[...] = logf[:, 0:C_HEADS]


def _fox_proj_sample(x, w):
    n = x.shape[0]
    out_shape = [
        jax.ShapeDtypeStruct((n, C_WIDTH), F32),
        jax.ShapeDtypeStruct((n, C_WIDTH), F32),
        jax.ShapeDtypeStruct((n, C_WIDTH), F32),
        jax.ShapeDtypeStruct((n, C_WIDTH), F32),
        jax.ShapeDtypeStruct((n, C_HEADS), F32),
    ]
    return pl.pallas_call(
        _fox_proj_sample_kernel,
        out_shape=out_shape,
        compiler_params=pltpu.CompilerParams(vmem_limit_bytes=VMEM_LIMIT),
        name="fox_proj_sample",
    )(x, w["norm1_g"], w["win1"], w["wf"], w["bf"], w["qg"], w["kg"], w["seg"], w["exp"])


def _sublane_total(x):
    acc = x[0:SUBLANES]
    for r in range(1, x.shape[0] // SUBLANES):
        acc = acc + x[r * SUBLANES:(r + 1) * SUBLANES]
    shift = SUBLANES // 2
    while shift >= 1:
        acc = acc + pltpu.roll(acc, shift, axis=0)
        shift //= 2
    return acc


class _DecodeRefs:
    def __init__(self, pt, qcol, kcol, vnew, lfn, kcache, vcache, lfcache, out,
                 pages, lfpages, page_sem, lf_sem, qb, s_scr, r_scr, ps_scr, l_scr, acc):
        self.pt, self.qcol, self.kcol, self.vnew, self.lfn = pt, qcol, kcol, vnew, lfn
        self.kcache, self.vcache, self.lfcache, self.out = kcache, vcache, lfcache, out
        self.pages, self.lfpages, self.page_sem, self.lf_sem = pages, lfpages, page_sem, lf_sem
        self.qb, self.s_scr, self.r_scr, self.ps_scr, self.l_scr, self.acc = (
            qb, s_scr, r_scr, ps_scr, l_scr, acc)
        self.n_pages = pt.shape[1]
        self.n_units = self.n_pages // PAGES_PER_STEP


def _decode_copies(d, batch, phase_is_v, unit, slot):
    src = d.vcache if phase_is_v else d.kcache
    copies = []
    for i in range(PAGES_PER_STEP):
        page = d.pt[batch, d.n_pages - 1 - (unit * PAGES_PER_STEP + i)]
        copies.append(pltpu.make_async_copy(src.at[page], d.pages.at[slot, i], d.page_sem.at[slot]))
        if not phase_is_v:
            copies.append(pltpu.make_async_copy(d.lfcache.at[page], d.lfpages.at[slot, i],
                                                d.lf_sem.at[slot]))
    return copies


def _decode_dma(d, batch, phase, unit, slot, start):
    for phase_is_v in (False, True):
        @pl.when(phase == int(phase_is_v))
        def _():
            for c in _decode_copies(d, batch, phase_is_v, unit, slot):
                if start:
                    c.start()
                else:
                    c.wait()


def _decode_k_unit(d, unit, slot):
    PP = PAGES_PER_STEP

    @pl.when(unit == 0)
    def _():
        qc = d.qcol[0]
        for h in range(C_HEADS):
            d.qb[h] = jnp.broadcast_to(qc[:, h:h + 1], (C_HEAD_DIM, PAGE_SIZE))
        d.r_scr[...] = jnp.broadcast_to(d.lfn[0], (C_HEADS, PAGE_SIZE))

    slot_i = lax.broadcasted_iota(jnp.int32, (PAGE_SIZE, 2 * PAGE_SIZE), 0)
    slot_j = lax.broadcasted_iota(jnp.int32, (PAGE_SIZE, 2 * PAGE_SIZE), 1)
    sum_mat = jnp.where((slot_j >= PAGE_SIZE) | (slot_i > slot_j), 1.0, 0.0).astype(BF16)
    lf_all = jnp.concatenate([d.lfpages[slot, i] for i in range(PP)], axis=0)
    hi, lo = _split_bf16(lf_all)
    sums = _dot(hi, sum_mat) + _dot(lo, sum_mat)

    sub = lax.broadcasted_iota(jnp.int32, (SUBLANES, PAGE_SIZE), 0)
    groups = C_HEADS // SUBLANES
    s_parts = [[jnp.zeros((SUBLANES, PAGE_SIZE), F32) for _ in range(groups)] for _ in range(PP)]
    for h in range(C_HEADS):
        qh = d.qb[h]
        for i in range(PP):
            tot = _sublane_total(d.pages[slot, i, h] * qh)
            g = h // SUBLANES
            s_parts[i][g] = jnp.where(sub == h % SUBLANES, tot, s_parts[i][g])
    r = d.r_scr[...]
    for i in range(PP):
        s = jnp.concatenate(s_parts[i], axis=0)
        page = sums[i * C_HEADS:(i + 1) * C_HEADS]
        d.s_scr[unit * PP + i] = s + page[:, 0:PAGE_SIZE] + r
        r = r + page[:, PAGE_SIZE:]
    d.r_scr[...] = r


def _decode_v_unit(d, unit, slot):
    PP = PAGES_PER_STEP

    @pl.when(unit == 0)
    def _():
        eye = (lax.broadcasted_iota(jnp.int32, (C_HEADS, C_HEADS), 0)
               == lax.broadcasted_iota(jnp.int32, (C_HEADS, C_HEADS), 1))
        self_row = jnp.sum(d.qcol[0] * d.kcol[0], axis=0, keepdims=True)
        s_self = jnp.sum(jnp.where(eye, self_row, 0.0), axis=1, keepdims=True)
        s_all = d.s_scr[...]
        m = jnp.max(jnp.max(s_all, axis=0), axis=1, keepdims=True)
        m = jnp.maximum(m, s_self)
        p_all = jnp.exp(s_all - m)
        d.s_scr[...] = p_all
        p_self = jnp.exp(s_self - m)
        d.ps_scr[...] = p_self
        d.l_scr[...] = jnp.sum(jnp.sum(p_all, axis=0), axis=1, keepdims=True) + p_self
        d.acc[...] = jnp.zeros(d.acc.shape, F32)

    p_pages = [d.s_scr[unit * PP + i] for i in range(PP)]
    for h in range(C_HEADS):
        a = d.acc[h]
        for i in range(PP):
            a = a + p_pages[i][h:h + 1, :] * d.pages[slot, i, h]
        d.acc[h] = a

    @pl.when(unit == d.n_units - 1)
    def _():
        ones = jnp.ones((SUBLANES, PAGE_SIZE), BF16)
        rows = lax.broadcasted_iota(jnp.int32, (C_HEADS, C_HEAD_DIM), 0)
        out = jnp.zeros((C_HEADS, C_HEAD_DIM), F32)
        for h in range(C_HEADS):
            hi, lo = _split_bf16(d.acc[h])
            tot = _dot_nt(ones, hi) + _dot_nt(ones, lo)
            out = jnp.where(rows == h, tot[0:1, :], out)
        d.out[0] = (out + d.ps_scr[...] * d.vnew[0]) / d.l_scr[...]


def _decode_unit(d, step, n_steps, unit):
    batch, phase = step // 2, step % 2
    slot = unit % 2
    _decode_dma(d, batch, phase, unit, slot, start=False)

    @pl.when(unit < d.n_units - 1)
    def _():
        _decode_dma(d, batch, phase, unit + 1, 1 - slot, start=True)

    @pl.when((unit == d.n_units - 1) & (step < n_steps - 1))
    def _():
        nxt = step + 1
        _decode_dma(d, nxt // 2, nxt % 2, 0, 0, start=True)

    @pl.when(phase == 0)
    def _():
        _decode_k_unit(d, unit, slot)

    @pl.when(phase == 1)
    def _():
        _decode_v_unit(d, unit, slot)


def _gate_out_kernel(att_ref, gate_ref, x_ref, w_ref, y_ref):
    mixed = (att_ref[...] * jax.nn.silu(gate_ref[...])).astype(BF16)
    y_ref[...] = _dot(mixed, w_ref[...]) + x_ref[...]


def _gate_out(att, gate, x, wout):
    return pl.pallas_call(
        _gate_out_kernel,
        out_shape=jax.ShapeDtypeStruct(x.shape, F32),
        compiler_params=pltpu.CompilerParams(vmem_limit_bytes=VMEM_LIMIT),
        name="gate_out_sample",
    )(att, gate, x, wout)


def _block_diag(w):
    nb, d, _ = w.shape
    eye = jnp.eye(nb, dtype=w.dtype)
    return (eye[:, None, :, None] * w[:, :, None, :]).reshape(nb * d, nb * d)


def _prepare_weights(norm0_g, w_in0, gmlp_v_g, gmlp_w_s, gmlp_b_s, lru_conv_w, lru_conv_b,
                     lru_w_r, lru_b_r, lru_w_i, lru_b_i, lru_lambda, w_out0, norm1_g, w_in1,
                     fox_b_f, q_norm_g, k_norm_g, w_out1):
    lane = np.arange(C_WIDTH) // C_HEAD_DIM
    seg = (lane[:, None] == np.arange(LANES)[None, :]).astype(np.float32)
    win1 = w_in1[:, 0:4 * C_WIDTH].astype(BF16)
    wf = w_in1[:, 4 * C_WIDTH:].astype(BF16)
    wout1 = w_out1.astype(BF16)
    return {
        "norm0_g": norm0_g.reshape(1, D_MODEL),
        "win0": w_in0.astype(BF16),
        "vg": gmlp_v_g.reshape(1, A_WIDTH),
        "ws": gmlp_w_s,
        "bst": gmlp_b_s.T,
        "cw": lru_conv_w,
        "cb": lru_conv_b.reshape(1, B_WIDTH),
        "wr": _block_diag(lru_w_r).astype(BF16),
        "br": lru_b_r.reshape(1, B_WIDTH),
        "wi": _block_diag(lru_w_i).astype(BF16),
        "bi": lru_b_i.reshape(1, B_WIDTH),
        "lam": lru_lambda.reshape(1, B_WIDTH),
        "wout0": w_out0.astype(BF16),
        "norm1_g": norm1_g.reshape(1, D_MODEL),
        "win1": win1,
        "win1_t": win1.T,
        "wf": jnp.pad(wf, ((0, 0), (0, LANES - C_HEADS))),
        "wf_t": wf.T,
        "bf": jnp.pad(fox_b_f, (0, LANES - C_HEADS)).reshape(1, LANES),
        "bf_col": fox_b_f.reshape(C_HEADS, 1),
        "qg": jnp.tile(q_norm_g, C_HEADS).reshape(1, C_WIDTH),
        "kg": jnp.tile(k_norm_g, C_HEADS).reshape(1, C_WIDTH),
        "qg_col": q_norm_g.reshape(C_HEAD_DIM, 1),
        "kg_col": k_norm_g.reshape(C_HEAD_DIM, 1),
        "seg": jnp.asarray(seg, BF16),
        "exp": jnp.asarray(seg.T, BF16),
        "wout1": wout1,
        "wout1_t": wout1.T,
    }


def kernel(x_prompt, x_sample, state_lru_conv, state_lru_h, cache_k, cache_v, cache_logf, page_table, norm0_g, w_in0, gmlp_v_g, gmlp_w_s, gmlp_b_s, lru_conv_w, lru_conv_b, lru_w_r, lru_b_r, lru_w_i, lru_b_i, lru_lambda, w_out0, norm1_g, w_in1, fox_b_f, q_norm_g, k_norm_g, w_out1):
    Bp, L, _ = x_prompt.shape
    Bs = x_sample.shape[0]
    w = _prepare_weights(norm0_g, w_in0, gmlp_v_g, gmlp_w_s, gmlp_b_s, lru_conv_w, lru_conv_b,
                         lru_w_r, lru_b_r, lru_w_i, lru_b_i, lru_lambda, w_out0, norm1_g, w_in1,
                         fox_b_f, q_norm_g, k_norm_g, w_out1)

    conv0 = jnp.zeros((Bp, CONV_W - 1, B_WIDTH), F32)
    h0 = jnp.zeros((Bp, B_WIDTH), F32)
    yp0, lru_conv_p, lru_h_p = _layer0_prompt(x_prompt, conv0, h0, w)
    ys0, gmlp_v_s, conv_s, lru_h_s = _layer0_sample(
        x_sample.reshape(Bs, D_MODEL), state_lru_conv, state_lru_h, w)

    qaug, kaug, kt_p, vt_p, gt_p, lft_p = _fox_proj_prompt(yp0, w)
    q_s, k_s, v_s, g_s, logf_s = _fox_proj_sample(ys0, w)
    att_t, att_s = _fox_attention(qaug, kaug, vt_p, q_s, k_s, v_s, logf_s,
                                  cache_k, cache_v, cache_logf, page_table)
    yp = _gate_out_t(att_t.reshape(Bp, C_WIDTH, L), gt_p, yp0, w["wout1_t"])
    k_p = jnp.transpose(kt_p, (0, 3, 1, 2))
    v_p = jnp.transpose(vt_p, (0, 3, 1, 2))
    logf_p = jnp.transpose(lft_p, (0, 2, 1))
    ys = _gate_out(att_s, g_s, ys0, w["wout1"])

    return (yp, ys.reshape(Bs, 1, D_MODEL), lru_conv_p, lru_h_p, k_p, v_p, logf_p,
            gmlp_v_s.reshape(Bs, 1, A_WIDTH), conv_s.reshape(Bs, CONV_W - 1, B_WIDTH), lru_h_s,
            k_s.reshape(Bs, 1, C_HEADS, C_HEAD_DIM), v_s.reshape(Bs, 1, C_HEADS, C_HEAD_DIM),
            logf_s.reshape(Bs, 1, C_HEADS))
```

```python
import jax
import jax.numpy as jnp
import numpy as np
from jax import lax
from jax.experimental import pallas as pl
from jax.experimental.pallas import tpu as pltpu

D_MODEL = 1024
A_WIDTH = 512
A_GROUPS = 4
A_GROUP_DIM = 128
CHUNK = 128
B_WIDTH = 512
B_BLOCKS = 8
B_BLOCK_DIM = 64
CONV_W = 4
LRU_C = 8.0
C_HEADS = 16
C_HEAD_DIM = 64
C_WIDTH = 1024
PAGE_SIZE = 128
ATTN_SCALE = C_HEAD_DIM ** -0.5
EPS = 1e-6

LANES = 128
SUBLANES = 8
NEG_BIG = -1e30

F32 = jnp.float32
BF16 = jnp.bfloat16

L0_TILE = 512
PROJ_TILE = 512
ATT_T = 512
OUT_TILE = 512
PAGES_PER_STEP = 16
MAX_SLABS = 8
AUG_ROWS = 128
V_AUG_ROWS = 80
LOG2E = 1.4426950408889634
N_BIAS = 3
VMEM_LIMIT = 56 * 1024 * 1024


def _cparams(n_grid_dims):
    return pltpu.CompilerParams(
        dimension_semantics=("arbitrary",) * n_grid_dims,
        vmem_limit_bytes=VMEM_LIMIT,
    )


def _dot(a, b):
    return jnp.dot(a, b, preferred_element_type=F32)


def _dot_nt(a, b):
    return lax.dot_general(a, b, (((1,), (1,)), ((), ())), preferred_element_type=F32)


def _split_bf16(x):
    hi = x.astype(BF16)
    lo = (x - hi.astype(F32)).astype(BF16)
    return hi, lo


def _rms_rows(x, gain):
    ms = jnp.mean(x * x, axis=-1, keepdims=True)
    return x * lax.rsqrt(ms + EPS) * gain


def _gmlp_v_rows(pv, vg_ref):
    v = jax.nn.gelu(pv)
    parts = []
    for g in range(A_GROUPS):
        sl = slice(g * A_GROUP_DIM, (g + 1) * A_GROUP_DIM)
        parts.append(_rms_rows(v[:, sl], vg_ref[:, sl]))
    return jnp.concatenate(parts, axis=-1)


def _lru_gates(xc, wr_ref, br_ref, wi_ref, bi_ref, lam_ref):
    xcb = xc.astype(BF16)
    r = jax.nn.sigmoid(_dot(xcb, wr_ref[...]) + br_ref[...])
    gi = jax.nn.sigmoid(_dot(xcb, wi_ref[...]) + bi_ref[...])
    log_a = -LRU_C * r * jax.nn.softplus(-lam_ref[...])
    a = jnp.exp(log_a)
    bterm = jnp.sqrt(-jnp.tanh(log_a) * (1.0 + a * a)) * (gi * xc)
    return a, bterm


def _layer0_prompt_kernel(x_ref, g_ref, win_ref, vg_ref, ws_ref, bst_ref, cw_ref, cb_ref,
                          wr_ref, br_ref, wi_ref, bi_ref, lam_ref, wout_ref, conv0_ref, h0_ref,
                          y_ref, convo_ref, ho_ref,
                          xbuf, hcar, a8, b8, s_scr):
    T = L0_TILE
    i = pl.program_id(1)

    @pl.when(i == 0)
    def _():
        xbuf[0:SUBLANES, :] = jnp.zeros((SUBLANES, B_WIDTH), F32)
        xbuf[SUBLANES - (CONV_W - 1):SUBLANES, :] = conv0_ref[0]
        hcar[...] = h0_ref[0]

    x = x_ref[0]
    xn = _rms_rows(x, g_ref[...]).astype(BF16)

    def proj(k):
        return _dot(xn, win_ref[:, k * 512:(k + 1) * 512])

    vn = _gmlp_v_rows(proj(1), vg_ref).astype(BF16)
    tri = (lax.broadcasted_iota(jnp.int32, (CHUNK, CHUNK), 0)
           >= lax.broadcasted_iota(jnp.int32, (CHUNK, CHUNK), 1))
    for g in range(A_GROUPS):
        wg = jnp.where(tri, ws_ref[g], 0.0).astype(BF16)
        bias = bst_ref[:, g:g + 1]
        for c in range(T // CHUNK):
            blk = vn[c * CHUNK:(c + 1) * CHUNK, g * A_GROUP_DIM:(g + 1) * A_GROUP_DIM]
            s_scr[c * CHUNK:(c + 1) * CHUNK, g * A_GROUP_DIM:(g + 1) * A_GROUP_DIM] = (
                _dot(wg, blk) + bias)
    u = jax.nn.gelu(proj(0))
    mix_a = (u * s_scr[...] * jax.nn.silu(proj(2))).astype(BF16)

    xb = proj(3)
    xbuf[SUBLANES:SUBLANES + T, :] = xb
    xc = cb_ref[...] + cw_ref[3:4, :] * xb
    for tap in range(CONV_W - 1):
        off = SUBLANES - (CONV_W - 1) + tap
        xc = xc + cw_ref[tap:tap + 1, :] * xbuf[off:off + T, :]
    convo_ref[0] = xbuf[T + SUBLANES - (CONV_W - 1):T + SUBLANES, :]
    xbuf[0:SUBLANES, :] = xbuf[T:T + SUBLANES, :]

    a, bt = _lru_gates(xc, wr_ref, br_ref, wi_ref, bi_ref, lam_ref)

    row = lax.broadcasted_iota(jnp.int32, (T, B_WIDTH), 0) & (SUBLANES - 1)
    shift = 1
    while shift < SUBLANES:
        a_sh = pltpu.roll(a, shift, axis=0)
        b_sh = pltpu.roll(bt, shift, axis=0)
        m = row >= shift
        bt = jnp.where(m, a * b_sh + bt, bt)
        a = jnp.where(m, a * a_sh, a)
        shift *= 2
    a8[...] = a
    b8[...] = bt

    def group_step(j, h):
        off = pl.multiple_of(j * SUBLANES, SUBLANES)
        rows = a8[pl.ds(off, SUBLANES), :] * h + b8[pl.ds(off, SUBLANES), :]
        b8[pl.ds(off, SUBLANES), :] = rows
        return rows[SUBLANES - 1:SUBLANES, :]

    h_last = lax.fori_loop(0, T // SUBLANES, group_step, hcar[...], unroll=True)
    hcar[...] = h_last
    ho_ref[0] = h_last

    mix_b = (b8[...] * jax.nn.silu(proj(4))).astype(BF16)

    y = _dot(mix_a, wout_ref[0:A_WIDTH, :]) + _dot(mix_b, wout_ref[A_WIDTH:, :]) + x
    y_ref[0] = y


def _layer0_prompt(x, conv0, h0, w):
    B, L, _ = x.shape
    T = L0_TILE
    assert L % T == 0 and T % CHUNK == 0
    nt = L // T
    full = lambda shape: pl.BlockSpec(shape, lambda b, i: (0,) * len(shape))
    in_specs = [
        pl.BlockSpec((1, T, D_MODEL), lambda b, i: (b, i, 0)),
        full((1, D_MODEL)),
        full(w["win0"].shape),
        full((1, A_WIDTH)),
        full((A_GROUPS, CHUNK, CHUNK)),
        full((CHUNK, A_GROUPS)),
        full((CONV_W, B_WIDTH)),
        full((1, B_WIDTH)),
        full((B_WIDTH, B_WIDTH)), full((1, B_WIDTH)),
        full((B_WIDTH, B_WIDTH)), full((1, B_WIDTH)),
        full((1, B_WIDTH)),
        full((D_MODEL, D_MODEL)),
        pl.BlockSpec((1, CONV_W - 1, B_WIDTH), lambda b, i: (b, 0, 0)),
        pl.BlockSpec((1, 1, B_WIDTH), lambda b, i: (b, 0, 0)),
    ]
    out_specs = [
        pl.BlockSpec((1, T, D_MODEL), lambda b, i: (b, i, 0)),
        pl.BlockSpec((1, CONV_W - 1, B_WIDTH), lambda b, i: (b, 0, 0)),
        pl.BlockSpec((1, 1, B_WIDTH), lambda b, i: (b, 0, 0)),
    ]
    out_shape = [
        jax.ShapeDtypeStruct((B, L, D_MODEL), F32),
        jax.ShapeDtypeStruct((B, CONV_W - 1, B_WIDTH), F32),
        jax.ShapeDtypeStruct((B, 1, B_WIDTH), F32),
    ]
    y, convo, ho = pl.pallas_call(
        _layer0_prompt_kernel,
        grid=(B, nt),
        in_specs=in_specs,
        out_specs=out_specs,
        out_shape=out_shape,
        scratch_shapes=[
            pltpu.VMEM((T + SUBLANES, B_WIDTH), F32),
            pltpu.VMEM((1, B_WIDTH), F32),
            pltpu.VMEM((T, B_WIDTH), F32),
            pltpu.VMEM((T, B_WIDTH), F32),
            pltpu.VMEM((T, A_WIDTH), F32),
        ],
        compiler_params=_cparams(2),
        name="layer0_prompt",
    )(x, w["norm0_g"], w["win0"], w["vg"], w["ws"], w["bst"], w["cw"], w["cb"],
      w["wr"], w["br"], w["wi"], w["bi"], w["lam"], w["wout0"], conv0, h0.reshape(B, 1, B_WIDTH))
    return y, convo, ho.reshape(B, B_WIDTH)


def _layer0_sample_kernel(x_ref, g_ref, win_ref, vg_ref, ws_ref, bst_ref, cw_ref, cb_ref,
                          wr_ref, br_ref, wi_ref, bi_ref, lam_ref, wout_ref, conv_ref, h_ref,
                          y_ref, v_ref, convo_ref, ho_ref):
    x = x_ref[...]
    xn = _rms_rows(x, g_ref[...]).astype(BF16)

    def proj(k):
        return _dot(xn, win_ref[:, k * 512:(k + 1) * 512])

    vn = _gmlp_v_rows(proj(1), vg_ref)
    v_ref[...] = vn
    s_parts = []
    for g in range(A_GROUPS):
        sl = slice(g * A_GROUP_DIM, (g + 1) * A_GROUP_DIM)
        s_parts.append(ws_ref[g, 0:1, 0:1] * vn[:, sl] + bst_ref[0:1, g:g + 1])
    s = jnp.concatenate(s_parts, axis=-1)
    mix_a = (jax.nn.gelu(proj(0)) * s * jax.nn.silu(proj(2))).astype(BF16)

    xb = proj(3)
    xc = cb_ref[...] + cw_ref[3:4, :] * xb
    for tap in range(CONV_W - 1):
        xc = xc + cw_ref[tap:tap + 1, :] * conv_ref[:, tap * B_WIDTH:(tap + 1) * B_WIDTH]
    for tap in range(CONV_W - 2):
        convo_ref[:, tap * B_WIDTH:(tap + 1) * B_WIDTH] = (
            conv_ref[:, (tap + 1) * B_WIDTH:(tap + 2) * B_WIDTH])
    convo_ref[:, (CONV_W - 2) * B_WIDTH:] = xb

    a, bt = _lru_gates(xc, wr_ref, br_ref, wi_ref, bi_ref, lam_ref)
    h = a * h_ref[...] + bt
    ho_ref[...] = h
    mix_b = (h * jax.nn.silu(proj(4))).astype(BF16)
    y_ref[...] = _dot(mix_a, wout_ref[0:A_WIDTH, :]) + _dot(mix_b, wout_ref[A_WIDTH:, :]) + x


def _layer0_sample(x, conv, h, w):
    n = x.shape[0]
    out_shape = [
        jax.ShapeDtypeStruct((n, D_MODEL), F32),
        jax.ShapeDtypeStruct((n, A_WIDTH), F32),
        jax.ShapeDtypeStruct((n, (CONV_W - 1) * B_WIDTH), F32),
        jax.ShapeDtypeStruct((n, B_WIDTH), F32),
    ]
    return pl.pallas_call(
        _layer0_sample_kernel,
        out_shape=out_shape,
        compiler_params=pltpu.CompilerParams(vmem_limit_bytes=VMEM_LIMIT),
        name="layer0_sample",
    )(x, w["norm0_g"], w["win0"], w["vg"], w["ws"], w["bst"], w["cw"], w["cb"],
      w["wr"], w["br"], w["wi"], w["bi"], w["lam"], w["wout0"],
      conv.reshape(n, (CONV_W - 1) * B_WIDTH), h)


def _head_norm_cols(t, gain_col):
    ms = jnp.mean(t * t, axis=0, keepdims=True)
    return t * lax.rsqrt(ms + EPS) * gain_col


def _fox_proj_prompt_kernel(x_ref, g_ref, wt_ref, wft_ref, bf_ref, qg_ref, kg_ref,
                            qaug_ref, kaug_ref, kt_ref, vt_ref, gt_ref, lf_ref, carry):
    T = PROJ_TILE
    i = pl.program_id(1)

    @pl.when(i == 0)
    def _():
        carry[...] = jnp.zeros_like(carry)

    xn = _rms_rows(x_ref[0], g_ref[...])
    xnt = xn.T.astype(BF16)

    def proj_t(k):
        return _dot(wt_ref[k * C_WIDTH:(k + 1) * C_WIDTH, :], xnt)

    lf = jax.nn.log_sigmoid(_dot(wft_ref[...], xnt) + bf_ref[...])
    lf_ref[0] = lf
    upper = (lax.broadcasted_iota(jnp.int32, (T, T), 0)
             <= lax.broadcasted_iota(jnp.int32, (T, T), 1)).astype(BF16)
    hi, lo = _split_bf16(lf)
    c = _dot(hi, upper) + _dot(lo, upper) + carry[...]
    carry[...] = c[:, T - 1:T]
    pieces, rest = [], c * (-LOG2E)
    for _ in range(N_BIAS):
        p = rest.astype(BF16).astype(F32)
        pieces.append(p)
        rest = rest - p

    sub = lax.broadcasted_iota(jnp.int32, (SUBLANES, T), 0)
    ones_rows = jnp.where(sub < N_BIAS, 1.0, 0.0)
    pad = jnp.zeros((AUG_ROWS - C_HEAD_DIM - SUBLANES, T), F32)
    qt = proj_t(0)
    kt = proj_t(1)
    for h in range(C_HEADS):
        sl = slice(h * C_HEAD_DIM, (h + 1) * C_HEAD_DIM)
        qn = _head_norm_cols(qt[sl], qg_ref[...]) * (ATTN_SCALE * LOG2E)
        kn = _head_norm_cols(kt[sl], kg_ref[...])
        kt_ref[0, h] = kn
        qaug_ref[0, h, 0] = jnp.concatenate([qn, ones_rows, pad], axis=0).astype(BF16)
        bias_rows = jnp.zeros((SUBLANES, T), F32)
        for j in range(N_BIAS):
            bias_rows = jnp.where(sub == j, pieces[j][h:h + 1, :], bias_rows)
        kaug_t = jnp.concatenate([kn, bias_rows, pad], axis=0)
        kaug_ref[0, h] = kaug_t.T.astype(BF16)
    vt_ref[0] = proj_t(2).reshape(C_HEADS, C_HEAD_DIM, T)
    gt_ref[0] = proj_t(3)


def _fox_proj_prompt(x, w):
    B, L, _ = x.shape
    T = PROJ_TILE
    assert L % ATT_T == 0 and ATT_T % T == 0
    per_q = ATT_T // T
    full = lambda shape: pl.BlockSpec(shape, lambda b, i: (0,) * len(shape))
    in_specs = [
        pl.BlockSpec((1, T, D_MODEL), lambda b, i: (b, i, 0)),
        full((1, D_MODEL)), full((4 * C_WIDTH, D_MODEL)), full((C_HEADS, D_MODEL)),
        full((C_HEADS, 1)), full((C_HEAD_DIM, 1)), full((C_HEAD_DIM, 1)),
    ]
    out_specs = [
        pl.BlockSpec((1, C_HEADS, 1, AUG_ROWS, T), lambda b, i: (b, 0, i // per_q, 0, i % per_q)),
        pl.BlockSpec((1, C_HEADS, T, AUG_ROWS), lambda b, i: (b, 0, i, 0)),
        pl.BlockSpec((1, C_HEADS, C_HEAD_DIM, T), lambda b, i: (b, 0, 0, i)),
        pl.BlockSpec((1, C_HEADS, C_HEAD_DIM, T), lambda b, i: (b, 0, 0, i)),
        pl.BlockSpec((1, C_WIDTH, T), lambda b, i: (b, 0, i)),
        pl.BlockSpec((1, C_HEADS, T), lambda b, i: (b, 0, i)),
    ]
    out_shape = [
        jax.ShapeDtypeStruct((B, C_HEADS, L // ATT_T, AUG_ROWS, ATT_T), BF16),
        jax.ShapeDtypeStruct((B, C_HEADS, L, AUG_ROWS), BF16),
        jax.ShapeDtypeStruct((B, C_HEADS, C_HEAD_DIM, L), F32),
        jax.ShapeDtypeStruct((B, C_HEADS, C_HEAD_DIM, L), F32),
        jax.ShapeDtypeStruct((B, C_WIDTH, L), F32),
        jax.ShapeDtypeStruct((B, C_HEADS, L), F32),
    ]
    return pl.pallas_call(
        _fox_proj_prompt_kernel,
        grid=(B, L // T),
        in_specs=in_specs,
        out_specs=out_specs,
        out_shape=out_shape,
        scratch_shapes=[pltpu.VMEM((C_HEADS, 1), F32)],
        compiler_params=_cparams(2),
        name="fox_proj_prompt",
    )(x, w["norm1_g"], w["win1_t"], w["wf_t"], w["bf_col"], w["qg_col"], w["kg_col"])


def _fox_attn_kernel(ki_tab, qi_tab, pt_ref,
                     qaug_ref, kaug_ref, vt_ref, qcol_ref, kcol_ref, vnew_ref, lfn_ref,
                     kcache, vcache, lfcache,
                     o_ref, dec_ref,
                     m_scr, acc_scr, vaug_scr, sa_scr, sb_scr, ma_scr, mb_scr, alpha_a, alpha_b,
                     pages, lfpages, page_sem, lf_sem, qb, s_scr, r_scr, ps_scr, l_scr, dacc):
    nq = qaug_ref.shape[2]
    t = ATT_T
    n_off = nq * (nq - 1) // 2

    d = _DecodeRefs(pt_ref, qcol_ref, kcol_ref, vnew_ref, lfn_ref, kcache, vcache, lfcache, dec_ref,
                    pages, lfpages, page_sem, lf_sem, qb, s_scr, r_scr, ps_scr, l_scr, dacc)
    step = pl.program_id(0) * pl.num_programs(1) + pl.program_id(1)
    n_steps = pl.num_programs(0) * pl.num_programs(1)
    n_off_pairs = n_off // 2 - 1
    unit_stride = n_off_pairs // d.n_units
    assert unit_stride >= 1 and d.n_units % 2 == 0

    @pl.when(step == 0)
    def _():
        _decode_dma(d, 0, 0, 0, 0, start=True)
        _decode_dma(d, 0, 0, 1, 1, start=True)

    m_scr[...] = jnp.full(m_scr.shape, NEG_BIG, F32)
    acc_scr[...] = jnp.zeros(acc_scr.shape, F32)
    extra = lax.broadcasted_iota(jnp.int32, (V_AUG_ROWS - C_HEAD_DIM, t), 0)
    one_row = jnp.where(extra == 0, 1.0, 0.0)
    for kb in range(nq):
        vaug_scr[kb] = jnp.concatenate(
            [vt_ref[0, 0, :, kb * t:(kb + 1) * t], one_row], axis=0).astype(BF16)

    def start(idx, buf, masked):
        s_ref, m_ref, alpha_ref = buf
        ki, qi = ki_tab[idx], qi_tab[idx]
        keys = kaug_ref[0, 0, pl.ds(pl.multiple_of(ki * t, t), t), :]
        s = _dot(keys, qaug_ref[0, 0, qi])
        if masked:
            causal = (lax.broadcasted_iota(jnp.int32, (t, t), 0)
                      <= lax.broadcasted_iota(jnp.int32, (t, t), 1))
            s = jnp.where(causal, s, NEG_BIG)
        s_ref[...] = s
        slab = jnp.max(s.reshape(MAX_SLABS, t // MAX_SLABS, t), axis=0)
        m_old = m_scr[qi]
        m_new = jnp.maximum(m_old, jnp.max(slab, axis=0, keepdims=True))
        alpha_ref[...] = jnp.exp2(m_old - m_new)
        m_ref[...] = m_new
        m_scr[qi] = m_new

    def finish(idx, buf):
        s_ref, m_ref, alpha_ref = buf
        ki, qi = ki_tab[idx], qi_tab[idx]
        p = jnp.exp2(s_ref[...] - m_ref[...]).astype(BF16)
        acc_scr[qi] = alpha_ref[...] * acc_scr[qi] + _dot(vaug_scr[ki], p)

    buf_a = (sa_scr, ma_scr, alpha_a)
    buf_b = (sb_scr, mb_scr, alpha_b)

    def run(first, count, masked, host_decode):
        assert count % 2 == 0 and count >= 2
        start(first, buf_a, masked)

        def pair(j, carry):
            if host_decode:
                @pl.when((j % unit_stride == 0) & (j // unit_stride < d.n_units))
                def _():
                    _decode_unit(d, step, n_steps, j // unit_stride)
            idx = first + 2 * j
            start(idx + 1, buf_b, masked)
            finish(idx, buf_a)
            start(idx + 2, buf_a, masked)
            finish(idx + 1, buf_b)
            return carry

        lax.fori_loop(0, count // 2 - 1, pair, 0)
        start(first + count - 1, buf_b, masked)
        finish(first + count - 2, buf_a)
        finish(first + count - 1, buf_b)

    run(0, n_off, False, True)
    run(n_off, nq, True, False)

    for qi in range(nq):
        o_ref[0, 0, :, qi * t:(qi + 1) * t] = (
            acc_scr[qi, 0:C_HEAD_DIM, :] / acc_scr[qi, C_HEAD_DIM:C_HEAD_DIM + 1, :])


def _fox_attention(qaug, kaug, vt, q_s, k_s, v_s, lf_s, cache_k, cache_v, cache_logf, page_table):
    B, H, nq, _, t = qaug.shape
    L = nq * t
    n, n_pages = page_table.shape
    PP = PAGES_PER_STEP
    assert nq % 2 == 0 and n_pages % PP == 0
    assert B * H == 2 * n, "one decode (batch, phase) per prompt (batch, head) grid step"
    off_diag = [(k, q) for k in range(nq) for q in range(k + 1, nq)]
    blocks = off_diag + [(k, k) for k in range(nq)]
    ki_tab = jnp.asarray(np.array([b[0] for b in blocks], np.int32))
    qi_tab = jnp.asarray(np.array([b[1] for b in blocks], np.int32))
    kt_cache = jnp.transpose(cache_k, (0, 2, 3, 1))
    vt_cache = jnp.transpose(cache_v, (0, 2, 3, 1))
    lft_cache = jnp.transpose(cache_logf, (0, 2, 1))
    to_cols = lambda a: jnp.transpose(a.reshape(n, C_HEADS, C_HEAD_DIM), (0, 2, 1))
    dec_block = lambda shape: pl.BlockSpec(
        (1,) + shape, lambda b, h, kt, qt, pt: ((b * H + h) // 2, 0, 0))
    hbm = pl.BlockSpec(memory_space=pl.ANY)
    grid_spec = pltpu.PrefetchScalarGridSpec(
        num_scalar_prefetch=3,
        grid=(B, H),
        in_specs=[
            pl.BlockSpec((1, 1, nq, AUG_ROWS, t), lambda b, h, kt, qt, pt: (b, h, 0, 0, 0)),
            pl.BlockSpec((1, 1, L, AUG_ROWS), lambda b, h, kt, qt, pt: (b, h, 0, 0)),
            pl.BlockSpec((1, 1, C_HEAD_DIM, L), lambda b, h, kt, qt, pt: (b, h, 0, 0)),
            dec_block((C_HEAD_DIM, C_HEADS)), dec_block((C_HEAD_DIM, C_HEADS)),
            dec_block((C_HEADS, C_HEAD_DIM)), dec_block((C_HEADS, 1)),
            hbm, hbm, hbm,
        ],
        out_specs=[
            pl.BlockSpec((1, 1, C_HEAD_DIM, L), lambda b, h, kt, qt, pt: (b, h, 0, 0)),
            dec_block((C_HEADS, C_HEAD_DIM)),
        ],
        scratch_shapes=[
            pltpu.VMEM((nq, 1, t), F32),
            pltpu.VMEM((nq, V_AUG_ROWS, t), F32),
            pltpu.VMEM((nq, V_AUG_ROWS, t), BF16),
            pltpu.VMEM((t, t), F32),
            pltpu.VMEM((t, t), F32),
            pltpu.VMEM((1, t), F32),
            pltpu.VMEM((1, t), F32),
            pltpu.VMEM((1, t), F32),
            pltpu.VMEM((1, t), F32),
            pltpu.VMEM((2, PP, C_HEADS, C_HEAD_DIM, PAGE_SIZE), F32),
            pltpu.VMEM((2, PP, C_HEADS, PAGE_SIZE), F32),
            pltpu.SemaphoreType.DMA((2,)),
            pltpu.SemaphoreType.DMA((2,)),
            pltpu.VMEM((C_HEADS, C_HEAD_DIM, PAGE_SIZE), F32),
            pltpu.VMEM((n_pages, C_HEADS, PAGE_SIZE), F32),
            pltpu.VMEM((C_HEADS, PAGE_SIZE), F32),
            pltpu.VMEM((C_HEADS, 1), F32),
            pltpu.VMEM((C_HEADS, 1), F32),
            pltpu.VMEM((C_HEADS, C_HEAD_DIM, PAGE_SIZE), F32),
        ],
    )
    att_t, att_s = pl.pallas_call(
        _fox_attn_kernel,
        grid_spec=grid_spec,
        out_shape=[jax.ShapeDtypeStruct((B, H, C_HEAD_DIM, L), F32),
                   jax.ShapeDtypeStruct((n, C_HEADS, C_HEAD_DIM), F32)],
        compiler_params=_cparams(2),
        name="fox_attention",
    )(ki_tab, qi_tab, page_table, qaug, kaug, vt,
      to_cols(q_s), to_cols(k_s), v_s.reshape(n, C_HEADS, C_HEAD_DIM), lf_s.reshape(n, C_HEADS, 1),
      kt_cache, vt_cache, lft_cache)
    return att_t, att_s.reshape(n, C_WIDTH)


def _gate_out_t_kernel(att_ref, gate_ref, x_ref, wt_ref, y_ref):
    mixed = (att_ref[0] * jax.nn.silu(gate_ref[0])).astype(BF16)
    y_ref[0] = _dot(wt_ref[...], mixed).T + x_ref[0]


def _gate_out_t(att_t, gate_t, x, wout_t):
    B, L, _ = x.shape
    T = OUT_TILE
    assert L % T == 0
    col = pl.BlockSpec((1, C_WIDTH, T), lambda b, i: (b, 0, i))
    row = pl.BlockSpec((1, T, D_MODEL), lambda b, i: (b, i, 0))
    return pl.pallas_call(
        _gate_out_t_kernel,
        grid=(B, L // T),
        in_specs=[col, col, row, pl.BlockSpec((D_MODEL, C_WIDTH), lambda b, i: (0, 0))],
        out_specs=row,
        out_shape=jax.ShapeDtypeStruct((B, L, D_MODEL), F32),
        compiler_params=_cparams(2),
        name="gate_out_prompt",
    )(att_t, gate_t, x, wout_t)


def _fox_proj_sample_kernel(x_ref, g_ref, w_ref, wf_ref, bf_ref, qg_ref, kg_ref, seg_ref, exp_ref,
                            q_ref, k_ref, v_ref, gate_ref, lf_ref):
    xn = _rms_rows(x_ref[...], g_ref[...]).astype(BF16)

    def head_norm(t, gain):
        ssq = _dot((t * t).astype(BF16), seg_ref[...])
        rs = lax.rsqrt(ssq * (1.0 / C_HEAD_DIM) + EPS)
        hi, lo = _split_bf16(rs)
        rs_full = _dot(hi, exp_ref[...]) + _dot(lo, exp_ref[...])
        return t * rs_full * gain

    q_ref[...] = head_norm(_dot(xn, w_ref[:, 0:C_WIDTH]), qg_ref[...]) * ATTN_SCALE
    k_ref[...] = head_norm(_dot(xn, w_ref[:, C_WIDTH:2 * C_WIDTH]), kg_ref[...])
    v_ref[...] = _dot(xn, w_ref[:, 2 * C_WIDTH:3 * C_WIDTH])
    gate_ref[...] = _dot(xn, w_ref[:, 3 * C_WIDTH:4 * C_WIDTH])
    logf = jax.nn.log_sigmoid(_dot(xn, wf_ref[...]) + bf_ref[...])
    lf_ref[...] = logf[:, 0:C_HEADS]


def _fox_proj_sample(x, w):
    n = x.shape[0]
    out_shape = [
        jax.ShapeDtypeStruct((n, C_WIDTH), F32),
        jax.ShapeDtypeStruct((n, C_WIDTH), F32),
        jax.ShapeDtypeStruct((n, C_WIDTH), F32),
        jax.ShapeDtypeStruct((n, C_WIDTH), F32),
        jax.ShapeDtypeStruct((n, C_HEADS), F32),
    ]
    return pl.pallas_call(
        _fox_proj_sample_kernel,
        out_shape=out_shape,
        compiler_params=pltpu.CompilerParams(vmem_limit_bytes=VMEM_LIMIT),
        name="fox_proj_sample",
    )(x, w["norm1_g"], w["win1"], w["wf"], w["bf"], w["qg"], w["kg"], w["seg"], w["exp"])


def _sublane_total(x):
    acc = x[0:SUBLANES]
    for r in range(1, x.shape[0] // SUBLANES):
        acc = acc + x[r * SUBLANES:(r + 1) * SUBLANES]
    shift = SUBLANES // 2
    while shift >= 1:
        acc = acc + pltpu.roll(acc, shift, axis=0)
        shift //= 2
    return acc


class _DecodeRefs:
    def __init__(self, pt, qcol, kcol, vnew, lfn, kcache, vcache, lfcache, out,
                 pages, lfpages, page_sem, lf_sem, qb, s_scr, r_scr, ps_scr, l_scr, acc):
        self.pt, self.qcol, self.kcol, self.vnew, self.lfn = pt, qcol, kcol, vnew, lfn
        self.kcache, self.vcache, self.lfcache, self.out = kcache, vcache, lfcache, out
        self.pages, self.lfpages, self.page_sem, self.lf_sem = pages, lfpages, page_sem, lf_sem
        self.qb, self.s_scr, self.r_scr, self.ps_scr, self.l_scr, self.acc = (
            qb, s_scr, r_scr, ps_scr, l_scr, acc)
        self.n_pages = pt.shape[1]
        self.n_units = self.n_pages // PAGES_PER_STEP


def _decode_copies(d, batch, phase_is_v, unit, slot):
    src = d.vcache if phase_is_v else d.kcache
    copies = []
    for i in range(PAGES_PER_STEP):
        page = d.pt[batch, d.n_pages - 1 - (unit * PAGES_PER_STEP + i)]
        copies.append(pltpu.make_async_copy(src.at[page], d.pages.at[slot, i], d.page_sem.at[slot]))
        if not phase_is_v:
            copies.append(pltpu.make_async_copy(d.lfcache.at[page], d.lfpages.at[slot, i],
                                                d.lf_sem.at[slot]))
    return copies


def _decode_dma(d, batch, phase, unit, slot, start):
    for phase_is_v in (False, True):
        @pl.when(phase == int(phase_is_v))
        def _():
            for c in _decode_copies(d, batch, phase_is_v, unit, slot):
                if start:
                    c.start()
                else:
                    c.wait()


def _decode_k_unit(d, unit, slot):
    PP = PAGES_PER_STEP

    @pl.when(unit == 0)
    def _():
        qc = d.qcol[0]
        for h in range(C_HEADS):
            d.qb[h] = jnp.broadcast_to(qc[:, h:h + 1], (C_HEAD_DIM, PAGE_SIZE))
        d.r_scr[...] = jnp.broadcast_to(d.lfn[0], (C_HEADS, PAGE_SIZE))

    slot_i = lax.broadcasted_iota(jnp.int32, (PAGE_SIZE, 2 * PAGE_SIZE), 0)
    slot_j = lax.broadcasted_iota(jnp.int32, (PAGE_SIZE, 2 * PAGE_SIZE), 1)
    sum_mat = jnp.where((slot_j >= PAGE_SIZE) | (slot_i > slot_j), 1.0, 0.0).astype(BF16)
    lf_all = jnp.concatenate([d.lfpages[slot, i] for i in range(PP)], axis=0)
    hi, lo = _split_bf16(lf_all)
    sums = _dot(hi, sum_mat) + _dot(lo, sum_mat)

    sub = lax.broadcasted_iota(jnp.int32, (SUBLANES, PAGE_SIZE), 0)
    groups = C_HEADS // SUBLANES
    s_parts = [[jnp.zeros((SUBLANES, PAGE_SIZE), F32) for _ in range(groups)] for _ in range(PP)]
    for h in range(C_HEADS):
        qh = d.qb[h]
        for i in range(PP):
            tot = _sublane_total(d.pages[slot, i, h] * qh)
            g = h // SUBLANES
            s_parts[i][g] = jnp.where(sub == h % SUBLANES, tot, s_parts[i][g])
    r = d.r_scr[...]
    for i in range(PP):
        s = jnp.concatenate(s_parts[i], axis=0)
        page = sums[i * C_HEADS:(i + 1) * C_HEADS]
        d.s_scr[unit * PP + i] = s + page[:, 0:PAGE_SIZE] + r
        r = r + page[:, PAGE_SIZE:]
    d.r_scr[...] = r


def _decode_v_unit(d, unit, slot):
    PP = PAGES_PER_STEP

    @pl.when(unit == 0)
    def _():
        eye = (lax.broadcasted_iota(jnp.int32, (C_HEADS, C_HEADS), 0)
               == lax.broadcasted_iota(jnp.int32, (C_HEADS, C_HEADS), 1))
        self_row = jnp.sum(d.qcol[0] * d.kcol[0], axis=0, keepdims=True)
        s_self = jnp.sum(jnp.where(eye, self_row, 0.0), axis=1, keepdims=True)
        s_all = d.s_scr[...]
        m = jnp.max(jnp.max(s_all, axis=0), axis=1, keepdims=True)
        m = jnp.maximum(m, s_self)
        p_all = jnp.exp(s_all - m)
        d.s_scr[...] = p_all
        p_self = jnp.exp(s_self - m)
        d.ps_scr[...] = p_self
        d.l_scr[...] = jnp.sum(jnp.sum(p_all, axis=0), axis=1, keepdims=True) + p_self
        d.acc[...] = jnp.zeros(d.acc.shape, F32)

    p_pages = [d.s_scr[unit * PP + i] for i in range(PP)]
    for h in range(C_HEADS):
        a = d.acc[h]
        for i in range(PP):
            a = a + p_pages[i][h:h + 1, :] * d.pages[slot, i, h]
        d.acc[h] = a

    @pl.when(unit == d.n_units - 1)
    def _():
        ones = jnp.ones((SUBLANES, PAGE_SIZE), BF16)
        rows = lax.broadcasted_iota(jnp.int32, (C_HEADS, C_HEAD_DIM), 0)
        out = jnp.zeros((C_HEADS, C_HEAD_DIM), F32)
        for h in range(C_HEADS):
            hi, lo = _split_bf16(d.acc[h])
            tot = _dot_nt(ones, hi) + _dot_nt(ones, lo)
            out = jnp.where(rows == h, tot[0:1, :], out)
        d.out[0] = (out + d.ps_scr[...] * d.vnew[0]) / d.l_scr[...]


def _decode_unit(d, step, n_steps, unit):
    batch, phase = step // 2, step % 2
    slot = unit % 2
    _decode_dma(d, batch, phase, unit, slot, start=False)

    @pl.when(phase == 0)
    def _():
        _decode_k_unit(d, unit, slot)

    @pl.when(phase == 1)
    def _():
        _decode_v_unit(d, unit, slot)

    @pl.when(unit + 2 < d.n_units)
    def _():
        _decode_dma(d, batch, phase, unit + 2, slot, start=True)

    @pl.when((unit + 2 >= d.n_units) & (step < n_steps - 1))
    def _():
        nxt = step + 1
        _decode_dma(d, nxt // 2, nxt % 2, unit + 2 - d.n_units, slot, start=True)


def _gate_out_kernel(att_ref, gate_ref, x_ref, w_ref, y_ref):
    mixed = (att_ref[...] * jax.nn.silu(gate_ref[...])).astype(BF16)
    y_ref[...] = _dot(mixed, w_ref[...]) + x_ref[...]


def _gate_out(att, gate, x, wout):
    return pl.pallas_call(
        _gate_out_kernel,
        out_shape=jax.ShapeDtypeStruct(x.shape, F32),
        compiler_params=pltpu.CompilerParams(vmem_limit_bytes=VMEM_LIMIT),
        name="gate_out_sample",
    )(att, gate, x, wout)


def _block_diag(w):
    nb, d, _ = w.shape
    eye = jnp.eye(nb, dtype=w.dtype)
    return (eye[:, None, :, None] * w[:, :, None, :]).reshape(nb * d, nb * d)


def _prepare_weights(norm0_g, w_in0, gmlp_v_g, gmlp_w_s, gmlp_b_s, lru_conv_w, lru_conv_b,
                     lru_w_r, lru_b_r, lru_w_i, lru_b_i, lru_lambda, w_out0, norm1_g, w_in1,
                     fox_b_f, q_norm_g, k_norm_g, w_out1):
    lane = np.arange(C_WIDTH) // C_HEAD_DIM
    seg = (lane[:, None] == np.arange(LANES)[None, :]).astype(np.float32)
    win1 = w_in1[:, 0:4 * C_WIDTH].astype(BF16)
    wf = w_in1[:, 4 * C_WIDTH:].astype(BF16)
    wout1 = w_out1.astype(BF16)
    return {
        "norm0_g": norm0_g.reshape(1, D_MODEL),
        "win0": w_in0.astype(BF16),
        "vg": gmlp_v_g.reshape(1, A_WIDTH),
        "ws": gmlp_w_s,
        "bst": gmlp_b_s.T,
        "cw": lru_conv_w,
        "cb": lru_conv_b.reshape(1, B_WIDTH),
        "wr": _block_diag(lru_w_r).astype(BF16),
        "br": lru_b_r.reshape(1, B_WIDTH),
        "wi": _block_diag(lru_w_i).astype(BF16),
        "bi": lru_b_i.reshape(1, B_WIDTH),
        "lam": lru_lambda.reshape(1, B_WIDTH),
        "wout0": w_out0.astype(BF16),
        "norm1_g": norm1_g.reshape(1, D_MODEL),
        "win1": win1,
        "win1_t": win1.T,
        "wf": jnp.pad(wf, ((0, 0), (0, LANES - C_HEADS))),
        "wf_t": wf.T,
        "bf": jnp.pad(fox_b_f, (0, LANES - C_HEADS)).reshape(1, LANES),
        "bf_col": fox_b_f.reshape(C_HEADS, 1),
        "qg": jnp.tile(q_norm_g, C_HEADS).reshape(1, C_WIDTH),
        "kg": jnp.tile(k_norm_g, C_HEADS).reshape(1, C_WIDTH),
        "qg_col": q_norm_g.reshape(C_HEAD_DIM, 1),
        "kg_col": k_norm_g.reshape(C_HEAD_DIM, 1),
        "seg": jnp.asarray(seg, BF16),
        "exp": jnp.asarray(seg.T, BF16),
        "wout1": wout1,
        "wout1_t": wout1.T,
    }


def kernel(x_prompt, x_sample, state_lru_conv, state_lru_h, cache_k, cache_v, cache_logf, page_table, norm0_g, w_in0, gmlp_v_g, gmlp_w_s, gmlp_b_s, lru_conv_w, lru_conv_b, lru_w_r, lru_b_r, lru_w_i, lru_b_i, lru_lambda, w_out0, norm1_g, w_in1, fox_b_f, q_norm_g, k_norm_g, w_out1):
    Bp, L, _ = x_prompt.shape
    Bs = x_sample.shape[0]
    w = _prepare_weights(norm0_g, w_in0, gmlp_v_g, gmlp_w_s, gmlp_b_s, lru_conv_w, lru_conv_b,
                         lru_w_r, lru_b_r, lru_w_i, lru_b_i, lru_lambda, w_out0, norm1_g, w_in1,
                         fox_b_f, q_norm_g, k_norm_g, w_out1)

    conv0 = jnp.zeros((Bp, CONV_W - 1, B_WIDTH), F32)
    h0 = jnp.zeros((Bp, B_WIDTH), F32)
    yp0, lru_conv_p, lru_h_p = _layer0_prompt(x_prompt, conv0, h0, w)
    ys0, gmlp_v_s, conv_s, lru_h_s = _layer0_sample(
        x_sample.reshape(Bs, D_MODEL), state_lru_conv, state_lru_h, w)

    qaug, kaug, kt_p, vt_p, gt_p, lft_p = _fox_proj_prompt(yp0, w)
    q_s, k_s, v_s, g_s, logf_s = _fox_proj_sample(ys0, w)
    att_t, att_s = _fox_attention(qaug, kaug, vt_p, q_s, k_s, v_s, logf_s,
                                  cache_k, cache_v, cache_logf, page_table)
    yp = _gate_out_t(att_t.reshape(Bp, C_WIDTH, L), gt_p, yp0, w["wout1_t"])
    k_p = jnp.transpose(kt_p, (0, 3, 1, 2))
    v_p = jnp.transpose(vt_p, (0, 3, 1, 2))
    logf_p = jnp.transpose(lft_p, (0, 2, 1))
    ys = _gate_out(att_s, g_s, ys0, w["wout1"])

    return (yp, ys.reshape(Bs, 1, D_MODEL), lru_conv_p, lru_h_p, k_p, v_p, logf_p,
            gmlp_v_s.reshape(Bs, 1, A_WIDTH), conv_s.reshape(Bs, CONV_W - 1, B_WIDTH), lru_h_s,
            k_s.reshape(Bs, 1, C_HEADS, C_HEAD_DIM), v_s.reshape(Bs, 1, C_HEADS, C_HEAD_DIM),
            logf_s.reshape(Bs, 1, C_HEADS))
```

```python
import jax
import jax.numpy as jnp
import numpy as np
from jax import lax
from jax.experimental import pallas as pl
from jax.experimental.pallas import tpu as pltpu

D_MODEL = 1024
A_WIDTH = 512
A_GROUPS = 4
A_GROUP_DIM = 128
CHUNK = 128
B_WIDTH = 512
B_BLOCKS = 8
B_BLOCK_DIM = 64
CONV_W = 4
LRU_C = 8.0
C_HEADS = 16
C_HEAD_DIM = 64
C_WIDTH = 1024
PAGE_SIZE = 128
ATTN_SCALE = C_HEAD_DIM ** -0.5
EPS = 1e-6

LANES = 128
SUBLANES = 8
NEG_BIG = -1e30

F32 = jnp.float32
BF16 = jnp.bfloat16

L0_TILE = 512
PROJ_TILE = 512
ATT_T = 512
OUT_TILE = 512
PAGES_PER_STEP = 16
ATT_UNROLL = 4
MAX_SLABS = 8
AUG_ROWS = 128
V_AUG_ROWS = 80
LOG2E = 1.4426950408889634
N_BIAS = 3
VMEM_LIMIT = 56 * 1024 * 1024


def _cparams(n_grid_dims):
    return pltpu.CompilerParams(
        dimension_semantics=("arbitrary",) * n_grid_dims,
        vmem_limit_bytes=VMEM_LIMIT,
    )


def _dot(a, b):
    return jnp.dot(a, b, preferred_element_type=F32)


def _dot_nt(a, b):
    return lax.dot_general(a, b, (((1,), (1,)), ((), ())), preferred_element_type=F32)


def _split_bf16(x):
    hi = x.astype(BF16)
    lo = (x - hi.astype(F32)).astype(BF16)
    return hi, lo


def _rms_rows(x, gain):
    ms = jnp.mean(x * x, axis=-1, keepdims=True)
    return x * lax.rsqrt(ms + EPS) * gain


def _gelu_tanh(x):
    c1 = 0.7978845608028654
    hx = 0.5 * x
    return hx + hx * jnp.tanh(x * (c1 + (c1 * 0.044715) * (x * x)))


def _gmlp_v_rows(pv, vg_ref):
    v = _gelu_tanh(pv)
    parts = []
    for g in range(A_GROUPS):
        sl = slice(g * A_GROUP_DIM, (g + 1) * A_GROUP_DIM)
        parts.append(_rms_rows(v[:, sl], vg_ref[:, sl]))
    return jnp.concatenate(parts, axis=-1)


def _lru_gates(xc, wr_ref, br_ref, wi_ref, bi_ref, lam_ref):
    xcb = xc.astype(BF16)
    r = jax.nn.sigmoid(_dot(xcb, wr_ref[...]) + br_ref[...])
    gi = jax.nn.sigmoid(_dot(xcb, wi_ref[...]) + bi_ref[...])
    log_a = -LRU_C * r * jax.nn.softplus(-lam_ref[...])
    a = jnp.exp(log_a)
    bterm = jnp.sqrt(-jnp.tanh(log_a) * (1.0 + a * a)) * (gi * xc)
    return a, bterm


def _layer0_prompt_kernel(x_ref, g_ref, win_ref, vg_ref, ws_ref, bst_ref, cw_ref, cb_ref,
                          wr_ref, br_ref, wi_ref, bi_ref, lam_ref, wout_ref, conv0_ref, h0_ref,
                          y_ref, convo_ref, ho_ref,
                          xbuf, hcar, a8, b8, s_scr):
    T = L0_TILE
    i = pl.program_id(1)

    @pl.when(i == 0)
    def _():
        xbuf[0:SUBLANES, :] = jnp.zeros((SUBLANES, B_WIDTH), F32)
        xbuf[SUBLANES - (CONV_W - 1):SUBLANES, :] = conv0_ref[0]
        hcar[...] = h0_ref[0]

    x = x_ref[0]
    xn = _rms_rows(x, g_ref[...]).astype(BF16)

    def proj(k):
        return _dot(xn, win_ref[:, k * 512:(k + 1) * 512])

    vn = _gmlp_v_rows(proj(1), vg_ref).astype(BF16)
    tri = (lax.broadcasted_iota(jnp.int32, (CHUNK, CHUNK), 0)
           >= lax.broadcasted_iota(jnp.int32, (CHUNK, CHUNK), 1))
    for g in range(A_GROUPS):
        wg = jnp.where(tri, ws_ref[g], 0.0).astype(BF16)
        bias = bst_ref[:, g:g + 1]
        for c in range(T // CHUNK):
            blk = vn[c * CHUNK:(c + 1) * CHUNK, g * A_GROUP_DIM:(g + 1) * A_GROUP_DIM]
            s_scr[c * CHUNK:(c + 1) * CHUNK, g * A_GROUP_DIM:(g + 1) * A_GROUP_DIM] = (
                _dot(wg, blk) + bias)
    u = _gelu_tanh(proj(0))
    mix_a = (u * s_scr[...] * jax.nn.silu(proj(2))).astype(BF16)

    xb = proj(3)
    xbuf[SUBLANES:SUBLANES + T, :] = xb
    xc = cb_ref[...] + cw_ref[3:4, :] * xb
    for tap in range(CONV_W - 1):
        off = SUBLANES - (CONV_W - 1) + tap
        xc = xc + cw_ref[tap:tap + 1, :] * xbuf[off:off + T, :]
    convo_ref[0] = xbuf[T + SUBLANES - (CONV_W - 1):T + SUBLANES, :]
    xbuf[0:SUBLANES, :] = xbuf[T:T + SUBLANES, :]

    a, bt = _lru_gates(xc, wr_ref, br_ref, wi_ref, bi_ref, lam_ref)

    a = a.reshape(T // SUBLANES, SUBLANES, B_WIDTH)
    bt = bt.reshape(T // SUBLANES, SUBLANES, B_WIDTH)
    row = lax.broadcasted_iota(jnp.int32, (1, SUBLANES, B_WIDTH), 1)
    shift = 1
    while shift < SUBLANES:
        a_sh = pltpu.roll(a, shift, axis=1)
        b_sh = pltpu.roll(bt, shift, axis=1)
        m = row >= shift
        bt = jnp.where(m, a * b_sh + bt, bt)
        a = jnp.where(m, a * a_sh, a)
        shift *= 2
    a8[...] = a.reshape(T, B_WIDTH)
    b8[...] = bt.reshape(T, B_WIDTH)

    def group_step(j, h):
        off = pl.multiple_of(j * SUBLANES, SUBLANES)
        rows = a8[pl.ds(off, SUBLANES), :] * h + b8[pl.ds(off, SUBLANES), :]
        b8[pl.ds(off, SUBLANES), :] = rows
        return rows[SUBLANES - 1:SUBLANES, :]

    h_last = lax.fori_loop(0, T // SUBLANES, group_step, hcar[...], unroll=True)
    hcar[...] = h_last
    ho_ref[0] = h_last

    mix_b = (b8[...] * jax.nn.silu(proj(4))).astype(BF16)

    y = _dot(mix_a, wout_ref[0:A_WIDTH, :]) + _dot(mix_b, wout_ref[A_WIDTH:, :]) + x
    y_ref[0] = y


def _layer0_prompt(x, conv0, h0, w):
    B, L, _ = x.shape
    T = L0_TILE
    assert L % T == 0 and T % CHUNK == 0
    nt = L // T
    full = lambda shape: pl.BlockSpec(shape, lambda b, i: (0,) * len(shape))
    in_specs = [
        pl.BlockSpec((1, T, D_MODEL), lambda b, i: (b, i, 0)),
        full((1, D_MODEL)),
        full(w["win0"].shape),
        full((1, A_WIDTH)),
        full((A_GROUPS, CHUNK, CHUNK)),
        full((CHUNK, A_GROUPS)),
        full((CONV_W, B_WIDTH)),
        full((1, B_WIDTH)),
        full((B_WIDTH, B_WIDTH)), full((1, B_WIDTH)),
        full((B_WIDTH, B_WIDTH)), full((1, B_WIDTH)),
        full((1, B_WIDTH)),
        full((D_MODEL, D_MODEL)),
        pl.BlockSpec((1, CONV_W - 1, B_WIDTH), lambda b, i: (b, 0, 0)),
        pl.BlockSpec((1, 1, B_WIDTH), lambda b, i: (b, 0, 0)),
    ]
    out_specs = [
        pl.BlockSpec((1, T, D_MODEL), lambda b, i: (b, i, 0)),
        pl.BlockSpec((1, CONV_W - 1, B_WIDTH), lambda b, i: (b, 0, 0)),
        pl.BlockSpec((1, 1, B_WIDTH), lambda b, i: (b, 0, 0)),
    ]
    out_shape = [
        jax.ShapeDtypeStruct((B, L, D_MODEL), F32),
        jax.ShapeDtypeStruct((B, CONV_W - 1, B_WIDTH), F32),
        jax.ShapeDtypeStruct((B, 1, B_WIDTH), F32),
    ]
    y, convo, ho = pl.pallas_call(
        _layer0_prompt_kernel,
        grid=(B, nt),
        in_specs=in_specs,
        out_specs=out_specs,
        out_shape=out_shape,
        scratch_shapes=[
            pltpu.VMEM((T + SUBLANES, B_WIDTH), F32),
            pltpu.VMEM((1, B_WIDTH), F32),
            pltpu.VMEM((T, B_WIDTH), F32),
            pltpu.VMEM((T, B_WIDTH), F32),
            pltpu.VMEM((T, A_WIDTH), F32),
        ],
        compiler_params=_cparams(2),
        name="layer0_prompt",
    )(x, w["norm0_g"], w["win0"], w["vg"], w["ws"], w["bst"], w["cw"], w["cb"],
      w["wr"], w["br"], w["wi"], w["bi"], w["lam"], w["wout0"], conv0, h0.reshape(B, 1, B_WIDTH))
    return y, convo, ho.reshape(B, B_WIDTH)


def _layer0_sample_kernel(x_ref, g_ref, win_ref, vg_ref, ws_ref, bst_ref, cw_ref, cb_ref,
                          wr_ref, br_ref, wi_ref, bi_ref, lam_ref, wout_ref, conv_ref, h_ref,
                          y_ref, v_ref, convo_ref, ho_ref):
    x = x_ref[...]
    xn = _rms_rows(x, g_ref[...]).astype(BF16)

    def proj(k):
        return _dot(xn, win_ref[:, k * 512:(k + 1) * 512])

    vn = _gmlp_v_rows(proj(1), vg_ref)
    v_ref[...] = vn
    s_parts = []
    for g in range(A_GROUPS):
        sl = slice(g * A_GROUP_DIM, (g + 1) * A_GROUP_DIM)
        s_parts.append(ws_ref[g, 0:1, 0:1] * vn[:, sl] + bst_ref[0:1, g:g + 1])
    s = jnp.concatenate(s_parts, axis=-1)
    mix_a = (_gelu_tanh(proj(0)) * s * jax.nn.silu(proj(2))).astype(BF16)

    xb = proj(3)
    xc = cb_ref[...] + cw_ref[3:4, :] * xb
    for tap in range(CONV_W - 1):
        xc = xc + cw_ref[tap:tap + 1, :] * conv_ref[:, tap * B_WIDTH:(tap + 1) * B_WIDTH]
    for tap in range(CONV_W - 2):
        convo_ref[:, tap * B_WIDTH:(tap + 1) * B_WIDTH] = (
            conv_ref[:, (tap + 1) * B_WIDTH:(tap + 2) * B_WIDTH])
    convo_ref[:, (CONV_W - 2) * B_WIDTH:] = xb

    a, bt = _lru_gates(xc, wr_ref, br_ref, wi_ref, bi_ref, lam_ref)
    h = a * h_ref[...] + bt
    ho_ref[...] = h
    mix_b = (h * jax.nn.silu(proj(4))).astype(BF16)
    y_ref[...] = _dot(mix_a, wout_ref[0:A_WIDTH, :]) + _dot(mix_b, wout_ref[A_WIDTH:, :]) + x


def _layer0_sample(x, conv, h, w):
    n = x.shape[0]
    out_shape = [
        jax.ShapeDtypeStruct((n, D_MODEL), F32),
        jax.ShapeDtypeStruct((n, A_WIDTH), F32),
        jax.ShapeDtypeStruct((n, (CONV_W - 1) * B_WIDTH), F32),
        jax.ShapeDtypeStruct((n, B_WIDTH), F32),
    ]
    return pl.pallas_call(
        _layer0_sample_kernel,
        out_shape=out_shape,
        compiler_params=pltpu.CompilerParams(vmem_limit_bytes=VMEM_LIMIT),
        name="layer0_sample",
    )(x, w["norm0_g"], w["win0"], w["vg"], w["ws"], w["bst"], w["cw"], w["cb"],
      w["wr"], w["br"], w["wi"], w["bi"], w["lam"], w["wout0"],
      conv.reshape(n, (CONV_W - 1) * B_WIDTH), h)


def _head_norm_cols(t, gain_col):
    ms = jnp.mean(t * t, axis=0, keepdims=True)
    return t * lax.rsqrt(ms + EPS) * gain_col


def _fox_proj_prompt_kernel(x_ref, g_ref, wt_ref, wft_ref, bf_ref, qg_ref, kg_ref,
                            qaug_ref, kaug_ref, kt_ref, vt_ref, gt_ref, lf_ref, carry):
    T = PROJ_TILE
    i = pl.program_id(1)

    @pl.when(i == 0)
    def _():
        carry[...] = jnp.zeros_like(carry)

    xn = _rms_rows(x_ref[0], g_ref[...])
    xnt = xn.T.astype(BF16)

    def proj_t(k):
        return _dot(wt_ref[k * C_WIDTH:(k + 1) * C_WIDTH, :], xnt)

    lf = jax.nn.log_sigmoid(_dot(wft_ref[...], xnt) + bf_ref[...])
    lf_ref[0] = lf
    upper = (lax.broadcasted_iota(jnp.int32, (T, T), 0)
             <= lax.broadcasted_iota(jnp.int32, (T, T), 1)).astype(BF16)
    hi, lo = _split_bf16(lf)
    c = _dot(hi, upper) + _dot(lo, upper) + carry[...]
    carry[...] = c[:, T - 1:T]
    pieces, rest = [], c * (-LOG2E)
    for _ in range(N_BIAS):
        p = rest.astype(BF16).astype(F32)
        pieces.append(p)
        rest = rest - p

    sub = lax.broadcasted_iota(jnp.int32, (SUBLANES, T), 0)
    ones_rows = jnp.where(sub < N_BIAS, 1.0, 0.0)
    pad = jnp.zeros((AUG_ROWS - C_HEAD_DIM - SUBLANES, T), F32)
    qt = proj_t(0)
    kt = proj_t(1)
    for h in range(C_HEADS):
        sl = slice(h * C_HEAD_DIM, (h + 1) * C_HEAD_DIM)
        qn = _head_norm_cols(qt[sl], qg_ref[...]) * (ATTN_SCALE * LOG2E)
        kn = _head_norm_cols(kt[sl], kg_ref[...])
        kt_ref[0, h] = kn
        qaug_ref[0, h, 0] = jnp.concatenate([qn, ones_rows, pad], axis=0).astype(BF16)
        bias_rows = jnp.zeros((SUBLANES, T), F32)
        for j in range(N_BIAS):
            bias_rows = jnp.where(sub == j, pieces[j][h:h + 1, :], bias_rows)
        kaug_t = jnp.concatenate([kn, bias_rows, pad], axis=0)
        kaug_ref[0, h] = kaug_t.T.astype(BF16)
    vt_ref[0] = proj_t(2).reshape(C_HEADS, C_HEAD_DIM, T)
    gt_ref[0] = proj_t(3)


def _fox_proj_prompt(x, w):
    B, L, _ = x.shape
    T = PROJ_TILE
    assert L % ATT_T == 0 and ATT_T % T == 0
    per_q = ATT_T // T
    full = lambda shape: pl.BlockSpec(shape, lambda b, i: (0,) * len(shape))
    in_specs = [
        pl.BlockSpec((1, T, D_MODEL), lambda b, i: (b, i, 0)),
        full((1, D_MODEL)), full((4 * C_WIDTH, D_MODEL)), full((C_HEADS, D_MODEL)),
        full((C_HEADS, 1)), full((C_HEAD_DIM, 1)), full((C_HEAD_DIM, 1)),
    ]
    out_specs = [
        pl.BlockSpec((1, C_HEADS, 1, AUG_ROWS, T), lambda b, i: (b, 0, i // per_q, 0, i % per_q)),
        pl.BlockSpec((1, C_HEADS, T, AUG_ROWS), lambda b, i: (b, 0, i, 0)),
        pl.BlockSpec((1, C_HEADS, C_HEAD_DIM, T), lambda b, i: (b, 0, 0, i)),
        pl.BlockSpec((1, C_HEADS, C_HEAD_DIM, T), lambda b, i: (b, 0, 0, i)),
        pl.BlockSpec((1, C_WIDTH, T), lambda b, i: (b, 0, i)),
        pl.BlockSpec((1, C_HEADS, T), lambda b, i: (b, 0, i)),
    ]
    out_shape = [
        jax.ShapeDtypeStruct((B, C_HEADS, L // ATT_T, AUG_ROWS, ATT_T), BF16),
        jax.ShapeDtypeStruct((B, C_HEADS, L, AUG_ROWS), BF16),
        jax.ShapeDtypeStruct((B, C_HEADS, C_HEAD_DIM, L), F32),
        jax.ShapeDtypeStruct((B, C_HEADS, C_HEAD_DIM, L), F32),
        jax.ShapeDtypeStruct((B, C_WIDTH, L), F32),
        jax.ShapeDtypeStruct((B, C_HEADS, L), F32),
    ]
    return pl.pallas_call(
        _fox_proj_prompt_kernel,
        grid=(B, L // T),
        in_specs=in_specs,
        out_specs=out_specs,
        out_shape=out_shape,
        scratch_shapes=[pltpu.VMEM((C_HEADS, 1), F32)],
        compiler_params=_cparams(2),
        name="fox_proj_prompt",
    )(x, w["norm1_g"], w["win1_t"], w["wf_t"], w["bf_col"], w["qg_col"], w["kg_col"])


def _fox_attn_kernel(ki_tab, qi_tab, pt_ref,
                     qaug_ref, kaug_ref, vt_ref, qcol_ref, kcol_ref, vnew_ref, lfn_ref,
                     kcache, vcache, lfcache,
                     o_ref, dec_ref,
                     m_scr, acc_scr, vaug_scr, sa_scr, sb_scr, ma_scr, mb_scr, alpha_a, alpha_b,
                     pages, lfpages, page_sem, lf_sem, qb, s_scr, r_scr, ps_scr, l_scr, dacc):
    nq = qaug_ref.shape[2]
    t = ATT_T
    n_off = nq * (nq - 1) // 2

    d = _DecodeRefs(pt_ref, qcol_ref, kcol_ref, vnew_ref, lfn_ref, kcache, vcache, lfcache, dec_ref,
                    pages, lfpages, page_sem, lf_sem, qb, s_scr, r_scr, ps_scr, l_scr, dacc)
    step = pl.program_id(0) * pl.num_programs(1) + pl.program_id(1)
    n_steps = pl.num_programs(0) * pl.num_programs(1)
    assert d.n_units % 2 == 0

    @pl.when(step == 0)
    def _():
        _decode_dma(d, 0, False, 0, 0, start=True)
        _decode_dma(d, 0, False, 1, 1, start=True)

    m_scr[...] = jnp.full(m_scr.shape, NEG_BIG, F32)
    acc_scr[...] = jnp.zeros(acc_scr.shape, F32)
    extra = lax.broadcasted_iota(jnp.int32, (V_AUG_ROWS - C_HEAD_DIM, t), 0)
    one_row = jnp.where(extra == 0, 1.0, 0.0)
    for kb in range(nq):
        vaug_scr[kb] = jnp.concatenate(
            [vt_ref[0, 0, :, kb * t:(kb + 1) * t], one_row], axis=0).astype(BF16)

    def start(idx, buf, masked):
        s_ref, m_ref, alpha_ref = buf
        ki, qi = ki_tab[idx], qi_tab[idx]
        keys = kaug_ref[0, 0, pl.ds(pl.multiple_of(ki * t, t), t), :]
        s = _dot(keys, qaug_ref[0, 0, qi])
        if masked:
            causal = (lax.broadcasted_iota(jnp.int32, (t, t), 0)
                      <= lax.broadcasted_iota(jnp.int32, (t, t), 1))
            s = jnp.where(causal, s, NEG_BIG)
        s_ref[...] = s
        slab = jnp.max(s.reshape(MAX_SLABS, t // MAX_SLABS, t), axis=0)
        m_old = m_scr[qi]
        m_new = jnp.maximum(m_old, jnp.max(slab, axis=0, keepdims=True))
        alpha_ref[...] = jnp.exp2(m_old - m_new)
        m_ref[...] = m_new
        m_scr[qi] = m_new

    def finish(idx, buf):
        s_ref, m_ref, alpha_ref = buf
        ki, qi = ki_tab[idx], qi_tab[idx]
        p = jnp.exp2(s_ref[...] - m_ref[...]).astype(BF16)
        acc_scr[qi] = alpha_ref[...] * acc_scr[qi] + _dot(vaug_scr[ki], p)

    buf_a = (sa_scr, ma_scr, alpha_a)
    buf_b = (sb_scr, mb_scr, alpha_b)

    bufs = (buf_a, buf_b)

    def run(first, count, masked, host_decode):
        u = ATT_UNROLL
        assert u % 2 == 0 and count % u == 0
        n_iter = count // u - 1
        unit_stride = max(n_iter // d.n_units, 1)
        assert not host_decode or n_iter >= d.n_units
        start(first, buf_a, masked)

        def body(it, carry):
            if host_decode:
                @pl.when((it % unit_stride == 0) & (it // unit_stride < d.n_units))
                def _():
                    _decode_unit(d, step, n_steps, it // unit_stride)
            base = first + u * it
            for k in range(u):
                start(base + k + 1, bufs[(k + 1) % 2], masked)
                finish(base + k, bufs[k % 2])
            return carry

        lax.fori_loop(0, n_iter, body, 0)
        base = first + u * n_iter
        for k in range(u - 1):
            start(base + k + 1, bufs[(k + 1) % 2], masked)
            finish(base + k, bufs[k % 2])
        finish(base + u - 1, bufs[(u - 1) % 2])

    run(0, n_off, False, True)
    run(n_off, nq, True, False)

    for qi in range(nq):
        o_ref[0, 0, :, qi * t:(qi + 1) * t] = (
            acc_scr[qi, 0:C_HEAD_DIM, :] / acc_scr[qi, C_HEAD_DIM:C_HEAD_DIM + 1, :])


def _fox_attention(qaug, kaug, vt, q_s, k_s, v_s, lf_s, cache_k, cache_v, cache_logf, page_table):
    B, H, nq, _, t = qaug.shape
    L = nq * t
    n, n_pages = page_table.shape
    PP = PAGES_PER_STEP
    assert nq % 2 == 0 and n_pages % PP == 0
    assert B * H == 2 * n, "one decode (batch, phase) per prompt (batch, head) grid step"
    off_diag = [(k, q) for k in range(nq) for q in range(k + 1, nq)]
    blocks = off_diag + [(k, k) for k in range(nq)]
    ki_tab = jnp.asarray(np.array([b[0] for b in blocks], np.int32))
    qi_tab = jnp.asarray(np.array([b[1] for b in blocks], np.int32))
    kt_cache = jnp.transpose(cache_k, (0, 2, 3, 1))
    vt_cache = jnp.transpose(cache_v, (0, 2, 3, 1))
    lft_cache = jnp.transpose(cache_logf, (0, 2, 1))
    to_cols = lambda a: jnp.transpose(a.reshape(n, C_HEADS, C_HEAD_DIM), (0, 2, 1))
    dec_block = lambda shape: pl.BlockSpec(
        (1,) + shape, lambda b, h, kt, qt, pt: ((b * H + h) // 2, 0, 0))
    hbm = pl.BlockSpec(memory_space=pl.ANY)
    grid_spec = pltpu.PrefetchScalarGridSpec(
        num_scalar_prefetch=3,
        grid=(B, H),
        in_specs=[
            pl.BlockSpec((1, 1, nq, AUG_ROWS, t), lambda b, h, kt, qt, pt: (b, h, 0, 0, 0)),
            pl.BlockSpec((1, 1, L, AUG_ROWS), lambda b, h, kt, qt, pt: (b, h, 0, 0)),
            pl.BlockSpec((1, 1, C_HEAD_DIM, L), lambda b, h, kt, qt, pt: (b, h, 0, 0)),
            dec_block((C_HEAD_DIM, C_HEADS)), dec_block((C_HEAD_DIM, C_HEADS)),
            dec_block((C_HEADS, C_HEAD_DIM)), dec_block((C_HEADS, 1)),
            hbm, hbm, hbm,
        ],
        out_specs=[
            pl.BlockSpec((1, 1, C_HEAD_DIM, L), lambda b, h, kt, qt, pt: (b, h, 0, 0)),
            dec_block((C_HEADS, C_HEAD_DIM)),
        ],
        scratch_shapes=[
            pltpu.VMEM((nq, 1, t), F32),
            pltpu.VMEM((nq, V_AUG_ROWS, t), F32),
            pltpu.VMEM((nq, V_AUG_ROWS, t), BF16),
            pltpu.VMEM((t, t), F32),
            pltpu.VMEM((t, t), F32),
            pltpu.VMEM((1, t), F32),
            pltpu.VMEM((1, t), F32),
            pltpu.VMEM((1, t), F32),
            pltpu.VMEM((1, t), F32),
            pltpu.VMEM((2, PP, C_HEADS, C_HEAD_DIM, PAGE_SIZE), F32),
            pltpu.VMEM((2, PP, C_HEADS, PAGE_SIZE), F32),
            pltpu.SemaphoreType.DMA((2,)),
            pltpu.SemaphoreType.DMA((2,)),
            pltpu.VMEM((C_HEADS, C_HEAD_DIM, PAGE_SIZE), F32),
            pltpu.VMEM((n_pages, C_HEADS, PAGE_SIZE), F32),
            pltpu.VMEM((C_HEADS, PAGE_SIZE), F32),
            pltpu.VMEM((C_HEADS, 1), F32),
            pltpu.VMEM((C_HEADS, 1), F32),
            pltpu.VMEM((C_HEADS, C_HEAD_DIM, PAGE_SIZE), F32),
        ],
    )
    att_t, att_s = pl.pallas_call(
        _fox_attn_kernel,
        grid_spec=grid_spec,
        out_shape=[jax.ShapeDtypeStruct((B, H, C_HEAD_DIM, L), F32),
                   jax.ShapeDtypeStruct((n, C_HEADS, C_HEAD_DIM), F32)],
        compiler_params=_cparams(2),
        name="fox_attention",
    )(ki_tab, qi_tab, page_table, qaug, kaug, vt,
      to_cols(q_s), to_cols(k_s), v_s.reshape(n, C_HEADS, C_HEAD_DIM), lf_s.reshape(n, C_HEADS, 1),
      kt_cache, vt_cache, lft_cache)
    return att_t, att_s.reshape(n, C_WIDTH)


def _gate_out_t_kernel(att_ref, gate_ref, x_ref, wt_ref, y_ref):
    mixed = (att_ref[0] * jax.nn.silu(gate_ref[0])).astype(BF16)
    y_ref[0] = _dot(wt_ref[...], mixed).T + x_ref[0]


def _gate_out_t(att_t, gate_t, x, wout_t):
    B, L, _ = x.shape
    T = OUT_TILE
    assert L % T == 0
    col = pl.BlockSpec((1, C_WIDTH, T), lambda b, i: (b, 0, i))
    row = pl.BlockSpec((1, T, D_MODEL), lambda b, i: (b, i, 0))
    return pl.pallas_call(
        _gate_out_t_kernel,
        grid=(B, L // T),
        in_specs=[col, col, row, pl.BlockSpec((D_MODEL, C_WIDTH), lambda b, i: (0, 0))],
        out_specs=row,
        out_shape=jax.ShapeDtypeStruct((B, L, D_MODEL), F32),
        compiler_params=_cparams(2),
        name="gate_out_prompt",
    )(att_t, gate_t, x, wout_t)


def _fox_proj_sample_kernel(x_ref, g_ref, wt_ref, wft_ref, bf_ref, qg_ref, kg_ref, seg_ref, exp_ref,
                            q_ref, k_ref, v_ref, gate_ref, lf_ref):
    xn = _rms_rows(x_ref[...], g_ref[...]).astype(BF16)

    def head_norm(t, gain):
        ssq = _dot((t * t).astype(BF16), seg_ref[...])
        rs = lax.rsqrt(ssq * (1.0 / C_HEAD_DIM) + EPS)
        hi, lo = _split_bf16(rs)
        rs_full = _dot(hi, exp_ref[...]) + _dot(lo, exp_ref[...])
        return t * rs_full * gain

    def proj(k):
        return _dot_nt(xn, wt_ref[k * C_WIDTH:(k + 1) * C_WIDTH, :])

    q_ref[...] = head_norm(proj(0), qg_ref[...]) * ATTN_SCALE
    k_ref[...] = head_norm(proj(1), kg_ref[...])
    v_ref[...] = proj(2)
    gate_ref[...] = proj(3)
    lf_ref[...] = jax.nn.log_sigmoid(_dot_nt(xn, wft_ref[...]) + bf_ref[...])


def _fox_proj_sample(x, w):
    n = x.shape[0]
    out_shape = [
        jax.ShapeDtypeStruct((n, C_WIDTH), F32),
        jax.ShapeDtypeStruct((n, C_WIDTH), F32),
        jax.ShapeDtypeStruct((n, C_WIDTH), F32),
        jax.ShapeDtypeStruct((n, C_WIDTH), F32),
        jax.ShapeDtypeStruct((n, C_HEADS), F32),
    ]
    return pl.pallas_call(
        _fox_proj_sample_kernel,
        out_shape=out_shape,
        compiler_params=pltpu.CompilerParams(vmem_limit_bytes=VMEM_LIMIT),
        name="fox_proj_sample",
    )(x, w["norm1_g"], w["win1_t"], w["wf_t"], w["bf_row"], w["qg"], w["kg"], w["seg"], w["exp"])


def _sublane_total(x):
    acc = x[0:SUBLANES]
    for r in range(1, x.shape[0] // SUBLANES):
        acc = acc + x[r * SUBLANES:(r + 1) * SUBLANES]
    shift = SUBLANES // 2
    while shift >= 1:
        acc = acc + pltpu.roll(acc, shift, axis=0)
        shift //= 2
    return acc


class _DecodeRefs:
    def __init__(self, pt, qcol, kcol, vnew, lfn, kcache, vcache, lfcache, out,
                 pages, lfpages, page_sem, lf_sem, qb, s_scr, r_scr, ps_scr, l_scr, acc):
        self.pt, self.qcol, self.kcol, self.vnew, self.lfn = pt, qcol, kcol, vnew, lfn
        self.kcache, self.vcache, self.lfcache, self.out = kcache, vcache, lfcache, out
        self.pages, self.lfpages, self.page_sem, self.lf_sem = pages, lfpages, page_sem, lf_sem
        self.qb, self.s_scr, self.r_scr, self.ps_scr, self.l_scr, self.acc = (
            qb, s_scr, r_scr, ps_scr, l_scr, acc)
        self.n_pages = pt.shape[1]
        self.n_units = self.n_pages // PAGES_PER_STEP


def _decode_copies(d, batch, phase_is_v, unit, slot):
    src = d.vcache if phase_is_v else d.kcache
    copies = []
    for i in range(PAGES_PER_STEP):
        page = d.pt[batch, d.n_pages - 1 - (unit * PAGES_PER_STEP + i)]
        copies.append(pltpu.make_async_copy(src.at[page], d.pages.at[slot, i], d.page_sem.at[slot]))
        if not phase_is_v:
            copies.append(pltpu.make_async_copy(d.lfcache.at[page], d.lfpages.at[slot, i],
                                                d.lf_sem.at[slot]))
    return copies


def _decode_dma(d, batch, phase_is_v, unit, slot, start):
    for c in _decode_copies(d, batch, phase_is_v, unit, slot):
        if start:
            c.start()
        else:
            c.wait()


def _decode_k_unit(d, unit, slot):
    PP = PAGES_PER_STEP

    @pl.when(unit == 0)
    def _():
        qc = d.qcol[0]
        for h in range(C_HEADS):
            d.qb[h] = jnp.broadcast_to(qc[:, h:h + 1], (C_HEAD_DIM, PAGE_SIZE))
        d.r_scr[...] = jnp.broadcast_to(d.lfn[0], (C_HEADS, PAGE_SIZE))

    slot_i = lax.broadcasted_iota(jnp.int32, (PAGE_SIZE, 2 * PAGE_SIZE), 0)
    slot_j = lax.broadcasted_iota(jnp.int32, (PAGE_SIZE, 2 * PAGE_SIZE), 1)
    sum_mat = jnp.where((slot_j >= PAGE_SIZE) | (slot_i > slot_j), 1.0, 0.0).astype(BF16)
    lf_all = jnp.concatenate([d.lfpages[slot, i] for i in range(PP)], axis=0)
    hi, lo = _split_bf16(lf_all)
    sums = _dot(hi, sum_mat) + _dot(lo, sum_mat)

    sub = lax.broadcasted_iota(jnp.int32, (SUBLANES, PAGE_SIZE), 0)
    groups = C_HEADS // SUBLANES
    s_parts = [[jnp.zeros((SUBLANES, PAGE_SIZE), F32) for _ in range(groups)] for _ in range(PP)]
    for h in range(C_HEADS):
        qh = d.qb[h]
        for i in range(PP):
            tot = _sublane_total(d.pages[slot, i, h] * qh)
            g = h // SUBLANES
            s_parts[i][g] = jnp.where(sub == h % SUBLANES, tot, s_parts[i][g])
    r = d.r_scr[...]
    for i in range(PP):
        s = jnp.concatenate(s_parts[i], axis=0)
        page = sums[i * C_HEADS:(i + 1) * C_HEADS]
        d.s_scr[unit * PP + i] = s + page[:, 0:PAGE_SIZE] + r
        r = r + page[:, PAGE_SIZE:]
    d.r_scr[...] = r


def _decode_v_unit(d, unit, slot):
    PP = PAGES_PER_STEP

    @pl.when(unit == 0)
    def _():
        eye = (lax.broadcasted_iota(jnp.int32, (C_HEADS, C_HEADS), 0)
               == lax.broadcasted_iota(jnp.int32, (C_HEADS, C_HEADS), 1))
        self_row = jnp.sum(d.qcol[0] * d.kcol[0], axis=0, keepdims=True)
        s_self = jnp.sum(jnp.where(eye, self_row, 0.0), axis=1, keepdims=True)
        s_all = d.s_scr[...]
        m = jnp.max(jnp.max(s_all, axis=0), axis=1, keepdims=True)
        m = jnp.maximum(m, s_self)
        p_all = jnp.exp(s_all - m)
        d.s_scr[...] = p_all
        p_self = jnp.exp(s_self - m)
        d.ps_scr[...] = p_self
        d.l_scr[...] = jnp.sum(jnp.sum(p_all, axis=0), axis=1, keepdims=True) + p_self
        d.acc[...] = jnp.zeros(d.acc.shape, F32)

    p_pages = [d.s_scr[unit * PP + i] for i in range(PP)]
    for h in range(C_HEADS):
        a = d.acc[h]
        for i in range(PP):
            a = a + p_pages[i][h:h + 1, :] * d.pages[slot, i, h]
        d.acc[h] = a

    @pl.when(unit == d.n_units - 1)
    def _():
        ones = jnp.ones((SUBLANES, PAGE_SIZE), BF16)
        rows = lax.broadcasted_iota(jnp.int32, (C_HEADS, C_HEAD_DIM), 0)
        out = jnp.zeros((C_HEADS, C_HEAD_DIM), F32)
        for h in range(C_HEADS):
            hi, lo = _split_bf16(d.acc[h])
            tot = _dot_nt(ones, hi) + _dot_nt(ones, lo)
            out = jnp.where(rows == h, tot[0:1, :], out)
        d.out[0] = (out + d.ps_scr[...] * d.vnew[0]) / d.l_scr[...]


def _decode_unit(d, step, n_steps, unit):
    batch = step // 2
    slot = unit % 2
    for phase_is_v in (False, True):
        @pl.when(step % 2 == int(phase_is_v))
        def _():
            _decode_dma(d, batch, phase_is_v, unit, slot, start=False)
            if phase_is_v:
                _decode_v_unit(d, unit, slot)
            else:
                _decode_k_unit(d, unit, slot)

            @pl.when(unit + 2 < d.n_units)
            def _():
                _decode_dma(d, batch, phase_is_v, unit + 2, slot, start=True)

            @pl.when((unit + 2 >= d.n_units) & (step < n_steps - 1))
            def _():
                _decode_dma(d, (step + 1) // 2, not phase_is_v, unit + 2 - d.n_units, slot, start=True)


def _gate_out_kernel(att_ref, gate_ref, x_ref, w_ref, y_ref):
    mixed = (att_ref[...] * jax.nn.silu(gate_ref[...])).astype(BF16)
    y_ref[...] = _dot(mixed, w_ref[...]) + x_ref[...]


def _gate_out(att, gate, x, wout):
    return pl.pallas_call(
        _gate_out_kernel,
        out_shape=jax.ShapeDtypeStruct(x.shape, F32),
        compiler_params=pltpu.CompilerParams(vmem_limit_bytes=VMEM_LIMIT),
        name="gate_out_sample",
    )(att, gate, x, wout)


def _block_diag(w):
    nb, d, _ = w.shape
    eye = jnp.eye(nb, dtype=w.dtype)
    return (eye[:, None, :, None] * w[:, :, None, :]).reshape(nb * d, nb * d)


def _prepare_weights(norm0_g, w_in0, gmlp_v_g, gmlp_w_s, gmlp_b_s, lru_conv_w, lru_conv_b,
                     lru_w_r, lru_b_r, lru_w_i, lru_b_i, lru_lambda, w_out0, norm1_g, w_in1,
                     fox_b_f, q_norm_g, k_norm_g, w_out1):
    lane = np.arange(C_WIDTH) // C_HEAD_DIM
    seg = (lane[:, None] == np.arange(LANES)[None, :]).astype(np.float32)
    win1_t = w_in1.T.astype(BF16)
    wout1 = w_out1.astype(BF16)
    return {
        "norm0_g": norm0_g.reshape(1, D_MODEL),
        "win0": w_in0.astype(BF16),
        "vg": gmlp_v_g.reshape(1, A_WIDTH),
        "ws": gmlp_w_s,
        "bst": gmlp_b_s.T,
        "cw": lru_conv_w,
        "cb": lru_conv_b.reshape(1, B_WIDTH),
        "wr": _block_diag(lru_w_r).astype(BF16),
        "br": lru_b_r.reshape(1, B_WIDTH),
        "wi": _block_diag(lru_w_i).astype(BF16),
        "bi": lru_b_i.reshape(1, B_WIDTH),
        "lam": lru_lambda.reshape(1, B_WIDTH),
        "wout0": w_out0.astype(BF16),
        "norm1_g": norm1_g.reshape(1, D_MODEL),
        "win1_t": win1_t[0:4 * C_WIDTH],
        "wf_t": win1_t[4 * C_WIDTH:],
        "bf_row": fox_b_f.reshape(1, C_HEADS),
        "bf_col": fox_b_f.reshape(C_HEADS, 1),
        "qg": jnp.tile(q_norm_g, C_HEADS).reshape(1, C_WIDTH),
        "kg": jnp.tile(k_norm_g, C_HEADS).reshape(1, C_WIDTH),
        "qg_col": q_norm_g.reshape(C_HEAD_DIM, 1),
        "kg_col": k_norm_g.reshape(C_HEAD_DIM, 1),
        "seg": jnp.asarray(seg, BF16),
        "exp": jnp.asarray(seg.T, BF16),
        "wout1": wout1,
        "wout1_t": wout1.T,
    }


def kernel(x_prompt, x_sample, state_lru_conv, state_lru_h, cache_k, cache_v, cache_logf, page_table, norm0_g, w_in0, gmlp_v_g, gmlp_w_s, gmlp_b_s, lru_conv_w, lru_conv_b, lru_w_r, lru_b_r, lru_w_i, lru_b_i, lru_lambda, w_out0, norm1_g, w_in1, fox_b_f, q_norm_g, k_norm_g, w_out1):
    Bp, L, _ = x_prompt.shape
    Bs = x_sample.shape[0]
    w = _prepare_weights(norm0_g, w_in0, gmlp_v_g, gmlp_w_s, gmlp_b_s, lru_conv_w, lru_conv_b,
                         lru_w_r, lru_b_r, lru_w_i, lru_b_i, lru_lambda, w_out0, norm1_g, w_in1,
                         fox_b_f, q_norm_g, k_norm_g, w_out1)

    conv0 = jnp.zeros((Bp, CONV_W - 1, B_WIDTH), F32)
    h0 = jnp.zeros((Bp, B_WIDTH), F32)
    yp0, lru_conv_p, lru_h_p = _layer0_prompt(x_prompt, conv0, h0, w)
    ys0, gmlp_v_s, conv_s, lru_h_s = _layer0_sample(
        x_sample.reshape(Bs, D_MODEL), state_lru_conv, state_lru_h, w)

    qaug, kaug, kt_p, vt_p, gt_p, lft_p = _fox_proj_prompt(yp0, w)
    q_s, k_s, v_s, g_s, logf_s = _fox_proj_sample(ys0, w)
    att_t, att_s = _fox_attention(qaug, kaug, vt_p, q_s, k_s, v_s, logf_s,
                                  cache_k, cache_v, cache_logf, page_table)
    yp = _gate_out_t(att_t.reshape(Bp, C_WIDTH, L), gt_p, yp0, w["wout1_t"])
    k_p = jnp.transpose(kt_p, (0, 3, 1, 2))
    v_p = jnp.transpose(vt_p, (0, 3, 1, 2))
    logf_p = jnp.transpose(lft_p, (0, 2, 1))
    ys = _gate_out(att_s, g_s, ys0, w["wout1"])

    return (yp, ys.reshape(Bs, 1, D_MODEL), lru_conv_p, lru_h_p, k_p, v_p, logf_p,
            gmlp_v_s.reshape(Bs, 1, A_WIDTH), conv_s.reshape(Bs, CONV_W - 1, B_WIDTH), lru_h_s,
            k_s.reshape(Bs, 1, C_HEADS, C_HEAD_DIM), v_s.reshape(Bs, 1, C_HEADS, C_HEAD_DIM),
            logf_s.reshape(Bs, 1, C_HEADS))
```

```python
import jax
import jax.numpy as jnp
import numpy as np
from jax import lax
from jax.experimental import pallas as pl
from jax.experimental.pallas import tpu as pltpu

D_MODEL = 1024
A_WIDTH = 512
A_GROUPS = 4
A_GROUP_DIM = 128
CHUNK = 128
B_WIDTH = 512
B_BLOCKS = 8
B_BLOCK_DIM = 64
CONV_W = 4
LRU_C = 8.0
C_HEADS = 16
C_HEAD_DIM = 64
C_WIDTH = 1024
PAGE_SIZE = 128
ATTN_SCALE = C_HEAD_DIM ** -0.5
EPS = 1e-6

LANES = 128
SUBLANES = 8
NEG_BIG = -1e30

F32 = jnp.float32
BF16 = jnp.bfloat16

L0_TILE = 512
PROJ_TILE = 512
ATT_T = 512
OUT_TILE = 512
PAGES_PER_STEP = 16
DECODE_SLOTS = 2
ATT_UNROLL = 4
MAX_SLABS = 8
AUG_ROWS = 128
V_AUG_ROWS = 80
LOG2E = 1.4426950408889634
N_BIAS = 3
VMEM_LIMIT = 56 * 1024 * 1024


def _cparams(n_grid_dims):
    return pltpu.CompilerParams(
        dimension_semantics=("arbitrary",) * n_grid_dims,
        vmem_limit_bytes=VMEM_LIMIT,
    )


def _dot(a, b):
    return jnp.dot(a, b, preferred_element_type=F32)


def _dot_nt(a, b):
    return lax.dot_general(a, b, (((1,), (1,)), ((), ())), preferred_element_type=F32)


def _split_bf16(x):
    hi = x.astype(BF16)
    lo = (x - hi.astype(F32)).astype(BF16)
    return hi, lo


def _rms_rows(x, gain):
    ms = jnp.mean(x * x, axis=-1, keepdims=True)
    return x * lax.rsqrt(ms + EPS) * gain


def _gelu_tanh(x):
    c1 = 0.7978845608028654
    hx = 0.5 * x
    return hx + hx * jnp.tanh(x * (c1 + (c1 * 0.044715) * (x * x)))


def _gmlp_v_rows(pv, vg_ref):
    v = _gelu_tanh(pv)
    parts = []
    for g in range(A_GROUPS):
        sl = slice(g * A_GROUP_DIM, (g + 1) * A_GROUP_DIM)
        parts.append(_rms_rows(v[:, sl], vg_ref[:, sl]))
    return jnp.concatenate(parts, axis=-1)


def _lru_gates(xc, wr_ref, br_ref, wi_ref, bi_ref, lam_ref):
    xcb = xc.astype(BF16)
    r = jax.nn.sigmoid(_dot(xcb, wr_ref[...]) + br_ref[...])
    gi = jax.nn.sigmoid(_dot(xcb, wi_ref[...]) + bi_ref[...])
    log_a = -LRU_C * r * jax.nn.softplus(-lam_ref[...])
    a = jnp.exp(log_a)
    bterm = jnp.sqrt(-jnp.tanh(log_a) * (1.0 + a * a)) * (gi * xc)
    return a, bterm


def _layer0_prompt_kernel(x_ref, g_ref, win_ref, vg_ref, ws_ref, bst_ref, cw_ref, cb_ref,
                          wr_ref, br_ref, wi_ref, bi_ref, lam_ref, wout_ref, conv0_ref, h0_ref,
                          y_ref, convo_ref, ho_ref,
                          xbuf, hcar, a8, b8, s_scr):
    T = L0_TILE
    i = pl.program_id(1)

    @pl.when(i == 0)
    def _():
        xbuf[0:SUBLANES, :] = jnp.zeros((SUBLANES, B_WIDTH), F32)
        xbuf[SUBLANES - (CONV_W - 1):SUBLANES, :] = conv0_ref[0]
        hcar[...] = h0_ref[0]

    x = x_ref[0]
    xn = _rms_rows(x, g_ref[...]).astype(BF16)

    def proj(k):
        return _dot(xn, win_ref[:, k * 512:(k + 1) * 512])

    vn = _gmlp_v_rows(proj(1), vg_ref).astype(BF16)
    tri = (lax.broadcasted_iota(jnp.int32, (CHUNK, CHUNK), 0)
           >= lax.broadcasted_iota(jnp.int32, (CHUNK, CHUNK), 1))
    for g in range(A_GROUPS):
        wg = jnp.where(tri, ws_ref[g], 0.0).astype(BF16)
        bias = bst_ref[:, g:g + 1]
        for c in range(T // CHUNK):
            blk = vn[c * CHUNK:(c + 1) * CHUNK, g * A_GROUP_DIM:(g + 1) * A_GROUP_DIM]
            s_scr[c * CHUNK:(c + 1) * CHUNK, g * A_GROUP_DIM:(g + 1) * A_GROUP_DIM] = (
                _dot(wg, blk) + bias)
    u = _gelu_tanh(proj(0))
    mix_a = (u * s_scr[...] * jax.nn.silu(proj(2))).astype(BF16)

    xb = proj(3)
    xbuf[SUBLANES:SUBLANES + T, :] = xb
    xc = cb_ref[...] + cw_ref[3:4, :] * xb
    for tap in range(CONV_W - 1):
        off = SUBLANES - (CONV_W - 1) + tap
        xc = xc + cw_ref[tap:tap + 1, :] * xbuf[off:off + T, :]
    convo_ref[0] = xbuf[T + SUBLANES - (CONV_W - 1):T + SUBLANES, :]
    xbuf[0:SUBLANES, :] = xbuf[T:T + SUBLANES, :]

    a, bt = _lru_gates(xc, wr_ref, br_ref, wi_ref, bi_ref, lam_ref)

    a = a.reshape(T // SUBLANES, SUBLANES, B_WIDTH)
    bt = bt.reshape(T // SUBLANES, SUBLANES, B_WIDTH)
    row = lax.broadcasted_iota(jnp.int32, (1, SUBLANES, B_WIDTH), 1)
    shift = 1
    while shift < SUBLANES:
        a_sh = pltpu.roll(a, shift, axis=1)
        b_sh = pltpu.roll(bt, shift, axis=1)
        m = row >= shift
        bt = jnp.where(m, a * b_sh + bt, bt)
        a = jnp.where(m, a * a_sh, a)
        shift *= 2
    a8[...] = a.reshape(T, B_WIDTH)
    b8[...] = bt.reshape(T, B_WIDTH)

    def group_step(j, h):
        off = pl.multiple_of(j * SUBLANES, SUBLANES)
        rows = a8[pl.ds(off, SUBLANES), :] * h + b8[pl.ds(off, SUBLANES), :]
        b8[pl.ds(off, SUBLANES), :] = rows
        return rows[SUBLANES - 1:SUBLANES, :]

    h_last = lax.fori_loop(0, T // SUBLANES, group_step, hcar[...], unroll=True)
    hcar[...] = h_last
    ho_ref[0] = h_last

    mix_b = (b8[...] * jax.nn.silu(proj(4))).astype(BF16)

    y = _dot(mix_a, wout_ref[0:A_WIDTH, :]) + _dot(mix_b, wout_ref[A_WIDTH:, :]) + x
    y_ref[0] = y


def _layer0_prompt(x, conv0, h0, w):
    B, L, _ = x.shape
    T = L0_TILE
    assert L % T == 0 and T % CHUNK == 0
    nt = L // T
    full = lambda shape: pl.BlockSpec(shape, lambda b, i: (0,) * len(shape))
    in_specs = [
        pl.BlockSpec((1, T, D_MODEL), lambda b, i: (b, i, 0)),
        full((1, D_MODEL)),
        full(w["win0"].shape),
        full((1, A_WIDTH)),
        full((A_GROUPS, CHUNK, CHUNK)),
        full((CHUNK, A_GROUPS)),
        full((CONV_W, B_WIDTH)),
        full((1, B_WIDTH)),
        full((B_WIDTH, B_WIDTH)), full((1, B_WIDTH)),
        full((B_WIDTH, B_WIDTH)), full((1, B_WIDTH)),
        full((1, B_WIDTH)),
        full((D_MODEL, D_MODEL)),
        pl.BlockSpec((1, CONV_W - 1, B_WIDTH), lambda b, i: (b, 0, 0)),
        pl.BlockSpec((1, 1, B_WIDTH), lambda b, i: (b, 0, 0)),
    ]
    out_specs = [
        pl.BlockSpec((1, T, D_MODEL), lambda b, i: (b, i, 0)),
        pl.BlockSpec((1, CONV_W - 1, B_WIDTH), lambda b, i: (b, 0, 0)),
        pl.BlockSpec((1, 1, B_WIDTH), lambda b, i: (b, 0, 0)),
    ]
    out_shape = [
        jax.ShapeDtypeStruct((B, L, D_MODEL), F32),
        jax.ShapeDtypeStruct((B, CONV_W - 1, B_WIDTH), F32),
        jax.ShapeDtypeStruct((B, 1, B_WIDTH), F32),
    ]
    y, convo, ho = pl.pallas_call(
        _layer0_prompt_kernel,
        grid=(B, nt),
        in_specs=in_specs,
        out_specs=out_specs,
        out_shape=out_shape,
        scratch_shapes=[
            pltpu.VMEM((T + SUBLANES, B_WIDTH), F32),
            pltpu.VMEM((1, B_WIDTH), F32),
            pltpu.VMEM((T, B_WIDTH), F32),
            pltpu.VMEM((T, B_WIDTH), F32),
            pltpu.VMEM((T, A_WIDTH), F32),
        ],
        compiler_params=_cparams(2),
        name="layer0_prompt",
    )(x, w["norm0_g"], w["win0"], w["vg"], w["ws"], w["bst"], w["cw"], w["cb"],
      w["wr"], w["br"], w["wi"], w["bi"], w["lam"], w["wout0"], conv0, h0.reshape(B, 1, B_WIDTH))
    return y, convo, ho.reshape(B, B_WIDTH)


def _layer0_sample_kernel(x_ref, g_ref, win_ref, vg_ref, ws_ref, bst_ref, cw_ref, cb_ref,
                          wr_ref, br_ref, wi_ref, bi_ref, lam_ref, wout_ref, conv_ref, h_ref,
                          y_ref, v_ref, convo_ref, ho_ref):
    x = x_ref[...]
    xn = _rms_rows(x, g_ref[...]).astype(BF16)

    def proj(k):
        return _dot(xn, win_ref[:, k * 512:(k + 1) * 512])

    vn = _gmlp_v_rows(proj(1), vg_ref)
    v_ref[...] = vn
    s_parts = []
    for g in range(A_GROUPS):
        sl = slice(g * A_GROUP_DIM, (g + 1) * A_GROUP_DIM)
        s_parts.append(ws_ref[g, 0:1, 0:1] * vn[:, sl] + bst_ref[0:1, g:g + 1])
    s = jnp.concatenate(s_parts, axis=-1)
    mix_a = (_gelu_tanh(proj(0)) * s * jax.nn.silu(proj(2))).astype(BF16)

    xb = proj(3)
    xc = cb_ref[...] + cw_ref[3:4, :] * xb
    for tap in range(CONV_W - 1):
        xc = xc + cw_ref[tap:tap + 1, :] * conv_ref[:, tap * B_WIDTH:(tap + 1) * B_WIDTH]
    for tap in range(CONV_W - 2):
        convo_ref[:, tap * B_WIDTH:(tap + 1) * B_WIDTH] = (
            conv_ref[:, (tap + 1) * B_WIDTH:(tap + 2) * B_WIDTH])
    convo_ref[:, (CONV_W - 2) * B_WIDTH:] = xb

    a, bt = _lru_gates(xc, wr_ref, br_ref, wi_ref, bi_ref, lam_ref)
    h = a * h_ref[...] + bt
    ho_ref[...] = h
    mix_b = (h * jax.nn.silu(proj(4))).astype(BF16)
    y_ref[...] = _dot(mix_a, wout_ref[0:A_WIDTH, :]) + _dot(mix_b, wout_ref[A_WIDTH:, :]) + x


def _layer0_sample(x, conv, h, w):
    n = x.shape[0]
    out_shape = [
        jax.ShapeDtypeStruct((n, D_MODEL), F32),
        jax.ShapeDtypeStruct((n, A_WIDTH), F32),
        jax.ShapeDtypeStruct((n, (CONV_W - 1) * B_WIDTH), F32),
        jax.ShapeDtypeStruct((n, B_WIDTH), F32),
    ]
    return pl.pallas_call(
        _layer0_sample_kernel,
        out_shape=out_shape,
        compiler_params=pltpu.CompilerParams(vmem_limit_bytes=VMEM_LIMIT),
        name="layer0_sample",
    )(x, w["norm0_g"], w["win0"], w["vg"], w["ws"], w["bst"], w["cw"], w["cb"],
      w["wr"], w["br"], w["wi"], w["bi"], w["lam"], w["wout0"],
      conv.reshape(n, (CONV_W - 1) * B_WIDTH), h)


def _head_norm_cols(t, gain_col):
    ms = jnp.mean(t * t, axis=0, keepdims=True)
    return t * lax.rsqrt(ms + EPS) * gain_col


def _fox_proj_prompt_kernel(x_ref, g_ref, wt_ref, bf_ref, qg_ref, kg_ref,
                            qaug_ref, kaug_ref, kt_ref, vt_ref, gt_ref, lf_ref, carry):
    T = PROJ_TILE
    i = pl.program_id(1)

    @pl.when(i == 0)
    def _():
        carry[...] = jnp.zeros_like(carry)

    xn = _rms_rows(x_ref[0], g_ref[...])
    xnt = xn.T.astype(BF16)

    def proj_t(k):
        return _dot(wt_ref[k * C_WIDTH:(k + 1) * C_WIDTH, :], xnt)

    wf_t = wt_ref[4 * C_WIDTH:4 * C_WIDTH + C_HEADS, :]
    lf = jax.nn.log_sigmoid(_dot(wf_t, xnt) + bf_ref[...])
    lf_ref[0] = lf
    upper = (lax.broadcasted_iota(jnp.int32, (T, T), 0)
             <= lax.broadcasted_iota(jnp.int32, (T, T), 1)).astype(BF16)
    hi, lo = _split_bf16(lf)
    c = _dot(hi, upper) + _dot(lo, upper) + carry[...]
    carry[...] = c[:, T - 1:T]
    pieces, rest = [], c * (-LOG2E)
    for _ in range(N_BIAS):
        p = rest.astype(BF16).astype(F32)
        pieces.append(p)
        rest = rest - p

    sub = lax.broadcasted_iota(jnp.int32, (SUBLANES, T), 0)
    ones_rows = jnp.where(sub < N_BIAS, 1.0, 0.0)
    pad = jnp.zeros((AUG_ROWS - C_HEAD_DIM - SUBLANES, T), F32)
    qt = proj_t(0)
    kt = proj_t(1)
    for h in range(C_HEADS):
        sl = slice(h * C_HEAD_DIM, (h + 1) * C_HEAD_DIM)
        qn = _head_norm_cols(qt[sl], qg_ref[...]) * (ATTN_SCALE * LOG2E)
        kn = _head_norm_cols(kt[sl], kg_ref[...])
        kt_ref[0, h] = kn
        qaug_ref[0, h, 0] = jnp.concatenate([qn, ones_rows, pad], axis=0).astype(BF16)
        bias_rows = jnp.zeros((SUBLANES, T), F32)
        for j in range(N_BIAS):
            bias_rows = jnp.where(sub == j, pieces[j][h:h + 1, :], bias_rows)
        kaug_t = jnp.concatenate([kn, bias_rows, pad], axis=0)
        kaug_ref[0, h] = kaug_t.T.astype(BF16)
    vt_ref[0] = proj_t(2).reshape(C_HEADS, C_HEAD_DIM, T)
    gt_ref[0] = jax.nn.silu(proj_t(3)).astype(BF16)


def _fox_proj_prompt(x, w):
    B, L, _ = x.shape
    T = PROJ_TILE
    assert L % ATT_T == 0 and ATT_T % T == 0
    per_q = ATT_T // T
    full = lambda shape: pl.BlockSpec(shape, lambda b, i: (0,) * len(shape))
    in_specs = [
        pl.BlockSpec((1, T, D_MODEL), lambda b, i: (b, i, 0)),
        full((1, D_MODEL)), full((4 * C_WIDTH + C_HEADS, D_MODEL)),
        full((C_HEADS, 1)), full((C_HEAD_DIM, 1)), full((C_HEAD_DIM, 1)),
    ]
    out_specs = [
        pl.BlockSpec((1, C_HEADS, 1, AUG_ROWS, T), lambda b, i: (b, 0, i // per_q, 0, i % per_q)),
        pl.BlockSpec((1, C_HEADS, T, AUG_ROWS), lambda b, i: (b, 0, i, 0)),
        pl.BlockSpec((1, C_HEADS, C_HEAD_DIM, T), lambda b, i: (b, 0, 0, i)),
        pl.BlockSpec((1, C_HEADS, C_HEAD_DIM, T), lambda b, i: (b, 0, 0, i)),
        pl.BlockSpec((1, C_WIDTH, T), lambda b, i: (b, 0, i)),
        pl.BlockSpec((1, C_HEADS, T), lambda b, i: (b, 0, i)),
    ]
    out_shape = [
        jax.ShapeDtypeStruct((B, C_HEADS, L // ATT_T, AUG_ROWS, ATT_T), BF16),
        jax.ShapeDtypeStruct((B, C_HEADS, L, AUG_ROWS), BF16),
        jax.ShapeDtypeStruct((B, C_HEADS, C_HEAD_DIM, L), F32),
        jax.ShapeDtypeStruct((B, C_HEADS, C_HEAD_DIM, L), F32),
        jax.ShapeDtypeStruct((B, C_WIDTH, L), BF16),
        jax.ShapeDtypeStruct((B, C_HEADS, L), F32),
    ]
    return pl.pallas_call(
        _fox_proj_prompt_kernel,
        grid=(B, L // T),
        in_specs=in_specs,
        out_specs=out_specs,
        out_shape=out_shape,
        scratch_shapes=[pltpu.VMEM((C_HEADS, 1), F32)],
        compiler_params=_cparams(2),
        name="fox_proj_prompt",
    )(x, w["norm1_g"], w["win1_t"], w["bf_col"], w["qg_col"], w["kg_col"])


def _fox_attn_kernel(ki_tab, qi_tab, pt_ref,
                     qaug_ref, kaug_ref, vt_ref, qcol_ref, kcol_ref, vnew_ref, lfn_ref,
                     kcache, vcache, lfcache,
                     o_ref, dec_ref,
                     m_scr, acc_scr, vaug_scr, sa_scr, sb_scr, ma_scr, mb_scr, alpha_a, alpha_b,
                     pages, lfpages, page_sem, lf_sem, qb, s_scr, r_scr, ps_scr, l_scr, dacc):
    nq = qaug_ref.shape[2]
    t = ATT_T
    n_off = nq * (nq - 1) // 2

    d = _DecodeRefs(pt_ref, qcol_ref, kcol_ref, vnew_ref, lfn_ref, kcache, vcache, lfcache, dec_ref,
                    pages, lfpages, page_sem, lf_sem, qb, s_scr, r_scr, ps_scr, l_scr, dacc)
    step = pl.program_id(0) * pl.num_programs(1) + pl.program_id(1)
    n_steps = pl.num_programs(0) * pl.num_programs(1)
    assert d.n_units % DECODE_SLOTS == 0

    @pl.when(step == 0)
    def _():
        for unit in range(DECODE_SLOTS):
            _decode_dma(d, 0, False, unit, unit, start=True)

    m_scr[...] = jnp.full(m_scr.shape, NEG_BIG, F32)
    acc_scr[...] = jnp.zeros(acc_scr.shape, F32)
    extra = lax.broadcasted_iota(jnp.int32, (V_AUG_ROWS - C_HEAD_DIM, t), 0)
    one_row = jnp.where(extra == 0, 1.0, 0.0)
    for kb in range(nq):
        vaug_scr[kb] = jnp.concatenate(
            [vt_ref[0, 0, :, kb * t:(kb + 1) * t], one_row], axis=0).astype(BF16)

    def start(idx, buf, masked):
        s_ref, m_ref, alpha_ref = buf
        ki, qi = ki_tab[idx], qi_tab[idx]
        keys = kaug_ref[0, 0, pl.ds(pl.multiple_of(ki * t, t), t), :]
        s = _dot(keys, qaug_ref[0, 0, qi])
        if masked:
            causal = (lax.broadcasted_iota(jnp.int32, (t, t), 0)
                      <= lax.broadcasted_iota(jnp.int32, (t, t), 1))
            s = jnp.where(causal, s, NEG_BIG)
        s_ref[...] = s
        slab = jnp.max(s.reshape(MAX_SLABS, t // MAX_SLABS, t), axis=0)
        m_old = m_scr[qi]
        m_new = jnp.maximum(m_old, jnp.max(slab, axis=0, keepdims=True))
        alpha_ref[...] = jnp.exp2(m_old - m_new)
        m_ref[...] = m_new
        m_scr[qi] = m_new

    def finish(idx, buf):
        s_ref, m_ref, alpha_ref = buf
        ki, qi = ki_tab[idx], qi_tab[idx]
        p = jnp.exp2(s_ref[...] - m_ref[...]).astype(BF16)
        acc_scr[qi] = alpha_ref[...] * acc_scr[qi] + _dot(vaug_scr[ki], p)

    buf_a = (sa_scr, ma_scr, alpha_a)
    buf_b = (sb_scr, mb_scr, alpha_b)

    bufs = (buf_a, buf_b)

    u = ATT_UNROLL
    assert u % 2 == 0 and n_off % u == 0 and nq % u == 0
    n_points = n_off // u + 1
    assert n_points % d.n_units == 0
    point_stride = n_points // d.n_units

    def decode_point(p):
        if isinstance(p, int):
            if p % point_stride == 0:
                _decode_unit(d, step, n_steps, p // point_stride)
        elif point_stride == 1:
            _decode_unit(d, step, n_steps, p)
        else:
            @pl.when(p % point_stride == 0)
            def _():
                _decode_unit(d, step, n_steps, p // point_stride)

    def run(first, count, masked, host_decode):
        n_iter = count // u - 1
        start(first, buf_a, masked)

        def body(it, carry):
            if host_decode:
                decode_point(it)
            base = first + u * it
            for k in range(u):
                start(base + k + 1, bufs[(k + 1) % 2], masked)
                finish(base + k, bufs[k % 2])
            return carry

        lax.fori_loop(0, n_iter, body, 0)
        if host_decode:
            decode_point(n_iter)
        base = first + u * n_iter
        for k in range(u - 1):
            start(base + k + 1, bufs[(k + 1) % 2], masked)
            finish(base + k, bufs[k % 2])
        finish(base + u - 1, bufs[(u - 1) % 2])

    run(0, n_off, False, True)
    decode_point(n_off // u)
    run(n_off, nq, True, False)

    for qi in range(nq):
        o_ref[0, 0, :, qi * t:(qi + 1) * t] = (
            acc_scr[qi, 0:C_HEAD_DIM, :] / acc_scr[qi, C_HEAD_DIM:C_HEAD_DIM + 1, :]).astype(BF16)


def _fox_attention(qaug, kaug, vt, q_s, k_s, v_s, lf_s, cache_k, cache_v, cache_logf, page_table):
    B, H, nq, _, t = qaug.shape
    L = nq * t
    n, n_pages = page_table.shape
    PP = PAGES_PER_STEP
    assert nq % 2 == 0 and n_pages % PP == 0
    assert B * H == 2 * n, "one decode (batch, phase) per prompt (batch, head) grid step"
    off_diag = [(k, q) for k in range(nq) for q in range(k + 1, nq)]
    blocks = off_diag + [(k, k) for k in range(nq)]
    ki_tab = jnp.asarray(np.array([b[0] for b in blocks], np.int32))
    qi_tab = jnp.asarray(np.array([b[1] for b in blocks], np.int32))
    kt_cache = jnp.transpose(cache_k, (0, 2, 3, 1))
    vt_cache = jnp.transpose(cache_v, (0, 2, 3, 1))
    lft_cache = jnp.transpose(cache_logf, (0, 2, 1))
    to_cols = lambda a: jnp.transpose(a.reshape(n, C_HEADS, C_HEAD_DIM), (0, 2, 1))
    dec_block = lambda shape: pl.BlockSpec(
        (1,) + shape, lambda b, h, kt, qt, pt: ((b * H + h) // 2, 0, 0))
    hbm = pl.BlockSpec(memory_space=pl.ANY)
    grid_spec = pltpu.PrefetchScalarGridSpec(
        num_scalar_prefetch=3,
        grid=(B, H),
        in_specs=[
            pl.BlockSpec((1, 1, nq, AUG_ROWS, t), lambda b, h, kt, qt, pt: (b, h, 0, 0, 0)),
            pl.BlockSpec((1, 1, L, AUG_ROWS), lambda b, h, kt, qt, pt: (b, h, 0, 0)),
            pl.BlockSpec((1, 1, C_HEAD_DIM, L), lambda b, h, kt, qt, pt: (b, h, 0, 0)),
            dec_block((C_HEAD_DIM, C_HEADS)), dec_block((C_HEAD_DIM, C_HEADS)),
            dec_block((C_HEADS, C_HEAD_DIM)), dec_block((C_HEADS, 1)),
            hbm, hbm, hbm,
        ],
        out_specs=[
            pl.BlockSpec((1, 1, C_HEAD_DIM, L), lambda b, h, kt, qt, pt: (b, h, 0, 0)),
            dec_block((C_HEADS, C_HEAD_DIM)),
        ],
        scratch_shapes=[
            pltpu.VMEM((nq, 1, t), F32),
            pltpu.VMEM((nq, V_AUG_ROWS, t), F32),
            pltpu.VMEM((nq, V_AUG_ROWS, t), BF16),
            pltpu.VMEM((t, t), F32),
            pltpu.VMEM((t, t), F32),
            pltpu.VMEM((1, t), F32),
            pltpu.VMEM((1, t), F32),
            pltpu.VMEM((1, t), F32),
            pltpu.VMEM((1, t), F32),
            pltpu.VMEM((DECODE_SLOTS, PP, C_HEADS, C_HEAD_DIM, PAGE_SIZE), F32),
            pltpu.VMEM((DECODE_SLOTS, PP, C_HEADS, PAGE_SIZE), F32),
            pltpu.SemaphoreType.DMA((DECODE_SLOTS,)),
            pltpu.SemaphoreType.DMA((DECODE_SLOTS,)),
            pltpu.VMEM((C_HEADS, C_HEAD_DIM, PAGE_SIZE), F32),
            pltpu.VMEM((n_pages, C_HEADS, PAGE_SIZE), F32),
            pltpu.VMEM((C_HEADS, PAGE_SIZE), F32),
            pltpu.VMEM((C_HEADS, 1), F32),
            pltpu.VMEM((C_HEADS, 1), F32),
            pltpu.VMEM((C_HEADS, C_HEAD_DIM, PAGE_SIZE), F32),
        ],
    )
    att_t, att_s = pl.pallas_call(
        _fox_attn_kernel,
        grid_spec=grid_spec,
        out_shape=[jax.ShapeDtypeStruct((B, H, C_HEAD_DIM, L), BF16),
                   jax.ShapeDtypeStruct((n, C_HEADS, C_HEAD_DIM), F32)],
        compiler_params=_cparams(2),
        name="fox_attention",
    )(ki_tab, qi_tab, page_table, qaug, kaug, vt,
      to_cols(q_s), to_cols(k_s), v_s.reshape(n, C_HEADS, C_HEAD_DIM), lf_s.reshape(n, C_HEADS, 1),
      kt_cache, vt_cache, lft_cache)
    return att_t, att_s.reshape(n, C_WIDTH)


def _gate_out_t_kernel(att_ref, gate_ref, x_ref, wt_ref, y_ref):
    mixed = (att_ref[0].astype(F32) * gate_ref[0].astype(F32)).astype(BF16)
    y_ref[0] = _dot(wt_ref[...], mixed).T + x_ref[0]


def _gate_out_t(att_t, gate_t, x, wout_t):
    B, L, _ = x.shape
    T = OUT_TILE
    assert L % T == 0
    col = pl.BlockSpec((1, C_WIDTH, T), lambda b, i: (b, 0, i))
    row = pl.BlockSpec((1, T, D_MODEL), lambda b, i: (b, i, 0))
    return pl.pallas_call(
        _gate_out_t_kernel,
        grid=(B, L // T),
        in_specs=[col, col, row, pl.BlockSpec((D_MODEL, C_WIDTH), lambda b, i: (0, 0))],
        out_specs=row,
        out_shape=jax.ShapeDtypeStruct((B, L, D_MODEL), F32),
        compiler_params=_cparams(2),
        name="gate_out_prompt",
    )(att_t, gate_t, x, wout_t)


def _fox_proj_sample_kernel(x_ref, g_ref, wt_ref, bf_ref, qg_ref, kg_ref, seg_ref, exp_ref,
                            q_ref, k_ref, v_ref, gate_ref, lf_ref):
    xn = _rms_rows(x_ref[...], g_ref[...]).astype(BF16)

    def head_norm(t, gain):
        ssq = _dot((t * t).astype(BF16), seg_ref[...])
        rs = lax.rsqrt(ssq * (1.0 / C_HEAD_DIM) + EPS)
        hi, lo = _split_bf16(rs)
        rs_full = _dot(hi, exp_ref[...]) + _dot(lo, exp_ref[...])
        return t * rs_full * gain

    def proj(k):
        return _dot_nt(xn, wt_ref[k * C_WIDTH:(k + 1) * C_WIDTH, :])

    q_ref[...] = head_norm(proj(0), qg_ref[...]) * ATTN_SCALE
    k_ref[...] = head_norm(proj(1), kg_ref[...])
    v_ref[...] = proj(2)
    gate_ref[...] = proj(3)
    wf_t = wt_ref[4 * C_WIDTH:4 * C_WIDTH + C_HEADS, :]
    lf_ref[...] = jax.nn.log_sigmoid(_dot_nt(xn, wf_t) + bf_ref[...])


def _fox_proj_sample(x, w):
    n = x.shape[0]
    out_shape = [
        jax.ShapeDtypeStruct((n, C_WIDTH), F32),
        jax.ShapeDtypeStruct((n, C_WIDTH), F32),
        jax.ShapeDtypeStruct((n, C_WIDTH), F32),
        jax.ShapeDtypeStruct((n, C_WIDTH), F32),
        jax.ShapeDtypeStruct((n, C_HEADS), F32),
    ]
    return pl.pallas_call(
        _fox_proj_sample_kernel,
        out_shape=out_shape,
        compiler_params=pltpu.CompilerParams(vmem_limit_bytes=VMEM_LIMIT),
        name="fox_proj_sample",
    )(x, w["norm1_g"], w["win1_t"], w["bf_row"], w["qg"], w["kg"], w["seg"], w["exp"])


def _sublane_total(x):
    acc = x[0:SUBLANES]
    for r in range(1, x.shape[0] // SUBLANES):
        acc = acc + x[r * SUBLANES:(r + 1) * SUBLANES]
    shift = SUBLANES // 2
    while shift >= 1:
        acc = acc + pltpu.roll(acc, shift, axis=0)
        shift //= 2
    return acc


class _DecodeRefs:
    def __init__(self, pt, qcol, kcol, vnew, lfn, kcache, vcache, lfcache, out,
                 pages, lfpages, page_sem, lf_sem, qb, s_scr, r_scr, ps_scr, l_scr, acc):
        self.pt, self.qcol, self.kcol, self.vnew, self.lfn = pt, qcol, kcol, vnew, lfn
        self.kcache, self.vcache, self.lfcache, self.out = kcache, vcache, lfcache, out
        self.pages, self.lfpages, self.page_sem, self.lf_sem = pages, lfpages, page_sem, lf_sem
        self.qb, self.s_scr, self.r_scr, self.ps_scr, self.l_scr, self.acc = (
            qb, s_scr, r_scr, ps_scr, l_scr, acc)
        self.n_pages = pt.shape[1]
        self.n_units = self.n_pages // PAGES_PER_STEP


def _decode_copies(d, batch, phase_is_v, unit, slot):
    src = d.vcache if phase_is_v else d.kcache
    copies = []
    for i in range(PAGES_PER_STEP):
        page = d.pt[batch, d.n_pages - 1 - (unit * PAGES_PER_STEP + i)]
        copies.append(pltpu.make_async_copy(src.at[page], d.pages.at[slot, i], d.page_sem.at[slot]))
        if not phase_is_v:
            copies.append(pltpu.make_async_copy(d.lfcache.at[page], d.lfpages.at[slot, i],
                                                d.lf_sem.at[slot]))
    return copies


def _decode_dma(d, batch, phase_is_v, unit, slot, start):
    for c in _decode_copies(d, batch, phase_is_v, unit, slot):
        if start:
            c.start()
        else:
            c.wait()


def _decode_k_unit(d, unit, slot):
    PP = PAGES_PER_STEP

    @pl.when(unit == 0)
    def _():
        qc = d.qcol[0]
        for h in range(C_HEADS):
            d.qb[h] = jnp.broadcast_to(qc[:, h:h + 1], (C_HEAD_DIM, PAGE_SIZE))
        d.r_scr[...] = jnp.broadcast_to(d.lfn[0], (C_HEADS, PAGE_SIZE))

    slot_i = lax.broadcasted_iota(jnp.int32, (PAGE_SIZE, 2 * PAGE_SIZE), 0)
    slot_j = lax.broadcasted_iota(jnp.int32, (PAGE_SIZE, 2 * PAGE_SIZE), 1)
    sum_mat = jnp.where((slot_j >= PAGE_SIZE) | (slot_i > slot_j), 1.0, 0.0).astype(BF16)
    lf_all = jnp.concatenate([d.lfpages[slot, i] for i in range(PP)], axis=0)
    hi, lo = _split_bf16(lf_all)
    sums = _dot(hi, sum_mat) + _dot(lo, sum_mat)

    sub = lax.broadcasted_iota(jnp.int32, (SUBLANES, PAGE_SIZE), 0)
    groups = C_HEADS // SUBLANES
    s_parts = [[jnp.zeros((SUBLANES, PAGE_SIZE), F32) for _ in range(groups)] for _ in range(PP)]
    for h in range(C_HEADS):
        qh = d.qb[h]
        for i in range(PP):
            tot = _sublane_total(d.pages[slot, i, h] * qh)
            g = h // SUBLANES
            s_parts[i][g] = jnp.where(sub == h % SUBLANES, tot, s_parts[i][g])
    r = d.r_scr[...]
    for i in range(PP):
        s = jnp.concatenate(s_parts[i], axis=0)
        page = sums[i * C_HEADS:(i + 1) * C_HEADS]
        d.s_scr[unit * PP + i] = s + page[:, 0:PAGE_SIZE] + r
        r = r + page[:, PAGE_SIZE:]
    d.r_scr[...] = r


def _decode_v_unit(d, unit, slot):
    PP = PAGES_PER_STEP

    @pl.when(unit == 0)
    def _():
        eye = (lax.broadcasted_iota(jnp.int32, (C_HEADS, C_HEADS), 0)
               == lax.broadcasted_iota(jnp.int32, (C_HEADS, C_HEADS), 1))
        self_row = jnp.sum(d.qcol[0] * d.kcol[0], axis=0, keepdims=True)
        s_self = jnp.sum(jnp.where(eye, self_row, 0.0), axis=1, keepdims=True)
        s_all = d.s_scr[...]
        m = jnp.max(jnp.max(s_all, axis=0), axis=1, keepdims=True)
        m = jnp.maximum(m, s_self)
        p_all = jnp.exp(s_all - m)
        d.s_scr[...] = p_all
        p_self = jnp.exp(s_self - m)
        d.ps_scr[...] = p_self
        d.l_scr[...] = jnp.sum(jnp.sum(p_all, axis=0), axis=1, keepdims=True) + p_self
        d.acc[...] = jnp.zeros(d.acc.shape, F32)

    p_pages = [d.s_scr[unit * PP + i] for i in range(PP)]
    for h in range(C_HEADS):
        a = d.acc[h]
        for i in range(PP):
            a = a + p_pages[i][h:h + 1, :] * d.pages[slot, i, h]
        d.acc[h] = a

    @pl.when(unit == d.n_units - 1)
    def _():
        ones = jnp.ones((SUBLANES, PAGE_SIZE), BF16)
        rows = lax.broadcasted_iota(jnp.int32, (C_HEADS, C_HEAD_DIM), 0)
        out = jnp.zeros((C_HEADS, C_HEAD_DIM), F32)
        for h in range(C_HEADS):
            hi, lo = _split_bf16(d.acc[h])
            tot = _dot_nt(ones, hi) + _dot_nt(ones, lo)
            out = jnp.where(rows == h, tot[0:1, :], out)
        d.out[0] = (out + d.ps_scr[...] * d.vnew[0]) / d.l_scr[...]


def _decode_unit(d, step, n_steps, unit):
    batch = step // 2
    ahead = DECODE_SLOTS
    slot = unit % ahead
    for phase_is_v in (False, True):
        @pl.when(step % 2 == int(phase_is_v))
        def _():
            _decode_dma(d, batch, phase_is_v, unit, slot, start=False)
            if phase_is_v:
                _decode_v_unit(d, unit, slot)
            else:
                _decode_k_unit(d, unit, slot)

            @pl.when(unit + ahead < d.n_units)
            def _():
                _decode_dma(d, batch, phase_is_v, unit + ahead, slot, start=True)

            @pl.when((unit + ahead >= d.n_units) & (step < n_steps - 1))
            def _():
                _decode_dma(d, (step + 1) // 2, not phase_is_v, unit + ahead - d.n_units, slot, start=True)


def _gate_out_kernel(att_ref, gate_ref, x_ref, w_ref, y_ref):
    mixed = (att_ref[...] * jax.nn.silu(gate_ref[...])).astype(BF16)
    y_ref[...] = _dot(mixed, w_ref[...]) + x_ref[...]


def _gate_out(att, gate, x, wout):
    return pl.pallas_call(
        _gate_out_kernel,
        out_shape=jax.ShapeDtypeStruct(x.shape, F32),
        compiler_params=pltpu.CompilerParams(vmem_limit_bytes=VMEM_LIMIT),
        name="gate_out_sample",
    )(att, gate, x, wout)


def _block_diag(w):
    nb, d, _ = w.shape
    eye = jnp.eye(nb, dtype=w.dtype)
    return (eye[:, None, :, None] * w[:, :, None, :]).reshape(nb * d, nb * d)


def _prepare_weights(norm0_g, w_in0, gmlp_v_g, gmlp_w_s, gmlp_b_s, lru_conv_w, lru_conv_b,
                     lru_w_r, lru_b_r, lru_w_i, lru_b_i, lru_lambda, w_out0, norm1_g, w_in1,
                     fox_b_f, q_norm_g, k_norm_g, w_out1):
    lane = np.arange(C_WIDTH) // C_HEAD_DIM
    seg = (lane[:, None] == np.arange(LANES)[None, :]).astype(np.float32)
    win1_t = w_in1.T.astype(BF16)
    wout1 = w_out1.astype(BF16)
    return {
        "norm0_g": norm0_g.reshape(1, D_MODEL),
        "win0": w_in0.astype(BF16),
        "vg": gmlp_v_g.reshape(1, A_WIDTH),
        "ws": gmlp_w_s,
        "bst": gmlp_b_s.T,
        "cw": lru_conv_w,
        "cb": lru_conv_b.reshape(1, B_WIDTH),
        "wr": _block_diag(lru_w_r).astype(BF16),
        "br": lru_b_r.reshape(1, B_WIDTH),
        "wi": _block_diag(lru_w_i).astype(BF16),
        "bi": lru_b_i.reshape(1, B_WIDTH),
        "lam": lru_lambda.reshape(1, B_WIDTH),
        "wout0": w_out0.astype(BF16),
        "norm1_g": norm1_g.reshape(1, D_MODEL),
        "win1_t": win1_t,
        "bf_row": fox_b_f.reshape(1, C_HEADS),
        "bf_col": fox_b_f.reshape(C_HEADS, 1),
        "qg": jnp.tile(q_norm_g, C_HEADS).reshape(1, C_WIDTH),
        "kg": jnp.tile(k_norm_g, C_HEADS).reshape(1, C_WIDTH),
        "qg_col": q_norm_g.reshape(C_HEAD_DIM, 1),
        "kg_col": k_norm_g.reshape(C_HEAD_DIM, 1),
        "seg": jnp.asarray(seg, BF16),
        "exp": jnp.asarray(seg.T, BF16),
        "wout1": wout1,
        "wout1_t": wout1.T,
    }


def kernel(x_prompt, x_sample, state_lru_conv, state_lru_h, cache_k, cache_v, cache_logf, page_table, norm0_g, w_in0, gmlp_v_g, gmlp_w_s, gmlp_b_s, lru_conv_w, lru_conv_b, lru_w_r, lru_b_r, lru_w_i, lru_b_i, lru_lambda, w_out0, norm1_g, w_in1, fox_b_f, q_norm_g, k_norm_g, w_out1):
    Bp, L, _ = x_prompt.shape
    Bs = x_sample.shape[0]
    w = _prepare_weights(norm0_g, w_in0, gmlp_v_g, gmlp_w_s, gmlp_b_s, lru_conv_w, lru_conv_b,
                         lru_w_r, lru_b_r, lru_w_i, lru_b_i, lru_lambda, w_out0, norm1_g, w_in1,
                         fox_b_f, q_norm_g, k_norm_g, w_out1)

    conv0 = jnp.zeros((Bp, CONV_W - 1, B_WIDTH), F32)
    h0 = jnp.zeros((Bp, B_WIDTH), F32)
    yp0, lru_conv_p, lru_h_p = _layer0_prompt(x_prompt, conv0, h0, w)
    ys0, gmlp_v_s, conv_s, lru_h_s = _layer0_sample(
        x_sample.reshape(Bs, D_MODEL), state_lru_conv, state_lru_h, w)

    qaug, kaug, kt_p, vt_p, gt_p, lft_p = _fox_proj_prompt(yp0, w)
    q_s, k_s, v_s, g_s, logf_s = _fox_proj_sample(ys0, w)
    att_t, att_s = _fox_attention(qaug, kaug, vt_p, q_s, k_s, v_s, logf_s,
                                  cache_k, cache_v, cache_logf, page_table)
    yp = _gate_out_t(att_t.reshape(Bp, C_WIDTH, L), gt_p, yp0, w["wout1_t"])
    k_p = jnp.transpose(kt_p, (0, 3, 1, 2))
    v_p = jnp.transpose(vt_p, (0, 3, 1, 2))
    logf_p = jnp.transpose(lft_p, (0, 2, 1))
    ys = _gate_out(att_s, g_s, ys0, w["wout1"])

    return (yp, ys.reshape(Bs, 1, D_MODEL), lru_conv_p, lru_h_p, k_p, v_p, logf_p,
            gmlp_v_s.reshape(Bs, 1, A_WIDTH), conv_s.reshape(Bs, CONV_W - 1, B_WIDTH), lru_h_s,
            k_s.reshape(Bs, 1, C_HEADS, C_HEAD_DIM), v_s.reshape(Bs, 1, C_HEADS, C_HEAD_DIM),
            logf_s.reshape(Bs, 1, C_HEADS))
```

```python
import jax
import jax.numpy as jnp
import numpy as np
from jax import lax
from jax.experimental import pallas as pl
from jax.experimental.pallas import tpu as pltpu

D_MODEL = 1024
A_WIDTH = 512
A_GROUPS = 4
A_GROUP_DIM = 128
CHUNK = 128
B_WIDTH = 512
B_BLOCKS = 8
B_BLOCK_DIM = 64
CONV_W = 4
LRU_C = 8.0
C_HEADS = 16
C_HEAD_DIM = 64
C_WIDTH = 1024
PAGE_SIZE = 128
ATTN_SCALE = C_HEAD_DIM ** -0.5
EPS = 1e-6

LANES = 128
SUBLANES = 8
NEG_BIG = -1e30

F32 = jnp.float32
BF16 = jnp.bfloat16

L0_TILE = 512
PROJ_TILE = 512
ATT_T = 512
OUT_TILE = 512
PAGES_PER_STEP = 16
DECODE_SLOTS = 2
MAX_SLABS = 8
AUG_ROWS = 128
V_AUG_ROWS = 80
LOG2E = 1.4426950408889634
N_BIAS = 3
VMEM_LIMIT = 56 * 1024 * 1024


def _cparams(n_grid_dims):
    return pltpu.CompilerParams(
        dimension_semantics=("arbitrary",) * n_grid_dims,
        vmem_limit_bytes=VMEM_LIMIT,
    )


def _dot(a, b):
    return jnp.dot(a, b, preferred_element_type=F32)


def _dot_nt(a, b):
    return lax.dot_general(a, b, (((1,), (1,)), ((), ())), preferred_element_type=F32)


def _split_bf16(x):
    hi = x.astype(BF16)
    lo = (x - hi.astype(F32)).astype(BF16)
    return hi, lo


def _rms_rows(x, gain):
    ms = jnp.mean(x * x, axis=-1, keepdims=True)
    return x * lax.rsqrt(ms + EPS) * gain


def _gelu_tanh(x):
    c1 = 0.7978845608028654
    hx = 0.5 * x
    return hx + hx * jnp.tanh(x * (c1 + (c1 * 0.044715) * (x * x)))


def _gmlp_v_rows(pv, vg_ref):
    v = _gelu_tanh(pv)
    parts = []
    for g in range(A_GROUPS):
        sl = slice(g * A_GROUP_DIM, (g + 1) * A_GROUP_DIM)
        parts.append(_rms_rows(v[:, sl], vg_ref[:, sl]))
    return jnp.concatenate(parts, axis=-1)


def _lru_gates(xc, wr_ref, br_ref, wi_ref, bi_ref, lam_ref):
    xcb = xc.astype(BF16)
    r = jax.nn.sigmoid(_dot(xcb, wr_ref[...]) + br_ref[...])
    gi = jax.nn.sigmoid(_dot(xcb, wi_ref[...]) + bi_ref[...])
    log_a = -LRU_C * r * jax.nn.softplus(-lam_ref[...])
    a = jnp.exp(log_a)
    bterm = jnp.sqrt(-jnp.tanh(log_a) * (1.0 + a * a)) * (gi * xc)
    return a, bterm


def _layer0_prompt_kernel(x_ref, g_ref, win_ref, vg_ref, ws_ref, bst_ref, cw_ref, cb_ref,
                          wr_ref, br_ref, wi_ref, bi_ref, lam_ref, wout_ref, conv0_ref, h0_ref,
                          y_ref, convo_ref, ho_ref,
                          xbuf, hcar, a8, b8, s_scr):
    T = L0_TILE
    i = pl.program_id(1)

    @pl.when(i == 0)
    def _():
        xbuf[0:SUBLANES, :] = jnp.zeros((SUBLANES, B_WIDTH), F32)
        xbuf[SUBLANES - (CONV_W - 1):SUBLANES, :] = conv0_ref[0]
        hcar[...] = h0_ref[0]

    x = x_ref[0]
    xn = _rms_rows(x, g_ref[...]).astype(BF16)

    def proj(k):
        return _dot(xn, win_ref[:, k * 512:(k + 1) * 512])

    vn = _gmlp_v_rows(proj(1), vg_ref).astype(BF16)
    tri = (lax.broadcasted_iota(jnp.int32, (CHUNK, CHUNK), 0)
           >= lax.broadcasted_iota(jnp.int32, (CHUNK, CHUNK), 1))
    for g in range(A_GROUPS):
        wg = jnp.where(tri, ws_ref[g], 0.0).astype(BF16)
        bias = bst_ref[:, g:g + 1]
        for c in range(T // CHUNK):
            blk = vn[c * CHUNK:(c + 1) * CHUNK, g * A_GROUP_DIM:(g + 1) * A_GROUP_DIM]
            s_scr[c * CHUNK:(c + 1) * CHUNK, g * A_GROUP_DIM:(g + 1) * A_GROUP_DIM] = (
                _dot(wg, blk) + bias)
    u = _gelu_tanh(proj(0))
    mix_a = (u * s_scr[...] * jax.nn.silu(proj(2))).astype(BF16)

    xb = proj(3)
    xbuf[SUBLANES:SUBLANES + T, :] = xb
    xc = cb_ref[...] + cw_ref[3:4, :] * xb
    for tap in range(CONV_W - 1):
        off = SUBLANES - (CONV_W - 1) + tap
        xc = xc + cw_ref[tap:tap + 1, :] * xbuf[off:off + T, :]
    convo_ref[0] = xbuf[T + SUBLANES - (CONV_W - 1):T + SUBLANES, :]
    xbuf[0:SUBLANES, :] = xbuf[T:T + SUBLANES, :]

    a, bt = _lru_gates(xc, wr_ref, br_ref, wi_ref, bi_ref, lam_ref)

    a = a.reshape(T // SUBLANES, SUBLANES, B_WIDTH)
    bt = bt.reshape(T // SUBLANES, SUBLANES, B_WIDTH)
    row = lax.broadcasted_iota(jnp.int32, (1, SUBLANES, B_WIDTH), 1)
    shift = 1
    while shift < SUBLANES:
        a_sh = pltpu.roll(a, shift, axis=1)
        b_sh = pltpu.roll(bt, shift, axis=1)
        m = row >= shift
        bt = jnp.where(m, a * b_sh + bt, bt)
        a = jnp.where(m, a * a_sh, a)
        shift *= 2
    a8[...] = a.reshape(T, B_WIDTH)
    b8[...] = bt.reshape(T, B_WIDTH)

    def group_step(j, h):
        off = pl.multiple_of(j * SUBLANES, SUBLANES)
        rows = a8[pl.ds(off, SUBLANES), :] * h + b8[pl.ds(off, SUBLANES), :]
        b8[pl.ds(off, SUBLANES), :] = rows
        return rows[SUBLANES - 1:SUBLANES, :]

    h_last = lax.fori_loop(0, T // SUBLANES, group_step, hcar[...], unroll=True)
    hcar[...] = h_last
    ho_ref[0] = h_last

    mix_b = (b8[...] * jax.nn.silu(proj(4))).astype(BF16)

    y = _dot(mix_a, wout_ref[0:A_WIDTH, :]) + _dot(mix_b, wout_ref[A_WIDTH:, :]) + x
    y_ref[0] = y


def _layer0_prompt(x, conv0, h0, w):
    B, L, _ = x.shape
    T = L0_TILE
    assert L % T == 0 and T % CHUNK == 0
    nt = L // T
    full = lambda shape: pl.BlockSpec(shape, lambda b, i: (0,) * len(shape))
    in_specs = [
        pl.BlockSpec((1, T, D_MODEL), lambda b, i: (b, i, 0)),
        full((1, D_MODEL)),
        full(w["win0"].shape),
        full((1, A_WIDTH)),
        full((A_GROUPS, CHUNK, CHUNK)),
        full((CHUNK, A_GROUPS)),
        full((CONV_W, B_WIDTH)),
        full((1, B_WIDTH)),
        full((B_WIDTH, B_WIDTH)), full((1, B_WIDTH)),
        full((B_WIDTH, B_WIDTH)), full((1, B_WIDTH)),
        full((1, B_WIDTH)),
        full((D_MODEL, D_MODEL)),
        pl.BlockSpec((1, CONV_W - 1, B_WIDTH), lambda b, i: (b, 0, 0)),
        pl.BlockSpec((1, 1, B_WIDTH), lambda b, i: (b, 0, 0)),
    ]
    out_specs = [
        pl.BlockSpec((1, T, D_MODEL), lambda b, i: (b, i, 0)),
        pl.BlockSpec((1, CONV_W - 1, B_WIDTH), lambda b, i: (b, 0, 0)),
        pl.BlockSpec((1, 1, B_WIDTH), lambda b, i: (b, 0, 0)),
    ]
    out_shape = [
        jax.ShapeDtypeStruct((B, L, D_MODEL), F32),
        jax.ShapeDtypeStruct((B, CONV_W - 1, B_WIDTH), F32),
        jax.ShapeDtypeStruct((B, 1, B_WIDTH), F32),
    ]
    y, convo, ho = pl.pallas_call(
        _layer0_prompt_kernel,
        grid=(B, nt),
        in_specs=in_specs,
        out_specs=out_specs,
        out_shape=out_shape,
        scratch_shapes=[
            pltpu.VMEM((T + SUBLANES, B_WIDTH), F32),
            pltpu.VMEM((1, B_WIDTH), F32),
            pltpu.VMEM((T, B_WIDTH), F32),
            pltpu.VMEM((T, B_WIDTH), F32),
            pltpu.VMEM((T, A_WIDTH), F32),
        ],
        compiler_params=_cparams(2),
        name="layer0_prompt",
    )(x, w["norm0_g"], w["win0"], w["vg"], w["ws"], w["bst"], w["cw"], w["cb"],
      w["wr"], w["br"], w["wi"], w["bi"], w["lam"], w["wout0"], conv0, h0.reshape(B, 1, B_WIDTH))
    return y, convo, ho.reshape(B, B_WIDTH)


def _layer0_sample_kernel(x_ref, g_ref, win_ref, vg_ref, ws_ref, bst_ref, cw_ref, cb_ref,
                          wr_ref, br_ref, wi_ref, bi_ref, lam_ref, wout_ref, conv_ref, h_ref,
                          y_ref, v_ref, convo_ref, ho_ref):
    x = x_ref[...]
    xn = _rms_rows(x, g_ref[...]).astype(BF16)

    def proj(k):
        return _dot(xn, win_ref[:, k * 512:(k + 1) * 512])

    vn = _gmlp_v_rows(proj(1), vg_ref)
    v_ref[...] = vn
    s_parts = []
    for g in range(A_GROUPS):
        sl = slice(g * A_GROUP_DIM, (g + 1) * A_GROUP_DIM)
        s_parts.append(ws_ref[g, 0:1, 0:1] * vn[:, sl] + bst_ref[0:1, g:g + 1])
    s = jnp.concatenate(s_parts, axis=-1)
    mix_a = (_gelu_tanh(proj(0)) * s * jax.nn.silu(proj(2))).astype(BF16)

    xb = proj(3)
    xc = cb_ref[...] + cw_ref[3:4, :] * xb
    for tap in range(CONV_W - 1):
        xc = xc + cw_ref[tap:tap + 1, :] * conv_ref[:, tap * B_WIDTH:(tap + 1) * B_WIDTH]
    for tap in range(CONV_W - 2):
        convo_ref[:, tap * B_WIDTH:(tap + 1) * B_WIDTH] = (
            conv_ref[:, (tap + 1) * B_WIDTH:(tap + 2) * B_WIDTH])
    convo_ref[:, (CONV_W - 2) * B_WIDTH:] = xb

    a, bt = _lru_gates(xc, wr_ref, br_ref, wi_ref, bi_ref, lam_ref)
    h = a * h_ref[...] + bt
    ho_ref[...] = h
    mix_b = (h * jax.nn.silu(proj(4))).astype(BF16)
    y_ref[...] = _dot(mix_a, wout_ref[0:A_WIDTH, :]) + _dot(mix_b, wout_ref[A_WIDTH:, :]) + x


def _layer0_sample(x, conv, h, w):
    n = x.shape[0]
    out_shape = [
        jax.ShapeDtypeStruct((n, D_MODEL), F32),
        jax.ShapeDtypeStruct((n, A_WIDTH), F32),
        jax.ShapeDtypeStruct((n, (CONV_W - 1) * B_WIDTH), F32),
        jax.ShapeDtypeStruct((n, B_WIDTH), F32),
    ]
    return pl.pallas_call(
        _layer0_sample_kernel,
        out_shape=out_shape,
        compiler_params=pltpu.CompilerParams(vmem_limit_bytes=VMEM_LIMIT),
        name="layer0_sample",
    )(x, w["norm0_g"], w["win0"], w["vg"], w["ws"], w["bst"], w["cw"], w["cb"],
      w["wr"], w["br"], w["wi"], w["bi"], w["lam"], w["wout0"],
      conv.reshape(n, (CONV_W - 1) * B_WIDTH), h)


def _head_norm_cols(t, gain_col):
    ms = jnp.mean(t * t, axis=0, keepdims=True)
    return t * lax.rsqrt(ms + EPS) * gain_col


def _fox_proj_prompt_kernel(x_ref, g_ref, wt_ref, bf_ref, qg_ref, kg_ref,
                            qaug_ref, kaug_ref, kt_ref, vt_ref, gt_ref, lf_ref, carry):
    T = PROJ_TILE
    i = pl.program_id(1)

    @pl.when(i == 0)
    def _():
        carry[...] = jnp.zeros_like(carry)

    xn = _rms_rows(x_ref[0], g_ref[...])
    xnt = xn.T.astype(BF16)

    def proj_t(k):
        return _dot(wt_ref[k * C_WIDTH:(k + 1) * C_WIDTH, :], xnt)

    wf_t = wt_ref[4 * C_WIDTH:4 * C_WIDTH + C_HEADS, :]
    lf = jax.nn.log_sigmoid(_dot(wf_t, xnt) + bf_ref[...])
    lf_ref[0] = lf
    upper = (lax.broadcasted_iota(jnp.int32, (T, T), 0)
             <= lax.broadcasted_iota(jnp.int32, (T, T), 1)).astype(BF16)
    hi, lo = _split_bf16(lf)
    c = _dot(hi, upper) + _dot(lo, upper) + carry[...]
    carry[...] = c[:, T - 1:T]
    pieces, rest = [], c * (-LOG2E)
    for _ in range(N_BIAS):
        p = rest.astype(BF16).astype(F32)
        pieces.append(p)
        rest = rest - p

    sub = lax.broadcasted_iota(jnp.int32, (SUBLANES, T), 0)
    ones_rows = jnp.where(sub < N_BIAS, 1.0, 0.0)
    pad = jnp.zeros((AUG_ROWS - C_HEAD_DIM - SUBLANES, T), F32)
    qt = proj_t(0)
    kt = proj_t(1)
    for h in range(C_HEADS):
        sl = slice(h * C_HEAD_DIM, (h + 1) * C_HEAD_DIM)
        qn = _head_norm_cols(qt[sl], qg_ref[...]) * (ATTN_SCALE * LOG2E)
        kn = _head_norm_cols(kt[sl], kg_ref[...])
        kt_ref[0, h] = kn
        qaug_ref[0, h, 0] = jnp.concatenate([qn, ones_rows, pad], axis=0).astype(BF16)
        bias_rows = jnp.zeros((SUBLANES, T), F32)
        for j in range(N_BIAS):
            bias_rows = jnp.where(sub == j, pieces[j][h:h + 1, :], bias_rows)
        kaug_t = jnp.concatenate([kn, bias_rows, pad], axis=0)
        kaug_ref[0, h] = kaug_t.T.astype(BF16)
    vt_ref[0] = proj_t(2).reshape(C_HEADS, C_HEAD_DIM, T)
    gt_ref[0] = jax.nn.silu(proj_t(3)).astype(BF16)


def _fox_proj_prompt(x, w):
    B, L, _ = x.shape
    T = PROJ_TILE
    assert L % ATT_T == 0 and ATT_T % T == 0
    per_q = ATT_T // T
    full = lambda shape: pl.BlockSpec(shape, lambda b, i: (0,) * len(shape))
    in_specs = [
        pl.BlockSpec((1, T, D_MODEL), lambda b, i: (b, i, 0)),
        full((1, D_MODEL)), full((4 * C_WIDTH + C_HEADS, D_MODEL)),
        full((C_HEADS, 1)), full((C_HEAD_DIM, 1)), full((C_HEAD_DIM, 1)),
    ]
    out_specs = [
        pl.BlockSpec((1, C_HEADS, 1, AUG_ROWS, T), lambda b, i: (b, 0, i // per_q, 0, i % per_q)),
        pl.BlockSpec((1, C_HEADS, T, AUG_ROWS), lambda b, i: (b, 0, i, 0)),
        pl.BlockSpec((1, C_HEADS, C_HEAD_DIM, T), lambda b, i: (b, 0, 0, i)),
        pl.BlockSpec((1, C_HEADS, C_HEAD_DIM, T), lambda b, i: (b, 0, 0, i)),
        pl.BlockSpec((1, C_WIDTH, T), lambda b, i: (b, 0, i)),
        pl.BlockSpec((1, C_HEADS, T), lambda b, i: (b, 0, i)),
    ]
    out_shape = [
        jax.ShapeDtypeStruct((B, C_HEADS, L // ATT_T, AUG_ROWS, ATT_T), BF16),
        jax.ShapeDtypeStruct((B, C_HEADS, L, AUG_ROWS), BF16),
        jax.ShapeDtypeStruct((B, C_HEADS, C_HEAD_DIM, L), F32),
        jax.ShapeDtypeStruct((B, C_HEADS, C_HEAD_DIM, L), F32),
        jax.ShapeDtypeStruct((B, C_WIDTH, L), BF16),
        jax.ShapeDtypeStruct((B, C_HEADS, L), F32),
    ]
    return pl.pallas_call(
        _fox_proj_prompt_kernel,
        grid=(B, L // T),
        in_specs=in_specs,
        out_specs=out_specs,
        out_shape=out_shape,
        scratch_shapes=[pltpu.VMEM((C_HEADS, 1), F32)],
        compiler_params=_cparams(2),
        name="fox_proj_prompt",
    )(x, w["norm1_g"], w["win1_t"], w["bf_col"], w["qg_col"], w["kg_col"])


def _fox_attn_kernel(pt_ref,
                     qaug_ref, kaug_ref, vt_ref, qcol_ref, kcol_ref, vnew_ref, lfn_ref,
                     kcache, vcache, lfcache,
                     o_ref, dec_ref,
                     m_scr, acc_scr, vaug_scr, sa_scr, sb_scr, ma_scr, mb_scr, alpha_a, alpha_b,
                     pages, lfpages, page_sem, lf_sem, qb, s_scr, r_scr, ps_scr, l_scr, dacc):
    nq = qaug_ref.shape[2]
    t = ATT_T
    blocks = [(k, q, False) for k in range(nq) for q in range(k + 1, nq)]
    blocks += [(k, k, True) for k in range(nq)]

    d = _DecodeRefs(pt_ref, qcol_ref, kcol_ref, vnew_ref, lfn_ref, kcache, vcache, lfcache, dec_ref,
                    pages, lfpages, page_sem, lf_sem, qb, s_scr, r_scr, ps_scr, l_scr, dacc)
    step = pl.program_id(0) * pl.num_programs(1) + pl.program_id(1)
    n_steps = pl.num_programs(0) * pl.num_programs(1)
    assert d.n_units % DECODE_SLOTS == 0

    @pl.when(step == 0)
    def _():
        for unit in range(DECODE_SLOTS):
            _decode_dma(d, 0, False, unit, unit, start=True)

    m_scr[...] = jnp.full(m_scr.shape, NEG_BIG, F32)
    acc_scr[...] = jnp.zeros(acc_scr.shape, F32)
    extra = lax.broadcasted_iota(jnp.int32, (V_AUG_ROWS - C_HEAD_DIM, t), 0)
    one_row = jnp.where(extra == 0, 1.0, 0.0)
    for kb in range(nq):
        vaug_scr[kb] = jnp.concatenate(
            [vt_ref[0, 0, :, kb * t:(kb + 1) * t], one_row], axis=0).astype(BF16)

    def start(block, buf):
        s_ref, m_ref, alpha_ref = buf
        ki, qi, masked = block
        keys = kaug_ref[0, 0, ki * t:(ki + 1) * t, :]
        s = _dot(keys, qaug_ref[0, 0, qi])
        if masked:
            causal = (lax.broadcasted_iota(jnp.int32, (t, t), 0)
                      <= lax.broadcasted_iota(jnp.int32, (t, t), 1))
            s = jnp.where(causal, s, NEG_BIG)
        s_ref[...] = s
        slab = jnp.max(s.reshape(MAX_SLABS, t // MAX_SLABS, t), axis=0)
        m_old = m_scr[qi]
        m_new = jnp.maximum(m_old, jnp.max(slab, axis=0, keepdims=True))
        alpha_ref[...] = jnp.exp2(m_old - m_new)
        m_ref[...] = m_new
        m_scr[qi] = m_new

    def finish(block, buf):
        s_ref, m_ref, alpha_ref = buf
        ki, qi, _ = block
        p = jnp.exp2(s_ref[...] - m_ref[...]).astype(BF16)
        acc_scr[qi] = alpha_ref[...] * acc_scr[qi] + _dot(vaug_scr[ki], p)

    bufs = ((sa_scr, ma_scr, alpha_a), (sb_scr, mb_scr, alpha_b))

    assert len(blocks) % d.n_units == 0
    blocks_per_unit = len(blocks) // d.n_units
    start(blocks[0], bufs[0])
    for i, block in enumerate(blocks):
        if i % blocks_per_unit == 0:
            _decode_unit(d, step, n_steps, i // blocks_per_unit)
        if i + 1 < len(blocks):
            start(blocks[i + 1], bufs[(i + 1) % 2])
        finish(block, bufs[i % 2])

    for qi in range(nq):
        o_ref[0, 0, :, qi * t:(qi + 1) * t] = (
            acc_scr[qi, 0:C_HEAD_DIM, :] / acc_scr[qi, C_HEAD_DIM:C_HEAD_DIM + 1, :]).astype(BF16)


def _fox_attention(qaug, kaug, vt, q_s, k_s, v_s, lf_s, cache_k, cache_v, cache_logf, page_table):
    B, H, nq, _, t = qaug.shape
    L = nq * t
    n, n_pages = page_table.shape
    PP = PAGES_PER_STEP
    assert nq % 2 == 0 and n_pages % PP == 0
    assert B * H == 2 * n, "one decode (batch, phase) per prompt (batch, head) grid step"
    kt_cache = jnp.transpose(cache_k, (0, 2, 3, 1))
    vt_cache = jnp.transpose(cache_v, (0, 2, 3, 1))
    lft_cache = jnp.transpose(cache_logf, (0, 2, 1))
    to_cols = lambda a: jnp.transpose(a.reshape(n, C_HEADS, C_HEAD_DIM), (0, 2, 1))
    dec_block = lambda shape: pl.BlockSpec(
        (1,) + shape, lambda b, h, pt: ((b * H + h) // 2, 0, 0))
    hbm = pl.BlockSpec(memory_space=pl.ANY)
    grid_spec = pltpu.PrefetchScalarGridSpec(
        num_scalar_prefetch=1,
        grid=(B, H),
        in_specs=[
            pl.BlockSpec((1, 1, nq, AUG_ROWS, t), lambda b, h, pt: (b, h, 0, 0, 0)),
            pl.BlockSpec((1, 1, L, AUG_ROWS), lambda b, h, pt: (b, h, 0, 0)),
            pl.BlockSpec((1, 1, C_HEAD_DIM, L), lambda b, h, pt: (b, h, 0, 0)),
            dec_block((C_HEAD_DIM, C_HEADS)), dec_block((C_HEAD_DIM, C_HEADS)),
            dec_block((C_HEADS, C_HEAD_DIM)), dec_block((C_HEADS, 1)),
            hbm, hbm, hbm,
        ],
        out_specs=[
            pl.BlockSpec((1, 1, C_HEAD_DIM, L), lambda b, h, pt: (b, h, 0, 0)),
            dec_block((C_HEADS, C_HEAD_DIM)),
        ],
        scratch_shapes=[
            pltpu.VMEM((nq, 1, t), F32),
            pltpu.VMEM((nq, V_AUG_ROWS, t), F32),
            pltpu.VMEM((nq, V_AUG_ROWS, t), BF16),
            pltpu.VMEM((t, t), F32),
            pltpu.VMEM((t, t), F32),
            pltpu.VMEM((1, t), F32),
            pltpu.VMEM((1, t), F32),
            pltpu.VMEM((1, t), F32),
            pltpu.VMEM((1, t), F32),
            pltpu.VMEM((DECODE_SLOTS, PP, C_HEADS, C_HEAD_DIM, PAGE_SIZE), F32),
            pltpu.VMEM((DECODE_SLOTS, PP, C_HEADS, PAGE_SIZE), F32),
            pltpu.SemaphoreType.DMA((DECODE_SLOTS,)),
            pltpu.SemaphoreType.DMA((DECODE_SLOTS,)),
            pltpu.VMEM((C_HEADS, C_HEAD_DIM, PAGE_SIZE), F32),
            pltpu.VMEM((n_pages, C_HEADS, PAGE_SIZE), F32),
            pltpu.VMEM((C_HEADS, PAGE_SIZE), F32),
            pltpu.VMEM((C_HEADS, 1), F32),
            pltpu.VMEM((C_HEADS, 1), F32),
            pltpu.VMEM((C_HEADS, C_HEAD_DIM, PAGE_SIZE), F32),
        ],
    )
    att_t, att_s = pl.pallas_call(
        _fox_attn_kernel,
        grid_spec=grid_spec,
        out_shape=[jax.ShapeDtypeStruct((B, H, C_HEAD_DIM, L), BF16),
                   jax.ShapeDtypeStruct((n, C_HEADS, C_HEAD_DIM), F32)],
        compiler_params=_cparams(2),
        name="fox_attention",
    )(page_table, qaug, kaug, vt,
      to_cols(q_s), to_cols(k_s), v_s.reshape(n, C_HEADS, C_HEAD_DIM), lf_s.reshape(n, C_HEADS, 1),
      kt_cache, vt_cache, lft_cache)
    return att_t, att_s.reshape(n, C_WIDTH)


def _gate_out_t_kernel(att_ref, gate_ref, x_ref, wt_ref, y_ref):
    mixed = (att_ref[0].astype(F32) * gate_ref[0].astype(F32)).astype(BF16)
    y_ref[0] = _dot(wt_ref[...], mixed).T + x_ref[0]


def _gate_out_t(att_t, gate_t, x, wout_t):
    B, L, _ = x.shape
    T = OUT_TILE
    assert L % T == 0
    col = pl.BlockSpec((1, C_WIDTH, T), lambda b, i: (b, 0, i))
    row = pl.BlockSpec((1, T, D_MODEL), lambda b, i: (b, i, 0))
    return pl.pallas_call(
        _gate_out_t_kernel,
        grid=(B, L // T),
        in_specs=[col, col, row, pl.BlockSpec((D_MODEL, C_WIDTH), lambda b, i: (0, 0))],
        out_specs=row,
        out_shape=jax.ShapeDtypeStruct((B, L, D_MODEL), F32),
        compiler_params=_cparams(2),
        name="gate_out_prompt",
    )(att_t, gate_t, x, wout_t)


def _fox_proj_sample_kernel(x_ref, g_ref, wt_ref, bf_ref, qg_ref, kg_ref, seg_ref, exp_ref,
                            q_ref, k_ref, v_ref, gate_ref, lf_ref):
    xn = _rms_rows(x_ref[...], g_ref[...]).astype(BF16)

    def head_norm(t, gain):
        ssq = _dot((t * t).astype(BF16), seg_ref[...])
        rs = lax.rsqrt(ssq * (1.0 / C_HEAD_DIM) + EPS)
        hi, lo = _split_bf16(rs)
        rs_full = _dot(hi, exp_ref[...]) + _dot(lo, exp_ref[...])
        return t * rs_full * gain

    def proj(k):
        return _dot_nt(xn, wt_ref[k * C_WIDTH:(k + 1) * C_WIDTH, :])

    q_ref[...] = head_norm(proj(0), qg_ref[...]) * ATTN_SCALE
    k_ref[...] = head_norm(proj(1), kg_ref[...])
    v_ref[...] = proj(2)
    gate_ref[...] = proj(3)
    wf_t = wt_ref[4 * C_WIDTH:4 * C_WIDTH + C_HEADS, :]
    lf_ref[...] = jax.nn.log_sigmoid(_dot_nt(xn, wf_t) + bf_ref[...])


def _fox_proj_sample(x, w):
    n = x.shape[0]
    out_shape = [
        jax.ShapeDtypeStruct((n, C_WIDTH), F32),
        jax.ShapeDtypeStruct((n, C_WIDTH), F32),
        jax.ShapeDtypeStruct((n, C_WIDTH), F32),
        jax.ShapeDtypeStruct((n, C_WIDTH), F32),
        jax.ShapeDtypeStruct((n, C_HEADS), F32),
    ]
    return pl.pallas_call(
        _fox_proj_sample_kernel,
        out_shape=out_shape,
        compiler_params=pltpu.CompilerParams(vmem_limit_bytes=VMEM_LIMIT),
        name="fox_proj_sample",
    )(x, w["norm1_g"], w["win1_t"], w["bf_row"], w["qg"], w["kg"], w["seg"], w["exp"])


def _sublane_total(x):
    acc = x[0:SUBLANES]
    for r in range(1, x.shape[0] // SUBLANES):
        acc = acc + x[r * SUBLANES:(r + 1) * SUBLANES]
    shift = SUBLANES // 2
    while shift >= 1:
        acc = acc + pltpu.roll(acc, shift, axis=0)
        shift //= 2
    return acc


class _DecodeRefs:
    def __init__(self, pt, qcol, kcol, vnew, lfn, kcache, vcache, lfcache, out,
                 pages, lfpages, page_sem, lf_sem, qb, s_scr, r_scr, ps_scr, l_scr, acc):
        self.pt, self.qcol, self.kcol, self.vnew, self.lfn = pt, qcol, kcol, vnew, lfn
        self.kcache, self.vcache, self.lfcache, self.out = kcache, vcache, lfcache, out
        self.pages, self.lfpages, self.page_sem, self.lf_sem = pages, lfpages, page_sem, lf_sem
        self.qb, self.s_scr, self.r_scr, self.ps_scr, self.l_scr, self.acc = (
            qb, s_scr, r_scr, ps_scr, l_scr, acc)
        self.n_pages = pt.shape[1]
        self.n_units = self.n_pages // PAGES_PER_STEP


def _decode_copies(d, batch, phase_is_v, unit, slot):
    src = d.vcache if phase_is_v else d.kcache
    copies = []
    for i in range(PAGES_PER_STEP):
        page = d.pt[batch, d.n_pages - 1 - (unit * PAGES_PER_STEP + i)]
        copies.append(pltpu.make_async_copy(src.at[page], d.pages.at[slot, i], d.page_sem.at[slot]))
        if not phase_is_v:
            copies.append(pltpu.make_async_copy(d.lfcache.at[page], d.lfpages.at[slot, i],
                                                d.lf_sem.at[slot]))
    return copies


def _decode_dma(d, batch, phase_is_v, unit, slot, start):
    for c in _decode_copies(d, batch, phase_is_v, unit, slot):
        if start:
            c.start()
        else:
            c.wait()


def _decode_k_unit(d, unit, slot):
    PP = PAGES_PER_STEP

    @pl.when(unit == 0)
    def _():
        qc = d.qcol[0]
        for h in range(C_HEADS):
            d.qb[h] = jnp.broadcast_to(qc[:, h:h + 1], (C_HEAD_DIM, PAGE_SIZE))
        d.r_scr[...] = jnp.broadcast_to(d.lfn[0], (C_HEADS, PAGE_SIZE))

    slot_i = lax.broadcasted_iota(jnp.int32, (PAGE_SIZE, 2 * PAGE_SIZE), 0)
    slot_j = lax.broadcasted_iota(jnp.int32, (PAGE_SIZE, 2 * PAGE_SIZE), 1)
    sum_mat = jnp.where((slot_j >= PAGE_SIZE) | (slot_i > slot_j), 1.0, 0.0).astype(BF16)
    lf_all = jnp.concatenate([d.lfpages[slot, i] for i in range(PP)], axis=0)
    hi, lo = _split_bf16(lf_all)
    sums = _dot(hi, sum_mat) + _dot(lo, sum_mat)

    sub = lax.broadcasted_iota(jnp.int32, (SUBLANES, PAGE_SIZE), 0)
    groups = C_HEADS // SUBLANES
    s_parts = [[jnp.zeros((SUBLANES, PAGE_SIZE), F32) for _ in range(groups)] for _ in range(PP)]
    for h in range(C_HEADS):
        qh = d.qb[h]
        for i in range(PP):
            tot = _sublane_total(d.pages[slot, i, h] * qh)
            g = h // SUBLANES
            s_parts[i][g] = jnp.where(sub == h % SUBLANES, tot, s_parts[i][g])
    r = d.r_scr[...]
    for i in range(PP):
        s = jnp.concatenate(s_parts[i], axis=0)
        page = sums[i * C_HEADS:(i + 1) * C_HEADS]
        d.s_scr[unit * PP + i] = s + page[:, 0:PAGE_SIZE] + r
        r = r + page[:, PAGE_SIZE:]
    d.r_scr[...] = r


def _decode_v_unit(d, unit, slot):
    PP = PAGES_PER_STEP

    @pl.when(unit == 0)
    def _():
        eye = (lax.broadcasted_iota(jnp.int32, (C_HEADS, C_HEADS), 0)
               == lax.broadcasted_iota(jnp.int32, (C_HEADS, C_HEADS), 1))
        self_row = jnp.sum(d.qcol[0] * d.kcol[0], axis=0, keepdims=True)
        s_self = jnp.sum(jnp.where(eye, self_row, 0.0), axis=1, keepdims=True)
        s_all = d.s_scr[...]
        m = jnp.max(jnp.max(s_all, axis=0), axis=1, keepdims=True)
        m = jnp.maximum(m, s_self)
        p_all = jnp.exp(s_all - m)
        d.s_scr[...] = p_all
        p_self = jnp.exp(s_self - m)
        d.ps_scr[...] = p_self
        d.l_scr[...] = jnp.sum(jnp.sum(p_all, axis=0), axis=1, keepdims=True) + p_self
        d.acc[...] = jnp.zeros(d.acc.shape, F32)

    p_pages = [d.s_scr[unit * PP + i] for i in range(PP)]
    for h in range(C_HEADS):
        a = d.acc[h]
        for i in range(PP):
            a = a + p_pages[i][h:h + 1, :] * d.pages[slot, i, h]
        d.acc[h] = a

    @pl.when(unit == d.n_units - 1)
    def _():
        ones = jnp.ones((SUBLANES, PAGE_SIZE), BF16)
        rows = lax.broadcasted_iota(jnp.int32, (C_HEADS, C_HEAD_DIM), 0)
        out = jnp.zeros((C_HEADS, C_HEAD_DIM), F32)
        for h in range(C_HEADS):
            hi, lo = _split_bf16(d.acc[h])
            tot = _dot_nt(ones, hi) + _dot_nt(ones, lo)
            out = jnp.where(rows == h, tot[0:1, :], out)
        d.out[0] = (out + d.ps_scr[...] * d.vnew[0]) / d.l_scr[...]


def _decode_unit(d, step, n_steps, unit):
    batch = step // 2
    ahead = DECODE_SLOTS
    slot = unit % ahead
    for phase_is_v in (False, True):
        @pl.when(step % 2 == int(phase_is_v))
        def _():
            _decode_dma(d, batch, phase_is_v, unit, slot, start=False)
            if phase_is_v:
                _decode_v_unit(d, unit, slot)
            else:
                _decode_k_unit(d, unit, slot)

            @pl.when(unit + ahead < d.n_units)
            def _():
                _decode_dma(d, batch, phase_is_v, unit + ahead, slot, start=True)

            @pl.when((unit + ahead >= d.n_units) & (step < n_steps - 1))
            def _():
                _decode_dma(d, (step + 1) // 2, not phase_is_v, unit + ahead - d.n_units, slot, start=True)


def _gate_out_kernel(att_ref, gate_ref, x_ref, w_ref, y_ref):
    mixed = (att_ref[...] * jax.nn.silu(gate_ref[...])).astype(BF16)
    y_ref[...] = _dot(mixed, w_ref[...]) + x_ref[...]


def _gate_out(att, gate, x, wout):
    return pl.pallas_call(
        _gate_out_kernel,
        out_shape=jax.ShapeDtypeStruct(x.shape, F32),
        compiler_params=pltpu.CompilerParams(vmem_limit_bytes=VMEM_LIMIT),
        name="gate_out_sample",
    )(att, gate, x, wout)


def _block_diag(w):
    nb, d, _ = w.shape
    eye = jnp.eye(nb, dtype=w.dtype)
    return (eye[:, None, :, None] * w[:, :, None, :]).reshape(nb * d, nb * d)


def _prepare_weights(norm0_g, w_in0, gmlp_v_g, gmlp_w_s, gmlp_b_s, lru_conv_w, lru_conv_b,
                     lru_w_r, lru_b_r, lru_w_i, lru_b_i, lru_lambda, w_out0, norm1_g, w_in1,
                     fox_b_f, q_norm_g, k_norm_g, w_out1):
    lane = np.arange(C_WIDTH) // C_HEAD_DIM
    seg = (lane[:, None] == np.arange(LANES)[None, :]).astype(np.float32)
    win1_t = w_in1.T.astype(BF16)
    wout1 = w_out1.astype(BF16)
    return {
        "norm0_g": norm0_g.reshape(1, D_MODEL),
        "win0": w_in0.astype(BF16),
        "vg": gmlp_v_g.reshape(1, A_WIDTH),
        "ws": gmlp_w_s,
        "bst": gmlp_b_s.T,
        "cw": lru_conv_w,
        "cb": lru_conv_b.reshape(1, B_WIDTH),
        "wr": _block_diag(lru_w_r).astype(BF16),
        "br": lru_b_r.reshape(1, B_WIDTH),
        "wi": _block_diag(lru_w_i).astype(BF16),
        "bi": lru_b_i.reshape(1, B_WIDTH),
        "lam": lru_lambda.reshape(1, B_WIDTH),
        "wout0": w_out0.astype(BF16),
        "norm1_g": norm1_g.reshape(1, D_MODEL),
        "win1_t": win1_t,
        "bf_row": fox_b_f.reshape(1, C_HEADS),
        "bf_col": fox_b_f.reshape(C_HEADS, 1),
        "qg": jnp.tile(q_norm_g, C_HEADS).reshape(1, C_WIDTH),
        "kg": jnp.tile(k_norm_g, C_HEADS).reshape(1, C_WIDTH),
        "qg_col": q_norm_g.reshape(C_HEAD_DIM, 1),
        "kg_col": k_norm_g.reshape(C_HEAD_DIM, 1),
        "seg": jnp.asarray(seg, BF16),
        "exp": jnp.asarray(seg.T, BF16),
        "wout1": wout1,
        "wout1_t": wout1.T,
    }


def kernel(x_prompt, x_sample, state_lru_conv, state_lru_h, cache_k, cache_v, cache_logf, page_table, norm0_g, w_in0, gmlp_v_g, gmlp_w_s, gmlp_b_s, lru_conv_w, lru_conv_b, lru_w_r, lru_b_r, lru_w_i, lru_b_i, lru_lambda, w_out0, norm1_g, w_in1, fox_b_f, q_norm_g, k_norm_g, w_out1):
    Bp, L, _ = x_prompt.shape
    Bs = x_sample.shape[0]
    w = _prepare_weights(norm0_g, w_in0, gmlp_v_g, gmlp_w_s, gmlp_b_s, lru_conv_w, lru_conv_b,
                         lru_w_r, lru_b_r, lru_w_i, lru_b_i, lru_lambda, w_out0, norm1_g, w_in1,
                         fox_b_f, q_norm_g, k_norm_g, w_out1)

    conv0 = jnp.zeros((Bp, CONV_W - 1, B_WIDTH), F32)
    h0 = jnp.zeros((Bp, B_WIDTH), F32)
    yp0, lru_conv_p, lru_h_p = _layer0_prompt(x_prompt, conv0, h0, w)
    ys0, gmlp_v_s, conv_s, lru_h_s = _layer0_sample(
        x_sample.reshape(Bs, D_MODEL), state_lru_conv, state_lru_h, w)

    qaug, kaug, kt_p, vt_p, gt_p, lft_p = _fox_proj_prompt(yp0, w)
    q_s, k_s, v_s, g_s, logf_s = _fox_proj_sample(ys0, w)
    att_t, att_s = _fox_attention(qaug, kaug, vt_p, q_s, k_s, v_s, logf_s,
                                  cache_k, cache_v, cache_logf, page_table)
    yp = _gate_out_t(att_t.reshape(Bp, C_WIDTH, L), gt_p, yp0, w["wout1_t"])
    k_p = jnp.transpose(kt_p, (0, 3, 1, 2))
    v_p = jnp.transpose(vt_p, (0, 3, 1, 2))
    logf_p = jnp.transpose(lft_p, (0, 2, 1))
    ys = _gate_out(att_s, g_s, ys0, w["wout1"])

    return (yp, ys.reshape(Bs, 1, D_MODEL), lru_conv_p, lru_h_p, k_p, v_p, logf_p,
            gmlp_v_s.reshape(Bs, 1, A_WIDTH), conv_s.reshape(Bs, CONV_W - 1, B_WIDTH), lru_h_s,
            k_s.reshape(Bs, 1, C_HEADS, C_HEAD_DIM), v_s.reshape(Bs, 1, C_HEADS, C_HEAD_DIM),
            logf_s.reshape(Bs, 1, C_HEADS))
```

```python
import jax
import jax.numpy as jnp
import numpy as np
from jax import lax
from jax.experimental import pallas as pl
from jax.experimental.pallas import tpu as pltpu

D_MODEL = 1024
A_WIDTH = 512
A_GROUPS = 4
A_GROUP_DIM = 128
CHUNK = 128
B_WIDTH = 512
B_BLOCKS = 8
B_BLOCK_DIM = 64
CONV_W = 4
LRU_C = 8.0
C_HEADS = 16
C_HEAD_DIM = 64
C_WIDTH = 1024
PAGE_SIZE = 128
ATTN_SCALE = C_HEAD_DIM ** -0.5
EPS = 1e-6

LANES = 128
SUBLANES = 8
NEG_BIG = -1e30

F32 = jnp.float32
BF16 = jnp.bfloat16

L0_TILE = 512
PROJ_TILE = 512
ATT_T = 512
OUT_TILE = 512
PAGES_PER_STEP = 16
DECODE_SLOTS = 2
GATE_GROUPS = 2
MAX_SLABS = 8
AUG_ROWS = 128
V_AUG_ROWS = 80
LOG2E = 1.4426950408889634
N_BIAS = 3
VMEM_LIMIT = 56 * 1024 * 1024


def _cparams(n_grid_dims):
    return pltpu.CompilerParams(
        dimension_semantics=("arbitrary",) * n_grid_dims,
        vmem_limit_bytes=VMEM_LIMIT,
    )


def _dot(a, b):
    return jnp.dot(a, b, preferred_element_type=F32)


def _dot_nt(a, b):
    return lax.dot_general(a, b, (((1,), (1,)), ((), ())), preferred_element_type=F32)


def _split_bf16(x):
    hi = x.astype(BF16)
    lo = (x - hi.astype(F32)).astype(BF16)
    return hi, lo


def _rms_rows(x, gain):
    ms = jnp.mean(x * x, axis=-1, keepdims=True)
    return x * lax.rsqrt(ms + EPS) * gain


def _gelu_tanh(x):
    c1 = 0.7978845608028654
    hx = 0.5 * x
    return hx + hx * jnp.tanh(x * (c1 + (c1 * 0.044715) * (x * x)))


def _gmlp_v_rows(pv, vg_ref):
    v = _gelu_tanh(pv)
    parts = []
    for g in range(A_GROUPS):
        sl = slice(g * A_GROUP_DIM, (g + 1) * A_GROUP_DIM)
        parts.append(_rms_rows(v[:, sl], vg_ref[:, sl]))
    return jnp.concatenate(parts, axis=-1)


def _lru_gates(xc, wr_ref, br_ref, wi_ref, bi_ref, lam_ref):
    xcb = xc.astype(BF16)
    def block_gate(w_ref, b_ref):
        gw = B_WIDTH // GATE_GROUPS
        parts = [_dot(xcb[:, g * gw:(g + 1) * gw], w_ref[g]) for g in range(GATE_GROUPS)]
        return jax.nn.sigmoid(jnp.concatenate(parts, axis=1) + b_ref[...])

    r = block_gate(wr_ref, br_ref)
    gi = block_gate(wi_ref, bi_ref)
    log_a = -LRU_C * r * jax.nn.softplus(-lam_ref[...])
    a = jnp.exp(log_a)
    bterm = jnp.sqrt(-jnp.tanh(log_a) * (1.0 + a * a)) * (gi * xc)
    return a, bterm


def _layer0_prompt_kernel(x_ref, g_ref, win_ref, vg_ref, ws_ref, bst_ref, cw_ref, cb_ref,
                          wr_ref, br_ref, wi_ref, bi_ref, lam_ref, wout_ref, conv0_ref, h0_ref,
                          y_ref, convo_ref, ho_ref,
                          xbuf, hcar, a8, b8, s_scr):
    T = L0_TILE
    i = pl.program_id(1)

    @pl.when(i == 0)
    def _():
        xbuf[0:SUBLANES, :] = jnp.zeros((SUBLANES, B_WIDTH), F32)
        xbuf[SUBLANES - (CONV_W - 1):SUBLANES, :] = conv0_ref[0]
        hcar[...] = h0_ref[0]

    x = x_ref[0]
    xn = _rms_rows(x, g_ref[...]).astype(BF16)

    def proj(k):
        return _dot(xn, win_ref[:, k * 512:(k + 1) * 512])

    vn = _gmlp_v_rows(proj(1), vg_ref).astype(BF16)
    tri = (lax.broadcasted_iota(jnp.int32, (CHUNK, CHUNK), 0)
           >= lax.broadcasted_iota(jnp.int32, (CHUNK, CHUNK), 1))
    for g in range(A_GROUPS):
        wg = jnp.where(tri, ws_ref[g], 0.0).astype(BF16)
        bias = bst_ref[:, g:g + 1]
        for c in range(T // CHUNK):
            blk = vn[c * CHUNK:(c + 1) * CHUNK, g * A_GROUP_DIM:(g + 1) * A_GROUP_DIM]
            s_scr[c * CHUNK:(c + 1) * CHUNK, g * A_GROUP_DIM:(g + 1) * A_GROUP_DIM] = (
                _dot(wg, blk) + bias)
    u = _gelu_tanh(proj(0))
    mix_a = (u * s_scr[...] * jax.nn.silu(proj(2))).astype(BF16)

    xb = proj(3)
    xbuf[SUBLANES:SUBLANES + T, :] = xb
    xc = cb_ref[...] + cw_ref[3:4, :] * xb
    for tap in range(CONV_W - 1):
        off = SUBLANES - (CONV_W - 1) + tap
        xc = xc + cw_ref[tap:tap + 1, :] * xbuf[off:off + T, :]
    convo_ref[0] = xbuf[T + SUBLANES - (CONV_W - 1):T + SUBLANES, :]
    xbuf[0:SUBLANES, :] = xbuf[T:T + SUBLANES, :]

    a, bt = _lru_gates(xc, wr_ref, br_ref, wi_ref, bi_ref, lam_ref)

    a = a.reshape(T // SUBLANES, SUBLANES, B_WIDTH)
    bt = bt.reshape(T // SUBLANES, SUBLANES, B_WIDTH)
    row = lax.broadcasted_iota(jnp.int32, (1, SUBLANES, B_WIDTH), 1)
    shift = 1
    while shift < SUBLANES:
        a_sh = pltpu.roll(a, shift, axis=1)
        b_sh = pltpu.roll(bt, shift, axis=1)
        m = row >= shift
        bt = jnp.where(m, a * b_sh + bt, bt)
        a = jnp.where(m, a * a_sh, a)
        shift *= 2
    a8[...] = a.reshape(T, B_WIDTH)
    b8[...] = bt.reshape(T, B_WIDTH)

    def group_step(j, h):
        off = pl.multiple_of(j * SUBLANES, SUBLANES)
        rows = a8[pl.ds(off, SUBLANES), :] * h + b8[pl.ds(off, SUBLANES), :]
        b8[pl.ds(off, SUBLANES), :] = rows
        return rows[SUBLANES - 1:SUBLANES, :]

    h_last = lax.fori_loop(0, T // SUBLANES, group_step, hcar[...], unroll=True)
    hcar[...] = h_last
    ho_ref[0] = h_last

    mix_b = (b8[...] * jax.nn.silu(proj(4))).astype(BF16)

    y = _dot(mix_a, wout_ref[0:A_WIDTH, :]) + _dot(mix_b, wout_ref[A_WIDTH:, :]) + x
    y_ref[0] = y


def _layer0_prompt(x, conv0, h0, w):
    B, L, _ = x.shape
    T = L0_TILE
    assert L % T == 0 and T % CHUNK == 0
    nt = L // T
    full = lambda shape: pl.BlockSpec(shape, lambda b, i: (0,) * len(shape))
    in_specs = [
        pl.BlockSpec((1, T, D_MODEL), lambda b, i: (b, i, 0)),
        full((1, D_MODEL)),
        full(w["win0"].shape),
        full((1, A_WIDTH)),
        full((A_GROUPS, CHUNK, CHUNK)),
        full((CHUNK, A_GROUPS)),
        full((CONV_W, B_WIDTH)),
        full((1, B_WIDTH)),
        full((GATE_GROUPS, B_WIDTH // GATE_GROUPS, B_WIDTH // GATE_GROUPS)), full((1, B_WIDTH)),
        full((GATE_GROUPS, B_WIDTH // GATE_GROUPS, B_WIDTH // GATE_GROUPS)), full((1, B_WIDTH)),
        full((1, B_WIDTH)),
        full((D_MODEL, D_MODEL)),
        pl.BlockSpec((1, CONV_W - 1, B_WIDTH), lambda b, i: (b, 0, 0)),
        pl.BlockSpec((1, 1, B_WIDTH), lambda b, i: (b, 0, 0)),
    ]
    out_specs = [
        pl.BlockSpec((1, T, D_MODEL), lambda b, i: (b, i, 0)),
        pl.BlockSpec((1, CONV_W - 1, B_WIDTH), lambda b, i: (b, 0, 0)),
        pl.BlockSpec((1, 1, B_WIDTH), lambda b, i: (b, 0, 0)),
    ]
    out_shape = [
        jax.ShapeDtypeStruct((B, L, D_MODEL), F32),
        jax.ShapeDtypeStruct((B, CONV_W - 1, B_WIDTH), F32),
        jax.ShapeDtypeStruct((B, 1, B_WIDTH), F32),
    ]
    y, convo, ho = pl.pallas_call(
        _layer0_prompt_kernel,
        grid=(B, nt),
        in_specs=in_specs,
        out_specs=out_specs,
        out_shape=out_shape,
        scratch_shapes=[
            pltpu.VMEM((T + SUBLANES, B_WIDTH), F32),
            pltpu.VMEM((1, B_WIDTH), F32),
            pltpu.VMEM((T, B_WIDTH), F32),
            pltpu.VMEM((T, B_WIDTH), F32),
            pltpu.VMEM((T, A_WIDTH), F32),
        ],
        compiler_params=_cparams(2),
        name="layer0_prompt",
    )(x, w["norm0_g"], w["win0"], w["vg"], w["ws"], w["bst"], w["cw"], w["cb"],
      w["wr"], w["br"], w["wi"], w["bi"], w["lam"], w["wout0"], conv0, h0.reshape(B, 1, B_WIDTH))
    return y, convo, ho.reshape(B, B_WIDTH)


def _layer0_sample_kernel(x_ref, g_ref, win_ref, vg_ref, ws_ref, bst_ref, cw_ref, cb_ref,
                          wr_ref, br_ref, wi_ref, bi_ref, lam_ref, wout_ref, conv_ref, h_ref,
                          y_ref, v_ref, convo_ref, ho_ref):
    x = x_ref[...]
    xn = _rms_rows(x, g_ref[...]).astype(BF16)

    def proj(k):
        return _dot(xn, win_ref[:, k * 512:(k + 1) * 512])

    vn = _gmlp_v_rows(proj(1), vg_ref)
    v_ref[...] = vn
    s_parts = []
    for g in range(A_GROUPS):
        sl = slice(g * A_GROUP_DIM, (g + 1) * A_GROUP_DIM)
        s_parts.append(ws_ref[g, 0:1, 0:1] * vn[:, sl] + bst_ref[0:1, g:g + 1])
    s = jnp.concatenate(s_parts, axis=-1)
    mix_a = (_gelu_tanh(proj(0)) * s * jax.nn.silu(proj(2))).astype(BF16)

    xb = proj(3)
    xc = cb_ref[...] + cw_ref[3:4, :] * xb
    for tap in range(CONV_W - 1):
        xc = xc + cw_ref[tap:tap + 1, :] * conv_ref[:, tap * B_WIDTH:(tap + 1) * B_WIDTH]
    for tap in range(CONV_W - 2):
        convo_ref[:, tap * B_WIDTH:(tap + 1) * B_WIDTH] = (
            conv_ref[:, (tap + 1) * B_WIDTH:(tap + 2) * B_WIDTH])
    convo_ref[:, (CONV_W - 2) * B_WIDTH:] = xb

    a, bt = _lru_gates(xc, wr_ref, br_ref, wi_ref, bi_ref, lam_ref)
    h = a * h_ref[...] + bt
    ho_ref[...] = h
    mix_b = (h * jax.nn.silu(proj(4))).astype(BF16)
    y_ref[...] = _dot(mix_a, wout_ref[0:A_WIDTH, :]) + _dot(mix_b, wout_ref[A_WIDTH:, :]) + x


def _layer0_sample(x, conv, h, w):
    n = x.shape[0]
    out_shape = [
        jax.ShapeDtypeStruct((n, D_MODEL), F32),
        jax.ShapeDtypeStruct((n, A_WIDTH), F32),
        jax.ShapeDtypeStruct((n, (CONV_W - 1) * B_WIDTH), F32),
        jax.ShapeDtypeStruct((n, B_WIDTH), F32),
    ]
    return pl.pallas_call(
        _layer0_sample_kernel,
        out_shape=out_shape,
        compiler_params=pltpu.CompilerParams(vmem_limit_bytes=VMEM_LIMIT),
        name="layer0_sample",
    )(x, w["norm0_g"], w["win0"], w["vg"], w["ws"], w["bst"], w["cw"], w["cb"],
      w["wr"], w["br"], w["wi"], w["bi"], w["lam"], w["wout0"],
      conv.reshape(n, (CONV_W - 1) * B_WIDTH), h)


def _head_norm_cols(t, gain_col):
    ms = jnp.mean(t * t, axis=0, keepdims=True)
    return t * lax.rsqrt(ms + EPS) * gain_col


def _fox_proj_prompt_kernel(x_ref, g_ref, wt_ref, bf_ref, qg_ref, kg_ref,
                            qaug_ref, kaug_ref, kt_ref, vt_ref, gt_ref, lf_ref, carry):
    T = PROJ_TILE
    i = pl.program_id(1)

    @pl.when(i == 0)
    def _():
        carry[...] = jnp.zeros_like(carry)

    xn = _rms_rows(x_ref[0], g_ref[...])
    xnt = xn.T.astype(BF16)

    def proj_t(k):
        return _dot(wt_ref[k * C_WIDTH:(k + 1) * C_WIDTH, :], xnt)

    wf_t = wt_ref[4 * C_WIDTH:4 * C_WIDTH + C_HEADS, :]
    lf = jax.nn.log_sigmoid(_dot(wf_t, xnt) + bf_ref[...])
    lf_ref[0] = lf
    upper = (lax.broadcasted_iota(jnp.int32, (T, T), 0)
             <= lax.broadcasted_iota(jnp.int32, (T, T), 1)).astype(BF16)
    hi, lo = _split_bf16(lf)
    c = _dot(hi, upper) + _dot(lo, upper) + carry[...]
    carry[...] = c[:, T - 1:T]
    pieces, rest = [], c * (-LOG2E)
    for _ in range(N_BIAS):
        p = rest.astype(BF16).astype(F32)
        pieces.append(p)
        rest = rest - p

    sub = lax.broadcasted_iota(jnp.int32, (SUBLANES, T), 0)
    ones_rows = jnp.where(sub < N_BIAS, 1.0, 0.0)
    pad = jnp.zeros((AUG_ROWS - C_HEAD_DIM - SUBLANES, T), F32)
    qt = proj_t(0)
    kt = proj_t(1)
    for h in range(C_HEADS):
        sl = slice(h * C_HEAD_DIM, (h + 1) * C_HEAD_DIM)
        qn = _head_norm_cols(qt[sl], qg_ref[...]) * (ATTN_SCALE * LOG2E)
        kn = _head_norm_cols(kt[sl], kg_ref[...])
        kt_ref[0, h] = kn
        qaug_ref[0, h, 0] = jnp.concatenate([qn, ones_rows, pad], axis=0).astype(BF16)
        bias_rows = jnp.zeros((SUBLANES, T), F32)
        for j in range(N_BIAS):
            bias_rows = jnp.where(sub == j, pieces[j][h:h + 1, :], bias_rows)
        kaug_t = jnp.concatenate([kn, bias_rows, pad], axis=0)
        kaug_ref[0, h] = kaug_t.T.astype(BF16)
    vt_ref[0] = proj_t(2).reshape(C_HEADS, C_HEAD_DIM, T)
    gt_ref[0] = proj_t(3).astype(BF16)


def _fox_proj_prompt(x, w):
    B, L, _ = x.shape
    T = PROJ_TILE
    assert L % ATT_T == 0 and ATT_T % T == 0
    per_q = ATT_T // T
    full = lambda shape: pl.BlockSpec(shape, lambda b, i: (0,) * len(shape))
    in_specs = [
        pl.BlockSpec((1, T, D_MODEL), lambda b, i: (b, i, 0)),
        full((1, D_MODEL)), full((4 * C_WIDTH + C_HEADS, D_MODEL)),
        full((C_HEADS, 1)), full((C_HEAD_DIM, 1)), full((C_HEAD_DIM, 1)),
    ]
    out_specs = [
        pl.BlockSpec((1, C_HEADS, 1, AUG_ROWS, T), lambda b, i: (b, 0, i // per_q, 0, i % per_q)),
        pl.BlockSpec((1, C_HEADS, T, AUG_ROWS), lambda b, i: (b, 0, i, 0)),
        pl.BlockSpec((1, C_HEADS, C_HEAD_DIM, T), lambda b, i: (b, 0, 0, i)),
        pl.BlockSpec((1, C_HEADS, C_HEAD_DIM, T), lambda b, i: (b, 0, 0, i)),
        pl.BlockSpec((1, C_WIDTH, T), lambda b, i: (b, 0, i)),
        pl.BlockSpec((1, C_HEADS, T), lambda b, i: (b, 0, i)),
    ]
    out_shape = [
        jax.ShapeDtypeStruct((B, C_HEADS, L // ATT_T, AUG_ROWS, ATT_T), BF16),
        jax.ShapeDtypeStruct((B, C_HEADS, L, AUG_ROWS), BF16),
        jax.ShapeDtypeStruct((B, C_HEADS, C_HEAD_DIM, L), F32),
        jax.ShapeDtypeStruct((B, C_HEADS, C_HEAD_DIM, L), F32),
        jax.ShapeDtypeStruct((B, C_WIDTH, L), BF16),
        jax.ShapeDtypeStruct((B, C_HEADS, L), F32),
    ]
    return pl.pallas_call(
        _fox_proj_prompt_kernel,
        grid=(B, L // T),
        in_specs=in_specs,
        out_specs=out_specs,
        out_shape=out_shape,
        scratch_shapes=[pltpu.VMEM((C_HEADS, 1), F32)],
        compiler_params=_cparams(2),
        name="fox_proj_prompt",
    )(x, w["norm1_g"], w["win1_t"], w["bf_col"], w["qg_col"], w["kg_col"])


def _fox_attn_kernel(pt_ref,
                     qaug_ref, kaug_ref, vt_ref, qcol_ref, kcol_ref, vnew_ref, lfn_ref,
                     kcache, vcache, lfcache,
                     o_ref, dec_ref,
                     m_scr, acc_scr, vaug_scr, sa_scr, sb_scr, ma_scr, mb_scr, alpha_a, alpha_b,
                     pages, lfpages, page_sem, lf_sem, qb, s_scr, r_scr, ps_scr, l_scr, dacc):
    nq = qaug_ref.shape[2]
    t = ATT_T
    blocks = [(k, q, False) for k in range(nq) for q in range(k + 1, nq)]
    blocks += [(k, k, True) for k in range(nq)]

    d = _DecodeRefs(pt_ref, qcol_ref, kcol_ref, vnew_ref, lfn_ref, kcache, vcache, lfcache, dec_ref,
                    pages, lfpages, page_sem, lf_sem, qb, s_scr, r_scr, ps_scr, l_scr, dacc)
    step = pl.program_id(0) * pl.num_programs(1) + pl.program_id(1)
    n_steps = pl.num_programs(0) * pl.num_programs(1)
    assert d.n_units % DECODE_SLOTS == 0

    @pl.when(step == 0)
    def _():
        for unit in range(DECODE_SLOTS):
            _decode_dma(d, 0, False, unit, unit, start=True)

    m_scr[...] = jnp.full(m_scr.shape, NEG_BIG, F32)
    acc_scr[...] = jnp.zeros(acc_scr.shape, F32)
    extra = lax.broadcasted_iota(jnp.int32, (V_AUG_ROWS - C_HEAD_DIM, t), 0)
    one_row = jnp.where(extra == 0, 1.0, 0.0)
    for kb in range(nq):
        vaug_scr[kb] = jnp.concatenate(
            [vt_ref[0, 0, :, kb * t:(kb + 1) * t], one_row], axis=0).astype(BF16)

    def pieces(masked):
        return [(t // 2, 0, t // 2), (t, t // 2, t)] if masked else [(t, 0, t)]

    def col_max(s):
        slab = jnp.max(s.reshape(MAX_SLABS, s.shape[0] // MAX_SLABS, s.shape[1]), axis=0)
        return jnp.max(slab, axis=0, keepdims=True)

    def start(block, buf):
        s_ref, m_ref, alpha_ref = buf
        ki, qi, masked = block
        maxes = []
        for nk, q0, q1 in pieces(masked):
            keys = kaug_ref[0, 0, ki * t:ki * t + nk, :]
            s = _dot(keys, qaug_ref[0, 0, qi, :, q0:q1])
            if masked:
                causal = (lax.broadcasted_iota(jnp.int32, s.shape, 0)
                          <= lax.broadcasted_iota(jnp.int32, s.shape, 1) + q0)
                s = jnp.where(causal, s, NEG_BIG)
            s_ref[0:nk, q0:q1] = s
            maxes.append(col_max(s))
        m_old = m_scr[qi]
        m_new = jnp.maximum(m_old, jnp.concatenate(maxes, axis=1))
        alpha_ref[...] = jnp.exp2(m_old - m_new)
        m_ref[...] = m_new
        m_scr[qi] = m_new

    def finish(block, buf):
        s_ref, m_ref, alpha_ref = buf
        ki, qi, masked = block
        outs = []
        for nk, q0, q1 in pieces(masked):
            p = jnp.exp2(s_ref[0:nk, q0:q1] - m_ref[:, q0:q1]).astype(BF16)
            outs.append(_dot(vaug_scr[ki, :, 0:nk], p))
        acc_scr[qi] = alpha_ref[...] * acc_scr[qi] + jnp.concatenate(outs, axis=1)

    bufs = ((sa_scr, ma_scr, alpha_a), (sb_scr, mb_scr, alpha_b))

    assert len(blocks) % d.n_units == 0
    blocks_per_unit = len(blocks) // d.n_units
    start(blocks[0], bufs[0])
    for i, block in enumerate(blocks):
        if i % blocks_per_unit == 0:
            _decode_unit(d, step, n_steps, i // blocks_per_unit)
        if i + 1 < len(blocks):
            start(blocks[i + 1], bufs[(i + 1) % 2])
        finish(block, bufs[i % 2])

    for qi in range(nq):
        o_ref[0, 0, :, qi * t:(qi + 1) * t] = (
            acc_scr[qi, 0:C_HEAD_DIM, :] / acc_scr[qi, C_HEAD_DIM:C_HEAD_DIM + 1, :]).astype(BF16)


def _fox_attention(qaug, kaug, vt, q_s, k_s, v_s, lf_s, cache_k, cache_v, cache_logf, page_table):
    B, H, nq, _, t = qaug.shape
    L = nq * t
    n, n_pages = page_table.shape
    PP = PAGES_PER_STEP
    assert nq % 2 == 0 and n_pages % PP == 0
    assert B * H == 2 * n, "one decode (batch, phase) per prompt (batch, head) grid step"
    kt_cache = jnp.transpose(cache_k, (0, 2, 3, 1))
    vt_cache = jnp.transpose(cache_v, (0, 2, 3, 1))
    lft_cache = jnp.transpose(cache_logf, (0, 2, 1))
    to_cols = lambda a: jnp.transpose(a.reshape(n, C_HEADS, C_HEAD_DIM), (0, 2, 1))
    dec_block = lambda shape: pl.BlockSpec(
        (1,) + shape, lambda b, h, pt: ((b * H + h) // 2, 0, 0))
    hbm = pl.BlockSpec(memory_space=pl.ANY)
    grid_spec = pltpu.PrefetchScalarGridSpec(
        num_scalar_prefetch=1,
        grid=(B, H),
        in_specs=[
            pl.BlockSpec((1, 1, nq, AUG_ROWS, t), lambda b, h, pt: (b, h, 0, 0, 0)),
            pl.BlockSpec((1, 1, L, AUG_ROWS), lambda b, h, pt: (b, h, 0, 0)),
            pl.BlockSpec((1, 1, C_HEAD_DIM, L), lambda b, h, pt: (b, h, 0, 0)),
            dec_block((C_HEAD_DIM, C_HEADS)), dec_block((C_HEAD_DIM, C_HEADS)),
            dec_block((C_HEADS, C_HEAD_DIM)), dec_block((C_HEADS, 1)),
            hbm, hbm, hbm,
        ],
        out_specs=[
            pl.BlockSpec((1, 1, C_HEAD_DIM, L), lambda b, h, pt: (b, h, 0, 0)),
            dec_block((C_HEADS, C_HEAD_DIM)),
        ],
        scratch_shapes=[
            pltpu.VMEM((nq, 1, t), F32),
            pltpu.VMEM((nq, V_AUG_ROWS, t), F32),
            pltpu.VMEM((nq, V_AUG_ROWS, t), BF16),
            pltpu.VMEM((t, t), F32),
            pltpu.VMEM((t, t), F32),
            pltpu.VMEM((1, t), F32),
            pltpu.VMEM((1, t), F32),
            pltpu.VMEM((1, t), F32),
            pltpu.VMEM((1, t), F32),
            pltpu.VMEM((DECODE_SLOTS, PP, C_HEADS, C_HEAD_DIM, PAGE_SIZE), F32),
            pltpu.VMEM((DECODE_SLOTS, PP, C_HEADS, PAGE_SIZE), F32),
            pltpu.SemaphoreType.DMA((DECODE_SLOTS,)),
            pltpu.SemaphoreType.DMA((DECODE_SLOTS,)),
            pltpu.VMEM((C_HEADS, C_HEAD_DIM, PAGE_SIZE), F32),
            pltpu.VMEM((n_pages, C_HEADS, PAGE_SIZE), F32),
            pltpu.VMEM((C_HEADS, PAGE_SIZE), F32),
            pltpu.VMEM((C_HEADS, 1), F32),
            pltpu.VMEM((C_HEADS, 1), F32),
            pltpu.VMEM((C_HEADS, C_HEAD_DIM, PAGE_SIZE), F32),
        ],
    )
    att_t, att_s = pl.pallas_call(
        _fox_attn_kernel,
        grid_spec=grid_spec,
        out_shape=[jax.ShapeDtypeStruct((B, H, C_HEAD_DIM, L), BF16),
                   jax.ShapeDtypeStruct((n, C_HEADS, C_HEAD_DIM), F32)],
        compiler_params=_cparams(2),
        name="fox_attention",
    )(page_table, qaug, kaug, vt,
      to_cols(q_s), to_cols(k_s), v_s.reshape(n, C_HEADS, C_HEAD_DIM), lf_s.reshape(n, C_HEADS, 1),
      kt_cache, vt_cache, lft_cache)
    return att_t, att_s.reshape(n, C_WIDTH)


def _gate_out_t_kernel(att_ref, gate_ref, x_ref, wt_ref, y_ref):
    mixed = (att_ref[0].astype(F32) * jax.nn.silu(gate_ref[0].astype(F32))).astype(BF16)
    y_ref[0] = _dot(wt_ref[...], mixed).T + x_ref[0]


def _gate_out_t(att_t, gate_t, x, wout_t):
    B, L, _ = x.shape
    T = OUT_TILE
    assert L % T == 0
    col = pl.BlockSpec((1, C_WIDTH, T), lambda b, i: (b, 0, i))
    row = pl.BlockSpec((1, T, D_MODEL), lambda b, i: (b, i, 0))
    return pl.pallas_call(
        _gate_out_t_kernel,
        grid=(B, L // T),
        in_specs=[col, col, row, pl.BlockSpec((D_MODEL, C_WIDTH), lambda b, i: (0, 0))],
        out_specs=row,
        out_shape=jax.ShapeDtypeStruct((B, L, D_MODEL), F32),
        compiler_params=_cparams(2),
        name="gate_out_prompt",
    )(att_t, gate_t, x, wout_t)


def _fox_proj_sample_kernel(x_ref, g_ref, wt_ref, bf_ref, qg_ref, kg_ref, seg_ref, exp_ref,
                            q_ref, k_ref, v_ref, gate_ref, lf_ref):
    xn = _rms_rows(x_ref[...], g_ref[...]).astype(BF16)

    def head_norm(t, gain):
        ssq = _dot((t * t).astype(BF16), seg_ref[...])
        rs = lax.rsqrt(ssq * (1.0 / C_HEAD_DIM) + EPS)
        hi, lo = _split_bf16(rs)
        rs_full = _dot(hi, exp_ref[...]) + _dot(lo, exp_ref[...])
        return t * rs_full * gain

    def proj(k):
        return _dot_nt(xn, wt_ref[k * C_WIDTH:(k + 1) * C_WIDTH, :])

    q_ref[...] = head_norm(proj(0), qg_ref[...]) * ATTN_SCALE
    k_ref[...] = head_norm(proj(1), kg_ref[...])
    v_ref[...] = proj(2)
    gate_ref[...] = proj(3)
    wf_t = wt_ref[4 * C_WIDTH:4 * C_WIDTH + C_HEADS, :]
    lf_ref[...] = jax.nn.log_sigmoid(_dot_nt(xn, wf_t) + bf_ref[...])


def _fox_proj_sample(x, w):
    n = x.shape[0]
    out_shape = [
        jax.ShapeDtypeStruct((n, C_WIDTH), F32),
        jax.ShapeDtypeStruct((n, C_WIDTH), F32),
        jax.ShapeDtypeStruct((n, C_WIDTH), F32),
        jax.ShapeDtypeStruct((n, C_WIDTH), F32),
        jax.ShapeDtypeStruct((n, C_HEADS), F32),
    ]
    return pl.pallas_call(
        _fox_proj_sample_kernel,
        out_shape=out_shape,
        compiler_params=pltpu.CompilerParams(vmem_limit_bytes=VMEM_LIMIT),
        name="fox_proj_sample",
    )(x, w["norm1_g"], w["win1_t"], w["bf_row"], w["qg"], w["kg"], w["seg"], w["exp"])


def _sublane_total(x):
    acc = x[0:SUBLANES]
    for r in range(1, x.shape[0] // SUBLANES):
        acc = acc + x[r * SUBLANES:(r + 1) * SUBLANES]
    shift = SUBLANES // 2
    while shift >= 1:
        acc = acc + pltpu.roll(acc, shift, axis=0)
        shift //= 2
    return acc


class _DecodeRefs:
    def __init__(self, pt, qcol, kcol, vnew, lfn, kcache, vcache, lfcache, out,
                 pages, lfpages, page_sem, lf_sem, qb, s_scr, r_scr, ps_scr, l_scr, acc):
        self.pt, self.qcol, self.kcol, self.vnew, self.lfn = pt, qcol, kcol, vnew, lfn
        self.kcache, self.vcache, self.lfcache, self.out = kcache, vcache, lfcache, out
        self.pages, self.lfpages, self.page_sem, self.lf_sem = pages, lfpages, page_sem, lf_sem
        self.qb, self.s_scr, self.r_scr, self.ps_scr, self.l_scr, self.acc = (
            qb, s_scr, r_scr, ps_scr, l_scr, acc)
        self.n_pages = pt.shape[1]
        self.n_units = self.n_pages // PAGES_PER_STEP


def _decode_copies(d, batch, phase_is_v, unit, slot):
    src = d.vcache if phase_is_v else d.kcache
    copies = []
    for i in range(PAGES_PER_STEP):
        page = d.pt[batch, d.n_pages - 1 - (unit * PAGES_PER_STEP + i)]
        copies.append(pltpu.make_async_copy(src.at[page], d.pages.at[slot, i], d.page_sem.at[slot]))
        if not phase_is_v:
            copies.append(pltpu.make_async_copy(d.lfcache.at[page], d.lfpages.at[slot, i],
                                                d.lf_sem.at[slot]))
    return copies


def _decode_dma(d, batch, phase_is_v, unit, slot, start):
    for c in _decode_copies(d, batch, phase_is_v, unit, slot):
        if start:
            c.start()
        else:
            c.wait()


def _decode_k_unit(d, unit, slot):
    PP = PAGES_PER_STEP

    @pl.when(unit == 0)
    def _():
        qc = d.qcol[0]
        for h in range(C_HEADS):
            d.qb[h] = jnp.broadcast_to(qc[:, h:h + 1], (C_HEAD_DIM, PAGE_SIZE))
        d.r_scr[...] = jnp.broadcast_to(d.lfn[0], (C_HEADS, PAGE_SIZE))

    slot_i = lax.broadcasted_iota(jnp.int32, (PAGE_SIZE, 2 * PAGE_SIZE), 0)
    slot_j = lax.broadcasted_iota(jnp.int32, (PAGE_SIZE, 2 * PAGE_SIZE), 1)
    sum_mat = jnp.where((slot_j >= PAGE_SIZE) | (slot_i > slot_j), 1.0, 0.0).astype(BF16)
    lf_all = jnp.concatenate([d.lfpages[slot, i] for i in range(PP)], axis=0)
    hi, lo = _split_bf16(lf_all)
    sums = _dot(hi, sum_mat) + _dot(lo, sum_mat)

    sub = lax.broadcasted_iota(jnp.int32, (SUBLANES, PAGE_SIZE), 0)
    groups = C_HEADS // SUBLANES
    s_parts = [[jnp.zeros((SUBLANES, PAGE_SIZE), F32) for _ in range(groups)] for _ in range(PP)]
    for h in range(C_HEADS):
        qh = d.qb[h]
        for i in range(PP):
            tot = _sublane_total(d.pages[slot, i, h] * qh)
            g = h // SUBLANES
            s_parts[i][g] = jnp.where(sub == h % SUBLANES, tot, s_parts[i][g])
    r = d.r_scr[...]
    for i in range(PP):
        s = jnp.concatenate(s_parts[i], axis=0)
        page = sums[i * C_HEADS:(i + 1) * C_HEADS]
        d.s_scr[unit * PP + i] = s + page[:, 0:PAGE_SIZE] + r
        r = r + page[:, PAGE_SIZE:]
    d.r_scr[...] = r


def _decode_v_unit(d, unit, slot):
    PP = PAGES_PER_STEP

    @pl.when(unit == 0)
    def _():
        eye = (lax.broadcasted_iota(jnp.int32, (C_HEADS, C_HEADS), 0)
               == lax.broadcasted_iota(jnp.int32, (C_HEADS, C_HEADS), 1))
        self_row = jnp.sum(d.qcol[0] * d.kcol[0], axis=0, keepdims=True)
        s_self = jnp.sum(jnp.where(eye, self_row, 0.0), axis=1, keepdims=True)
        s_all = d.s_scr[...]
        m = jnp.max(jnp.max(s_all, axis=0), axis=1, keepdims=True)
        m = jnp.maximum(m, s_self)
        p_all = jnp.exp(s_all - m)
        d.s_scr[...] = p_all
        p_self = jnp.exp(s_self - m)
        d.ps_scr[...] = p_self
        d.l_scr[...] = jnp.sum(jnp.sum(p_all, axis=0), axis=1, keepdims=True) + p_self
        d.acc[...] = jnp.zeros(d.acc.shape, F32)

    p_pages = [d.s_scr[unit * PP + i] for i in range(PP)]
    for h in range(C_HEADS):
        a = d.acc[h]
        for i in range(PP):
            a = a + p_pages[i][h:h + 1, :] * d.pages[slot, i, h]
        d.acc[h] = a

    @pl.when(unit == d.n_units - 1)
    def _():
        ones = jnp.ones((SUBLANES, PAGE_SIZE), BF16)
        rows = lax.broadcasted_iota(jnp.int32, (C_HEADS, C_HEAD_DIM), 0)
        out = jnp.zeros((C_HEADS, C_HEAD_DIM), F32)
        for h in range(C_HEADS):
            hi, lo = _split_bf16(d.acc[h])
            tot = _dot_nt(ones, hi) + _dot_nt(ones, lo)
            out = jnp.where(rows == h, tot[0:1, :], out)
        d.out[0] = (out + d.ps_scr[...] * d.vnew[0]) / d.l_scr[...]


def _decode_unit(d, step, n_steps, unit):
    batch = step // 2
    ahead = DECODE_SLOTS
    slot = unit % ahead
    for phase_is_v in (False, True):
        @pl.when(step % 2 == int(phase_is_v))
        def _():
            _decode_dma(d, batch, phase_is_v, unit, slot, start=False)
            if phase_is_v:
                _decode_v_unit(d, unit, slot)
            else:
                _decode_k_unit(d, unit, slot)

            @pl.when(unit + ahead < d.n_units)
            def _():
                _decode_dma(d, batch, phase_is_v, unit + ahead, slot, start=True)

            @pl.when((unit + ahead >= d.n_units) & (step < n_steps - 1))
            def _():
                _decode_dma(d, (step + 1) // 2, not phase_is_v, unit + ahead - d.n_units, slot, start=True)


def _gate_out_kernel(att_ref, gate_ref, x_ref, w_ref, y_ref):
    mixed = (att_ref[...] * jax.nn.silu(gate_ref[...])).astype(BF16)
    y_ref[...] = _dot(mixed, w_ref[...]) + x_ref[...]


def _gate_out(att, gate, x, wout):
    return pl.pallas_call(
        _gate_out_kernel,
        out_shape=jax.ShapeDtypeStruct(x.shape, F32),
        compiler_params=pltpu.CompilerParams(vmem_limit_bytes=VMEM_LIMIT),
        name="gate_out_sample",
    )(att, gate, x, wout)


def _block_diag(w):
    nb, d, _ = w.shape
    per = nb // GATE_GROUPS
    wg = w.reshape(GATE_GROUPS, per, d, d)
    eye = jnp.eye(per, dtype=w.dtype)
    return (eye[None, :, None, :, None] * wg[:, :, :, None, :]).reshape(GATE_GROUPS, per * d, per * d)


def _prepare_weights(norm0_g, w_in0, gmlp_v_g, gmlp_w_s, gmlp_b_s, lru_conv_w, lru_conv_b,
                     lru_w_r, lru_b_r, lru_w_i, lru_b_i, lru_lambda, w_out0, norm1_g, w_in1,
                     fox_b_f, q_norm_g, k_norm_g, w_out1):
    lane = np.arange(C_WIDTH) // C_HEAD_DIM
    seg = (lane[:, None] == np.arange(LANES)[None, :]).astype(np.float32)
    win1_t = w_in1.T.astype(BF16)
    wout1 = w_out1.astype(BF16)
    return {
        "norm0_g": norm0_g.reshape(1, D_MODEL),
        "win0": w_in0.astype(BF16),
        "vg": gmlp_v_g.reshape(1, A_WIDTH),
        "ws": gmlp_w_s,
        "bst": gmlp_b_s.T,
        "cw": lru_conv_w,
        "cb": lru_conv_b.reshape(1, B_WIDTH),
        "wr": _block_diag(lru_w_r).astype(BF16),
        "br": lru_b_r.reshape(1, B_WIDTH),
        "wi": _block_diag(lru_w_i).astype(BF16),
        "bi": lru_b_i.reshape(1, B_WIDTH),
        "lam": lru_lambda.reshape(1, B_WIDTH),
        "wout0": w_out0.astype(BF16),
        "norm1_g": norm1_g.reshape(1, D_MODEL),
        "win1_t": win1_t,
        "bf_row": fox_b_f.reshape(1, C_HEADS),
        "bf_col": fox_b_f.reshape(C_HEADS, 1),
        "qg": jnp.tile(q_norm_g, C_HEADS).reshape(1, C_WIDTH),
        "kg": jnp.tile(k_norm_g, C_HEADS).reshape(1, C_WIDTH),
        "qg_col": q_norm_g.reshape(C_HEAD_DIM, 1),
        "kg_col": k_norm_g.reshape(C_HEAD_DIM, 1),
        "seg": jnp.asarray(seg, BF16),
        "exp": jnp.asarray(seg.T, BF16),
        "wout1": wout1,
        "wout1_t": wout1.T,
    }


def kernel(x_prompt, x_sample, state_lru_conv, state_lru_h, cache_k, cache_v, cache_logf, page_table, norm0_g, w_in0, gmlp_v_g, gmlp_w_s, gmlp_b_s, lru_conv_w, lru_conv_b, lru_w_r, lru_b_r, lru_w_i, lru_b_i, lru_lambda, w_out0, norm1_g, w_in1, fox_b_f, q_norm_g, k_norm_g, w_out1):
    Bp, L, _ = x_prompt.shape
    Bs = x_sample.shape[0]
    w = _prepare_weights(norm0_g, w_in0, gmlp_v_g, gmlp_w_s, gmlp_b_s, lru_conv_w, lru_conv_b,
                         lru_w_r, lru_b_r, lru_w_i, lru_b_i, lru_lambda, w_out0, norm1_g, w_in1,
                         fox_b_f, q_norm_g, k_norm_g, w_out1)

    conv0 = jnp.zeros((Bp, CONV_W - 1, B_WIDTH), F32)
    h0 = jnp.zeros((Bp, B_WIDTH), F32)
    yp0, lru_conv_p, lru_h_p = _layer0_prompt(x_prompt, conv0, h0, w)
    ys0, gmlp_v_s, conv_s, lru_h_s = _layer0_sample(
        x_sample.reshape(Bs, D_MODEL), state_lru_conv, state_lru_h, w)

    qaug, kaug, kt_p, vt_p, gt_p, lft_p = _fox_proj_prompt(yp0, w)
    q_s, k_s, v_s, g_s, logf_s = _fox_proj_sample(ys0, w)
    att_t, att_s = _fox_attention(qaug, kaug, vt_p, q_s, k_s, v_s, logf_s,
                                  cache_k, cache_v, cache_logf, page_table)
    yp = _gate_out_t(att_t.reshape(Bp, C_WIDTH, L), gt_p, yp0, w["wout1_t"])
    k_p = jnp.transpose(kt_p, (0, 3, 1, 2))
    v_p = jnp.transpose(vt_p, (0, 3, 1, 2))
    logf_p = jnp.transpose(lft_p, (0, 2, 1))
    ys = _gate_out(att_s, g_s, ys0, w["wout1"])

    return (yp, ys.reshape(Bs, 1, D_MODEL), lru_conv_p, lru_h_p, k_p, v_p, logf_p,
            gmlp_v_s.reshape(Bs, 1, A_WIDTH), conv_s.reshape(Bs, CONV_W - 1, B_WIDTH), lru_h_s,
            k_s.reshape(Bs, 1, C_HEADS, C_HEAD_DIM), v_s.reshape(Bs, 1, C_HEADS, C_HEAD_DIM),
            logf_s.reshape(Bs, 1, C_HEADS))
```

```python
import jax
import jax.numpy as jnp
import numpy as np
from jax import lax
from jax.experimental import pallas as pl
from jax.experimental.pallas import tpu as pltpu

D_MODEL = 1024
A_WIDTH = 512
A_GROUPS = 4
A_GROUP_DIM = 128
CHUNK = 128
B_WIDTH = 512
B_BLOCKS = 8
B_BLOCK_DIM = 64
CONV_W = 4
LRU_C = 8.0
C_HEADS = 16
C_HEAD_DIM = 64
C_WIDTH = 1024
PAGE_SIZE = 128
ATTN_SCALE = C_HEAD_DIM ** -0.5
EPS = 1e-6

LANES = 128
SUBLANES = 8
NEG_BIG = -1e30

F32 = jnp.float32
BF16 = jnp.bfloat16

L0_TILE = 512
PROJ_TILE = 512
ATT_T = 512
OUT_TILE = 1024
PAGES_PER_STEP = 16
DECODE_SLOTS = 2
GATE_GROUPS = 2
ATT_BUFFERS = 3
MAX_SLABS = 8
AUG_ROWS = 128
V_AUG_ROWS = 80
LOG2E = 1.4426950408889634
N_BIAS = 3
VMEM_LIMIT = 56 * 1024 * 1024


def _cparams(n_grid_dims):
    return pltpu.CompilerParams(
        dimension_semantics=("arbitrary",) * n_grid_dims,
        vmem_limit_bytes=VMEM_LIMIT,
    )


def _dot(a, b):
    return jnp.dot(a, b, preferred_element_type=F32)


def _dot_nt(a, b):
    return lax.dot_general(a, b, (((1,), (1,)), ((), ())), preferred_element_type=F32)


def _split_bf16(x):
    hi = x.astype(BF16)
    lo = (x - hi.astype(F32)).astype(BF16)
    return hi, lo


def _rms_rows(x, gain):
    ms = jnp.mean(x * x, axis=-1, keepdims=True)
    return x * lax.rsqrt(ms + EPS) * gain


def _gelu_tanh(x):
    c1 = 0.7978845608028654
    hx = 0.5 * x
    return hx + hx * jnp.tanh(x * (c1 + (c1 * 0.044715) * (x * x)))


def _gmlp_v_rows(pv, vg_ref):
    v = _gelu_tanh(pv)
    parts = []
    for g in range(A_GROUPS):
        sl = slice(g * A_GROUP_DIM, (g + 1) * A_GROUP_DIM)
        parts.append(_rms_rows(v[:, sl], vg_ref[:, sl]))
    return jnp.concatenate(parts, axis=-1)


def _lru_gates(xc, wr_ref, br_ref, wi_ref, bi_ref, lam_ref):
    xcb = xc.astype(BF16)
    def block_gate(w_ref, b_ref):
        gw = B_WIDTH // GATE_GROUPS
        parts = [_dot(xcb[:, g * gw:(g + 1) * gw], w_ref[g]) for g in range(GATE_GROUPS)]
        return jax.nn.sigmoid(jnp.concatenate(parts, axis=1) + b_ref[...])

    r = block_gate(wr_ref, br_ref)
    gi = block_gate(wi_ref, bi_ref)
    log_a = -LRU_C * r * jax.nn.softplus(-lam_ref[...])
    a = jnp.exp(log_a)
    bterm = jnp.sqrt(-jnp.tanh(log_a) * (1.0 + a * a)) * (gi * xc)
    return a, bterm


def _layer0_prompt_kernel(x_ref, g_ref, win_ref, vg_ref, ws_ref, bst_ref, cw_ref, cb_ref,
                          wr_ref, br_ref, wi_ref, bi_ref, lam_ref, wout_ref, conv0_ref, h0_ref,
                          y_ref, convo_ref, ho_ref,
                          xbuf, hcar, a8, b8, s_scr):
    T = L0_TILE
    i = pl.program_id(1)

    @pl.when(i == 0)
    def _():
        xbuf[0:SUBLANES, :] = jnp.zeros((SUBLANES, B_WIDTH), F32)
        xbuf[SUBLANES - (CONV_W - 1):SUBLANES, :] = conv0_ref[0]
        hcar[...] = h0_ref[0]

    x = x_ref[0]
    xn = _rms_rows(x, g_ref[...]).astype(BF16)

    def proj(k):
        return _dot(xn, win_ref[:, k * 512:(k + 1) * 512])

    vn = _gmlp_v_rows(proj(1), vg_ref).astype(BF16)
    tri = (lax.broadcasted_iota(jnp.int32, (CHUNK, CHUNK), 0)
           >= lax.broadcasted_iota(jnp.int32, (CHUNK, CHUNK), 1))
    for g in range(A_GROUPS):
        wg = jnp.where(tri, ws_ref[g], 0.0).astype(BF16)
        bias = bst_ref[:, g:g + 1]
        for c in range(T // CHUNK):
            blk = vn[c * CHUNK:(c + 1) * CHUNK, g * A_GROUP_DIM:(g + 1) * A_GROUP_DIM]
            s_scr[c * CHUNK:(c + 1) * CHUNK, g * A_GROUP_DIM:(g + 1) * A_GROUP_DIM] = (
                _dot(wg, blk) + bias)
    u = _gelu_tanh(proj(0))
    mix_a = (u * s_scr[...] * jax.nn.silu(proj(2))).astype(BF16)

    xb = proj(3)
    xbuf[SUBLANES:SUBLANES + T, :] = xb
    xc = cb_ref[...] + cw_ref[3:4, :] * xb
    for tap in range(CONV_W - 1):
        off = SUBLANES - (CONV_W - 1) + tap
        xc = xc + cw_ref[tap:tap + 1, :] * xbuf[off:off + T, :]
    convo_ref[0] = xbuf[T + SUBLANES - (CONV_W - 1):T + SUBLANES, :]
    xbuf[0:SUBLANES, :] = xbuf[T:T + SUBLANES, :]

    a, bt = _lru_gates(xc, wr_ref, br_ref, wi_ref, bi_ref, lam_ref)

    a = a.reshape(T // SUBLANES, SUBLANES, B_WIDTH)
    bt = bt.reshape(T // SUBLANES, SUBLANES, B_WIDTH)
    row = lax.broadcasted_iota(jnp.int32, (1, SUBLANES, B_WIDTH), 1)
    shift = 1
    while shift < SUBLANES:
        a_sh = pltpu.roll(a, shift, axis=1)
        b_sh = pltpu.roll(bt, shift, axis=1)
        m = row >= shift
        bt = jnp.where(m, a * b_sh + bt, bt)
        a = jnp.where(m, a * a_sh, a)
        shift *= 2
    a8[...] = a.reshape(T, B_WIDTH)
    b8[...] = bt.reshape(T, B_WIDTH)

    def group_step(j, h):
        off = pl.multiple_of(j * SUBLANES, SUBLANES)
        rows = a8[pl.ds(off, SUBLANES), :] * h + b8[pl.ds(off, SUBLANES), :]
        b8[pl.ds(off, SUBLANES), :] = rows
        return rows[SUBLANES - 1:SUBLANES, :]

    h_last = lax.fori_loop(0, T // SUBLANES, group_step, hcar[...], unroll=True)
    hcar[...] = h_last
    ho_ref[0] = h_last

    mix_b = (b8[...] * jax.nn.silu(proj(4))).astype(BF16)

    y = _dot(mix_a, wout_ref[0:A_WIDTH, :]) + _dot(mix_b, wout_ref[A_WIDTH:, :]) + x
    y_ref[0] = y


def _layer0_prompt(x, conv0, h0, w):
    B, L, _ = x.shape
    T = L0_TILE
    assert L % T == 0 and T % CHUNK == 0
    nt = L // T
    full = lambda shape: pl.BlockSpec(shape, lambda b, i: (0,) * len(shape))
    in_specs = [
        pl.BlockSpec((1, T, D_MODEL), lambda b, i: (b, i, 0)),
        full((1, D_MODEL)),
        full(w["win0"].shape),
        full((1, A_WIDTH)),
        full((A_GROUPS, CHUNK, CHUNK)),
        full((CHUNK, A_GROUPS)),
        full((CONV_W, B_WIDTH)),
        full((1, B_WIDTH)),
        full((GATE_GROUPS, B_WIDTH // GATE_GROUPS, B_WIDTH // GATE_GROUPS)), full((1, B_WIDTH)),
        full((GATE_GROUPS, B_WIDTH // GATE_GROUPS, B_WIDTH // GATE_GROUPS)), full((1, B_WIDTH)),
        full((1, B_WIDTH)),
        full((D_MODEL, D_MODEL)),
        pl.BlockSpec((1, CONV_W - 1, B_WIDTH), lambda b, i: (b, 0, 0)),
        pl.BlockSpec((1, 1, B_WIDTH), lambda b, i: (b, 0, 0)),
    ]
    out_specs = [
        pl.BlockSpec((1, T, D_MODEL), lambda b, i: (b, i, 0)),
        pl.BlockSpec((1, CONV_W - 1, B_WIDTH), lambda b, i: (b, 0, 0)),
        pl.BlockSpec((1, 1, B_WIDTH), lambda b, i: (b, 0, 0)),
    ]
    out_shape = [
        jax.ShapeDtypeStruct((B, L, D_MODEL), F32),
        jax.ShapeDtypeStruct((B, CONV_W - 1, B_WIDTH), F32),
        jax.ShapeDtypeStruct((B, 1, B_WIDTH), F32),
    ]
    y, convo, ho = pl.pallas_call(
        _layer0_prompt_kernel,
        grid=(B, nt),
        in_specs=in_specs,
        out_specs=out_specs,
        out_shape=out_shape,
        scratch_shapes=[
            pltpu.VMEM((T + SUBLANES, B_WIDTH), F32),
            pltpu.VMEM((1, B_WIDTH), F32),
            pltpu.VMEM((T, B_WIDTH), F32),
            pltpu.VMEM((T, B_WIDTH), F32),
            pltpu.VMEM((T, A_WIDTH), F32),
        ],
        compiler_params=_cparams(2),
        name="layer0_prompt",
    )(x, w["norm0_g"], w["win0"], w["vg"], w["ws"], w["bst"], w["cw"], w["cb"],
      w["wr"], w["br"], w["wi"], w["bi"], w["lam"], w["wout0"], conv0, h0.reshape(B, 1, B_WIDTH))
    return y, convo, ho.reshape(B, B_WIDTH)


def _layer0_sample_kernel(x_ref, g_ref, win_ref, vg_ref, ws_ref, bst_ref, cw_ref, cb_ref,
                          wr_ref, br_ref, wi_ref, bi_ref, lam_ref, wout_ref, conv_ref, h_ref,
                          y_ref, v_ref, convo_ref, ho_ref):
    x = x_ref[...]
    xn = _rms_rows(x, g_ref[...]).astype(BF16)

    def proj(k):
        return _dot(xn, win_ref[:, k * 512:(k + 1) * 512])

    vn = _gmlp_v_rows(proj(1), vg_ref)
    v_ref[...] = vn
    s_parts = []
    for g in range(A_GROUPS):
        sl = slice(g * A_GROUP_DIM, (g + 1) * A_GROUP_DIM)
        s_parts.append(ws_ref[g, 0:1, 0:1] * vn[:, sl] + bst_ref[0:1, g:g + 1])
    s = jnp.concatenate(s_parts, axis=-1)
    mix_a = (_gelu_tanh(proj(0)) * s * jax.nn.silu(proj(2))).astype(BF16)

    xb = proj(3)
    xc = cb_ref[...] + cw_ref[3:4, :] * xb
    for tap in range(CONV_W - 1):
        xc = xc + cw_ref[tap:tap + 1, :] * conv_ref[:, tap * B_WIDTH:(tap + 1) * B_WIDTH]
    for tap in range(CONV_W - 2):
        convo_ref[:, tap * B_WIDTH:(tap + 1) * B_WIDTH] = (
            conv_ref[:, (tap + 1) * B_WIDTH:(tap + 2) * B_WIDTH])
    convo_ref[:, (CONV_W - 2) * B_WIDTH:] = xb

    a, bt = _lru_gates(xc, wr_ref, br_ref, wi_ref, bi_ref, lam_ref)
    h = a * h_ref[...] + bt
    ho_ref[...] = h
    mix_b = (h * jax.nn.silu(proj(4))).astype(BF16)
    y_ref[...] = _dot(mix_a, wout_ref[0:A_WIDTH, :]) + _dot(mix_b, wout_ref[A_WIDTH:, :]) + x


def _layer0_sample(x, conv, h, w):
    n = x.shape[0]
    out_shape = [
        jax.ShapeDtypeStruct((n, D_MODEL), F32),
        jax.ShapeDtypeStruct((n, A_WIDTH), F32),
        jax.ShapeDtypeStruct((n, (CONV_W - 1) * B_WIDTH), F32),
        jax.ShapeDtypeStruct((n, B_WIDTH), F32),
    ]
    return pl.pallas_call(
        _layer0_sample_kernel,
        out_shape=out_shape,
        compiler_params=pltpu.CompilerParams(vmem_limit_bytes=VMEM_LIMIT),
        name="layer0_sample",
    )(x, w["norm0_g"], w["win0"], w["vg"], w["ws"], w["bst"], w["cw"], w["cb"],
      w["wr"], w["br"], w["wi"], w["bi"], w["lam"], w["wout0"],
      conv.reshape(n, (CONV_W - 1) * B_WIDTH), h)


def _head_norm_cols(t, gain_col):
    ms = jnp.mean(t * t, axis=0, keepdims=True)
    return t * lax.rsqrt(ms + EPS) * gain_col


def _fox_proj_prompt_kernel(x_ref, g_ref, wt_ref, bf_ref, qg_ref, kg_ref,
                            qaug_ref, kaug_ref, kt_ref, vt_ref, gt_ref, lf_ref, carry):
    T = PROJ_TILE
    i = pl.program_id(1)

    @pl.when(i == 0)
    def _():
        carry[...] = jnp.zeros_like(carry)

    xn = _rms_rows(x_ref[0], g_ref[...])
    xnt = xn.T.astype(BF16)

    def proj_t(k):
        return _dot(wt_ref[k * C_WIDTH:(k + 1) * C_WIDTH, :], xnt)

    wf_t = wt_ref[4 * C_WIDTH:4 * C_WIDTH + C_HEADS, :]
    lf = jax.nn.log_sigmoid(_dot(wf_t, xnt) + bf_ref[...])
    lf_ref[0] = lf
    upper = (lax.broadcasted_iota(jnp.int32, (T, T), 0)
             <= lax.broadcasted_iota(jnp.int32, (T, T), 1)).astype(BF16)
    hi, lo = _split_bf16(lf)
    c = _dot(hi, upper) + _dot(lo, upper) + carry[...]
    carry[...] = c[:, T - 1:T]
    pieces, rest = [], c * (-LOG2E)
    for _ in range(N_BIAS):
        p = rest.astype(BF16).astype(F32)
        pieces.append(p)
        rest = rest - p

    sub = lax.broadcasted_iota(jnp.int32, (SUBLANES, T), 0)
    ones_rows = jnp.where(sub < N_BIAS, 1.0, 0.0)
    pad = jnp.zeros((AUG_ROWS - C_HEAD_DIM - SUBLANES, T), F32)
    qt = proj_t(0)
    kt = proj_t(1)
    for h in range(C_HEADS):
        sl = slice(h * C_HEAD_DIM, (h + 1) * C_HEAD_DIM)
        qn = _head_norm_cols(qt[sl], qg_ref[...]) * (ATTN_SCALE * LOG2E)
        kn = _head_norm_cols(kt[sl], kg_ref[...])
        kt_ref[0, h] = kn
        qaug_ref[0, h, 0] = jnp.concatenate([qn, ones_rows, pad], axis=0).astype(BF16)
        bias_rows = jnp.zeros((SUBLANES, T), F32)
        for j in range(N_BIAS):
            bias_rows = jnp.where(sub == j, pieces[j][h:h + 1, :], bias_rows)
        kaug_t = jnp.concatenate([kn, bias_rows, pad], axis=0)
        kaug_ref[0, h] = kaug_t.T.astype(BF16)
    vt_ref[0] = proj_t(2).reshape(C_HEADS, C_HEAD_DIM, T)
    gt_ref[0] = proj_t(3).astype(BF16)


def _fox_proj_prompt(x, w):
    B, L, _ = x.shape
    T = PROJ_TILE
    assert L % ATT_T == 0 and ATT_T % T == 0
    per_q = ATT_T // T
    full = lambda shape: pl.BlockSpec(shape, lambda b, i: (0,) * len(shape))
    in_specs = [
        pl.BlockSpec((1, T, D_MODEL), lambda b, i: (b, i, 0)),
        full((1, D_MODEL)), full((4 * C_WIDTH + C_HEADS, D_MODEL)),
        full((C_HEADS, 1)), full((C_HEAD_DIM, 1)), full((C_HEAD_DIM, 1)),
    ]
    out_specs = [
        pl.BlockSpec((1, C_HEADS, 1, AUG_ROWS, T), lambda b, i: (b, 0, i // per_q, 0, i % per_q)),
        pl.BlockSpec((1, C_HEADS, T, AUG_ROWS), lambda b, i: (b, 0, i, 0)),
        pl.BlockSpec((1, C_HEADS, C_HEAD_DIM, T), lambda b, i: (b, 0, 0, i)),
        pl.BlockSpec((1, C_HEADS, C_HEAD_DIM, T), lambda b, i: (b, 0, 0, i)),
        pl.BlockSpec((1, C_WIDTH, T), lambda b, i: (b, 0, i)),
        pl.BlockSpec((1, C_HEADS, T), lambda b, i: (b, 0, i)),
    ]
    out_shape = [
        jax.ShapeDtypeStruct((B, C_HEADS, L // ATT_T, AUG_ROWS, ATT_T), BF16),
        jax.ShapeDtypeStruct((B, C_HEADS, L, AUG_ROWS), BF16),
        jax.ShapeDtypeStruct((B, C_HEADS, C_HEAD_DIM, L), F32),
        jax.ShapeDtypeStruct((B, C_HEADS, C_HEAD_DIM, L), F32),
        jax.ShapeDtypeStruct((B, C_WIDTH, L), BF16),
        jax.ShapeDtypeStruct((B, C_HEADS, L), F32),
    ]
    return pl.pallas_call(
        _fox_proj_prompt_kernel,
        grid=(B, L // T),
        in_specs=in_specs,
        out_specs=out_specs,
        out_shape=out_shape,
        scratch_shapes=[pltpu.VMEM((C_HEADS, 1), F32)],
        compiler_params=_cparams(2),
        name="fox_proj_prompt",
    )(x, w["norm1_g"], w["win1_t"], w["bf_col"], w["qg_col"], w["kg_col"])


def _fox_attn_kernel(pt_ref,
                     qaug_ref, kaug_ref, vt_ref, qcol_ref, kcol_ref, vnew_ref, lfn_ref,
                     kcache, vcache, lfcache,
                     o_ref, dec_ref,
                     m_scr, acc_scr, vaug_scr, sblk_scr, mblk_scr, ablk_scr,
                     pages, lfpages, page_sem, lf_sem, qb, s_scr, r_scr, ps_scr, l_scr, dacc):
    nq = qaug_ref.shape[2]
    t = ATT_T
    blocks = [(k, q, False) for k in range(nq) for q in range(k + 1, nq)]
    blocks += [(k, k, True) for k in range(nq)]

    d = _DecodeRefs(pt_ref, qcol_ref, kcol_ref, vnew_ref, lfn_ref, kcache, vcache, lfcache, dec_ref,
                    pages, lfpages, page_sem, lf_sem, qb, s_scr, r_scr, ps_scr, l_scr, dacc)
    step = pl.program_id(0) * pl.num_programs(1) + pl.program_id(1)
    n_steps = pl.num_programs(0) * pl.num_programs(1)
    assert d.n_units % DECODE_SLOTS == 0

    @pl.when(step == 0)
    def _():
        for unit in range(DECODE_SLOTS):
            _decode_dma(d, 0, False, unit, unit, start=True)

    m_scr[...] = jnp.full(m_scr.shape, NEG_BIG, F32)
    acc_scr[...] = jnp.zeros(acc_scr.shape, F32)
    extra = lax.broadcasted_iota(jnp.int32, (V_AUG_ROWS - C_HEAD_DIM, t), 0)
    one_row = jnp.where(extra == 0, 1.0, 0.0)
    for kb in range(nq):
        vaug_scr[kb] = jnp.concatenate(
            [vt_ref[0, 0, :, kb * t:(kb + 1) * t], one_row], axis=0).astype(BF16)

    def pieces(masked):
        return [(t // 2, 0, t // 2), (t, t // 2, t)] if masked else [(t, 0, t)]

    def col_max(s):
        slab = jnp.max(s.reshape(MAX_SLABS, s.shape[0] // MAX_SLABS, s.shape[1]), axis=0)
        return jnp.max(slab, axis=0, keepdims=True)

    def start(block, buf):
        s_ref, m_ref, alpha_ref = buf
        ki, qi, masked = block
        maxes = []
        for nk, q0, q1 in pieces(masked):
            keys = kaug_ref[0, 0, ki * t:ki * t + nk, :]
            s = _dot(keys, qaug_ref[0, 0, qi, :, q0:q1])
            if masked:
                causal = (lax.broadcasted_iota(jnp.int32, s.shape, 0)
                          <= lax.broadcasted_iota(jnp.int32, s.shape, 1) + q0)
                s = jnp.where(causal, s, NEG_BIG)
            s_ref[0:nk, q0:q1] = s
            maxes.append(col_max(s))
        m_old = m_scr[qi]
        m_new = jnp.maximum(m_old, jnp.concatenate(maxes, axis=1))
        alpha_ref[...] = jnp.exp2(m_old - m_new)
        m_ref[...] = m_new
        m_scr[qi] = m_new

    def finish(block, buf):
        s_ref, m_ref, alpha_ref = buf
        ki, qi, masked = block
        outs = []
        for nk, q0, q1 in pieces(masked):
            p = jnp.exp2(s_ref[0:nk, q0:q1] - m_ref[:, q0:q1]).astype(BF16)
            outs.append(_dot(vaug_scr[ki, :, 0:nk], p))
        acc_scr[qi] = alpha_ref[...] * acc_scr[qi] + jnp.concatenate(outs, axis=1)

    bufs = [(sblk_scr.at[k], mblk_scr.at[k], ablk_scr.at[k]) for k in range(ATT_BUFFERS)]
    depth = ATT_BUFFERS - 1

    assert len(blocks) % d.n_units == 0
    blocks_per_unit = len(blocks) // d.n_units
    for j in range(depth):
        start(blocks[j], bufs[j])
    for i, block in enumerate(blocks):
        if i % blocks_per_unit == 0:
            _decode_unit(d, step, n_steps, i // blocks_per_unit)
        if i + depth < len(blocks):
            start(blocks[i + depth], bufs[(i + depth) % len(bufs)])
        finish(block, bufs[i % len(bufs)])

    for qi in range(nq):
        o_ref[0, 0, :, qi * t:(qi + 1) * t] = (
            acc_scr[qi, 0:C_HEAD_DIM, :] / acc_scr[qi, C_HEAD_DIM:C_HEAD_DIM + 1, :]).astype(BF16)


def _fox_attention(qaug, kaug, vt, q_s, k_s, v_s, lf_s, cache_k, cache_v, cache_logf, page_table):
    B, H, nq, _, t = qaug.shape
    L = nq * t
    n, n_pages = page_table.shape
    PP = PAGES_PER_STEP
    assert nq % 2 == 0 and n_pages % PP == 0
    assert B * H == 2 * n, "one decode (batch, phase) per prompt (batch, head) grid step"
    kt_cache = jnp.transpose(cache_k, (0, 2, 3, 1))
    vt_cache = jnp.transpose(cache_v, (0, 2, 3, 1))
    lft_cache = jnp.transpose(cache_logf, (0, 2, 1))
    to_cols = lambda a: jnp.transpose(a.reshape(n, C_HEADS, C_HEAD_DIM), (0, 2, 1))
    dec_block = lambda shape: pl.BlockSpec(
        (1,) + shape, lambda b, h, pt: ((b * H + h) // 2, 0, 0))
    hbm = pl.BlockSpec(memory_space=pl.ANY)
    grid_spec = pltpu.PrefetchScalarGridSpec(
        num_scalar_prefetch=1,
        grid=(B, H),
        in_specs=[
            pl.BlockSpec((1, 1, nq, AUG_ROWS, t), lambda b, h, pt: (b, h, 0, 0, 0)),
            pl.BlockSpec((1, 1, L, AUG_ROWS), lambda b, h, pt: (b, h, 0, 0)),
            pl.BlockSpec((1, 1, C_HEAD_DIM, L), lambda b, h, pt: (b, h, 0, 0)),
            dec_block((C_HEAD_DIM, C_HEADS)), dec_block((C_HEAD_DIM, C_HEADS)),
            dec_block((C_HEADS, C_HEAD_DIM)), dec_block((C_HEADS, 1)),
            hbm, hbm, hbm,
        ],
        out_specs=[
            pl.BlockSpec((1, 1, C_HEAD_DIM, L), lambda b, h, pt: (b, h, 0, 0)),
            dec_block((C_HEADS, C_HEAD_DIM)),
        ],
        scratch_shapes=[
            pltpu.VMEM((nq, 1, t), F32),
            pltpu.VMEM((nq, V_AUG_ROWS, t), F32),
            pltpu.VMEM((nq, V_AUG_ROWS, t), BF16),
            pltpu.VMEM((ATT_BUFFERS, t, t), F32),
            pltpu.VMEM((ATT_BUFFERS, 1, t), F32),
            pltpu.VMEM((ATT_BUFFERS, 1, t), F32),
            pltpu.VMEM((DECODE_SLOTS, PP, C_HEADS, C_HEAD_DIM, PAGE_SIZE), F32),
            pltpu.VMEM((DECODE_SLOTS, PP, C_HEADS, PAGE_SIZE), F32),
            pltpu.SemaphoreType.DMA((DECODE_SLOTS,)),
            pltpu.SemaphoreType.DMA((DECODE_SLOTS,)),
            pltpu.VMEM((C_HEADS, C_HEAD_DIM, PAGE_SIZE), F32),
            pltpu.VMEM((n_pages, C_HEADS, PAGE_SIZE), F32),
            pltpu.VMEM((C_HEADS, PAGE_SIZE), F32),
            pltpu.VMEM((C_HEADS, 1), F32),
            pltpu.VMEM((C_HEADS, 1), F32),
            pltpu.VMEM((C_HEADS, C_HEAD_DIM, PAGE_SIZE), F32),
        ],
    )
    att_t, att_s = pl.pallas_call(
        _fox_attn_kernel,
        grid_spec=grid_spec,
        out_shape=[jax.ShapeDtypeStruct((B, H, C_HEAD_DIM, L), BF16),
                   jax.ShapeDtypeStruct((n, C_HEADS, C_HEAD_DIM), F32)],
        compiler_params=_cparams(2),
        name="fox_attention",
    )(page_table, qaug, kaug, vt,
      to_cols(q_s), to_cols(k_s), v_s.reshape(n, C_HEADS, C_HEAD_DIM), lf_s.reshape(n, C_HEADS, 1),
      kt_cache, vt_cache, lft_cache)
    return att_t, att_s.reshape(n, C_WIDTH)


def _gate_out_t_kernel(att_ref, gate_ref, x_ref, wt_ref, y_ref):
    mixed = (att_ref[0].astype(F32) * jax.nn.silu(gate_ref[0].astype(F32))).astype(BF16)
    y_ref[0] = _dot(wt_ref[...], mixed).T + x_ref[0]


def _gate_out_t(att_t, gate_t, x, wout_t):
    B, L, _ = x.shape
    T = OUT_TILE
    assert L % T == 0
    col = pl.BlockSpec((1, C_WIDTH, T), lambda b, i: (b, 0, i))
    row = pl.BlockSpec((1, T, D_MODEL), lambda b, i: (b, i, 0))
    return pl.pallas_call(
        _gate_out_t_kernel,
        grid=(B, L // T),
        in_specs=[col, col, row, pl.BlockSpec((D_MODEL, C_WIDTH), lambda b, i: (0, 0))],
        out_specs=row,
        out_shape=jax.ShapeDtypeStruct((B, L, D_MODEL), F32),
        compiler_params=_cparams(2),
        name="gate_out_prompt",
    )(att_t, gate_t, x, wout_t)


def _fox_proj_sample_kernel(x_ref, g_ref, wt_ref, bf_ref, qg_ref, kg_ref, seg_ref, exp_ref,
                            q_ref, k_ref, v_ref, gate_ref, lf_ref):
    xn = _rms_rows(x_ref[...], g_ref[...]).astype(BF16)

    def head_norm(t, gain):
        ssq = _dot((t * t).astype(BF16), seg_ref[...])
        rs = lax.rsqrt(ssq * (1.0 / C_HEAD_DIM) + EPS)
        hi, lo = _split_bf16(rs)
        rs_full = _dot(hi, exp_ref[...]) + _dot(lo, exp_ref[...])
        return t * rs_full * gain

    def proj(k):
        return _dot_nt(xn, wt_ref[k * C_WIDTH:(k + 1) * C_WIDTH, :])

    q_ref[...] = head_norm(proj(0), qg_ref[...]) * ATTN_SCALE
    k_ref[...] = head_norm(proj(1), kg_ref[...])
    v_ref[...] = proj(2)
    gate_ref[...] = proj(3)
    wf_t = wt_ref[4 * C_WIDTH:4 * C_WIDTH + C_HEADS, :]
    lf_ref[...] = jax.nn.log_sigmoid(_dot_nt(xn, wf_t) + bf_ref[...])


def _fox_proj_sample(x, w):
    n = x.shape[0]
    out_shape = [
        jax.ShapeDtypeStruct((n, C_WIDTH), F32),
        jax.ShapeDtypeStruct((n, C_WIDTH), F32),
        jax.ShapeDtypeStruct((n, C_WIDTH), F32),
        jax.ShapeDtypeStruct((n, C_WIDTH), F32),
        jax.ShapeDtypeStruct((n, C_HEADS), F32),
    ]
    return pl.pallas_call(
        _fox_proj_sample_kernel,
        out_shape=out_shape,
        compiler_params=pltpu.CompilerParams(vmem_limit_bytes=VMEM_LIMIT),
        name="fox_proj_sample",
    )(x, w["norm1_g"], w["win1_t"], w["bf_row"], w["qg"], w["kg"], w["seg"], w["exp"])


def _sublane_total(x):
    acc = x[0:SUBLANES]
    for r in range(1, x.shape[0] // SUBLANES):
        acc = acc + x[r * SUBLANES:(r + 1) * SUBLANES]
    shift = SUBLANES // 2
    while shift >= 1:
        acc = acc + pltpu.roll(acc, shift, axis=0)
        shift //= 2
    return acc


class _DecodeRefs:
    def __init__(self, pt, qcol, kcol, vnew, lfn, kcache, vcache, lfcache, out,
                 pages, lfpages, page_sem, lf_sem, qb, s_scr, r_scr, ps_scr, l_scr, acc):
        self.pt, self.qcol, self.kcol, self.vnew, self.lfn = pt, qcol, kcol, vnew, lfn
        self.kcache, self.vcache, self.lfcache, self.out = kcache, vcache, lfcache, out
        self.pages, self.lfpages, self.page_sem, self.lf_sem = pages, lfpages, page_sem, lf_sem
        self.qb, self.s_scr, self.r_scr, self.ps_scr, self.l_scr, self.acc = (
            qb, s_scr, r_scr, ps_scr, l_scr, acc)
        self.n_pages = pt.shape[1]
        self.n_units = self.n_pages // PAGES_PER_STEP


def _decode_copies(d, batch, phase_is_v, unit, slot):
    src = d.vcache if phase_is_v else d.kcache
    copies = []
    for i in range(PAGES_PER_STEP):
        page = d.pt[batch, d.n_pages - 1 - (unit * PAGES_PER_STEP + i)]
        copies.append(pltpu.make_async_copy(src.at[page], d.pages.at[slot, i], d.page_sem.at[slot]))
        if not phase_is_v:
            copies.append(pltpu.make_async_copy(d.lfcache.at[page], d.lfpages.at[slot, i],
                                                d.lf_sem.at[slot]))
    return copies


def _decode_dma(d, batch, phase_is_v, unit, slot, start):
    for c in _decode_copies(d, batch, phase_is_v, unit, slot):
        if start:
            c.start()
        else:
            c.wait()


def _decode_k_unit(d, unit, slot):
    PP = PAGES_PER_STEP

    @pl.when(unit == 0)
    def _():
        qc = d.qcol[0]
        for h in range(C_HEADS):
            d.qb[h] = jnp.broadcast_to(qc[:, h:h + 1], (C_HEAD_DIM, PAGE_SIZE))
        d.r_scr[...] = jnp.broadcast_to(d.lfn[0], (C_HEADS, PAGE_SIZE))

    slot_i = lax.broadcasted_iota(jnp.int32, (PAGE_SIZE, 2 * PAGE_SIZE), 0)
    slot_j = lax.broadcasted_iota(jnp.int32, (PAGE_SIZE, 2 * PAGE_SIZE), 1)
    sum_mat = jnp.where((slot_j >= PAGE_SIZE) | (slot_i > slot_j), 1.0, 0.0).astype(BF16)
    lf_all = jnp.concatenate([d.lfpages[slot, i] for i in range(PP)], axis=0)
    hi, lo = _split_bf16(lf_all)
    sums = _dot(hi, sum_mat) + _dot(lo, sum_mat)

    sub = lax.broadcasted_iota(jnp.int32, (SUBLANES, PAGE_SIZE), 0)
    groups = C_HEADS // SUBLANES
    s_parts = [[jnp.zeros((SUBLANES, PAGE_SIZE), F32) for _ in range(groups)] for _ in range(PP)]
    for h in range(C_HEADS):
        qh = d.qb[h]
        for i in range(PP):
            tot = _sublane_total(d.pages[slot, i, h] * qh)
            g = h // SUBLANES
            s_parts[i][g] = jnp.where(sub == h % SUBLANES, tot, s_parts[i][g])
    r = d.r_scr[...]
    for i in range(PP):
        s = jnp.concatenate(s_parts[i], axis=0)
        page = sums[i * C_HEADS:(i + 1) * C_HEADS]
        d.s_scr[unit * PP + i] = s + page[:, 0:PAGE_SIZE] + r
        r = r + page[:, PAGE_SIZE:]
    d.r_scr[...] = r


def _decode_v_unit(d, unit, slot):
    PP = PAGES_PER_STEP

    @pl.when(unit == 0)
    def _():
        eye = (lax.broadcasted_iota(jnp.int32, (C_HEADS, C_HEADS), 0)
               == lax.broadcasted_iota(jnp.int32, (C_HEADS, C_HEADS), 1))
        self_row = jnp.sum(d.qcol[0] * d.kcol[0], axis=0, keepdims=True)
        s_self = jnp.sum(jnp.where(eye, self_row, 0.0), axis=1, keepdims=True)
        s_all = d.s_scr[...]
        m = jnp.max(jnp.max(s_all, axis=0), axis=1, keepdims=True)
        m = jnp.maximum(m, s_self)
        p_all = jnp.exp(s_all - m)
        d.s_scr[...] = p_all
        p_self = jnp.exp(s_self - m)
        d.ps_scr[...] = p_self
        d.l_scr[...] = jnp.sum(jnp.sum(p_all, axis=0), axis=1, keepdims=True) + p_self
        d.acc[...] = jnp.zeros(d.acc.shape, F32)

    p_pages = [d.s_scr[unit * PP + i] for i in range(PP)]
    for h in range(C_HEADS):
        a = d.acc[h]
        for i in range(PP):
            a = a + p_pages[i][h:h + 1, :] * d.pages[slot, i, h]
        d.acc[h] = a

    @pl.when(unit == d.n_units - 1)
    def _():
        ones = jnp.ones((SUBLANES, PAGE_SIZE), BF16)
        rows = lax.broadcasted_iota(jnp.int32, (C_HEADS, C_HEAD_DIM), 0)
        out = jnp.zeros((C_HEADS, C_HEAD_DIM), F32)
        for h in range(C_HEADS):
            hi, lo = _split_bf16(d.acc[h])
            tot = _dot_nt(ones, hi) + _dot_nt(ones, lo)
            out = jnp.where(rows == h, tot[0:1, :], out)
        d.out[0] = (out + d.ps_scr[...] * d.vnew[0]) / d.l_scr[...]


def _decode_unit(d, step, n_steps, unit):
    batch = step // 2
    ahead = DECODE_SLOTS
    slot = unit % ahead
    for phase_is_v in (False, True):
        @pl.when(step % 2 == int(phase_is_v))
        def _():
            _decode_dma(d, batch, phase_is_v, unit, slot, start=False)
            if phase_is_v:
                _decode_v_unit(d, unit, slot)
            else:
                _decode_k_unit(d, unit, slot)

            @pl.when(unit + ahead < d.n_units)
            def _():
                _decode_dma(d, batch, phase_is_v, unit + ahead, slot, start=True)

            @pl.when((unit + ahead >= d.n_units) & (step < n_steps - 1))
            def _():
                _decode_dma(d, (step + 1) // 2, not phase_is_v, unit + ahead - d.n_units, slot, start=True)


def _gate_out_kernel(att_ref, gate_ref, x_ref, w_ref, y_ref):
    mixed = (att_ref[...] * jax.nn.silu(gate_ref[...])).astype(BF16)
    y_ref[...] = _dot(mixed, w_ref[...]) + x_ref[...]


def _gate_out(att, gate, x, wout):
    return pl.pallas_call(
        _gate_out_kernel,
        out_shape=jax.ShapeDtypeStruct(x.shape, F32),
        compiler_params=pltpu.CompilerParams(vmem_limit_bytes=VMEM_LIMIT),
        name="gate_out_sample",
    )(att, gate, x, wout)


def _block_diag(w):
    nb, d, _ = w.shape
    per = nb // GATE_GROUPS
    wg = w.reshape(GATE_GROUPS, per, d, d)
    eye = jnp.eye(per, dtype=w.dtype)
    return (eye[None, :, None, :, None] * wg[:, :, :, None, :]).reshape(GATE_GROUPS, per * d, per * d)


def _prepare_weights(norm0_g, w_in0, gmlp_v_g, gmlp_w_s, gmlp_b_s, lru_conv_w, lru_conv_b,
                     lru_w_r, lru_b_r, lru_w_i, lru_b_i, lru_lambda, w_out0, norm1_g, w_in1,
                     fox_b_f, q_norm_g, k_norm_g, w_out1):
    lane = np.arange(C_WIDTH) // C_HEAD_DIM
    seg = (lane[:, None] == np.arange(LANES)[None, :]).astype(np.float32)
    win1_t = w_in1.T.astype(BF16)
    wout1 = w_out1.astype(BF16)
    return {
        "norm0_g": norm0_g.reshape(1, D_MODEL),
        "win0": w_in0.astype(BF16),
        "vg": gmlp_v_g.reshape(1, A_WIDTH),
        "ws": gmlp_w_s,
        "bst": gmlp_b_s.T,
        "cw": lru_conv_w,
        "cb": lru_conv_b.reshape(1, B_WIDTH),
        "wr": _block_diag(lru_w_r).astype(BF16),
        "br": lru_b_r.reshape(1, B_WIDTH),
        "wi": _block_diag(lru_w_i).astype(BF16),
        "bi": lru_b_i.reshape(1, B_WIDTH),
        "lam": lru_lambda.reshape(1, B_WIDTH),
        "wout0": w_out0.astype(BF16),
        "norm1_g": norm1_g.reshape(1, D_MODEL),
        "win1_t": win1_t,
        "bf_row": fox_b_f.reshape(1, C_HEADS),
        "bf_col": fox_b_f.reshape(C_HEADS, 1),
        "qg": jnp.tile(q_norm_g, C_HEADS).reshape(1, C_WIDTH),
        "kg": jnp.tile(k_norm_g, C_HEADS).reshape(1, C_WIDTH),
        "qg_col": q_norm_g.reshape(C_HEAD_DIM, 1),
        "kg_col": k_norm_g.reshape(C_HEAD_DIM, 1),
        "seg": jnp.asarray(seg, BF16),
        "exp": jnp.asarray(seg.T, BF16),
        "wout1": wout1,
        "wout1_t": wout1.T,
    }


def kernel(x_prompt, x_sample, state_lru_conv, state_lru_h, cache_k, cache_v, cache_logf, page_table, norm0_g, w_in0, gmlp_v_g, gmlp_w_s, gmlp_b_s, lru_conv_w, lru_conv_b, lru_w_r, lru_b_r, lru_w_i, lru_b_i, lru_lambda, w_out0, norm1_g, w_in1, fox_b_f, q_norm_g, k_norm_g, w_out1):
    Bp, L, _ = x_prompt.shape
    Bs = x_sample.shape[0]
    w = _prepare_weights(norm0_g, w_in0, gmlp_v_g, gmlp_w_s, gmlp_b_s, lru_conv_w, lru_conv_b,
                         lru_w_r, lru_b_r, lru_w_i, lru_b_i, lru_lambda, w_out0, norm1_g, w_in1,
                         fox_b_f, q_norm_g, k_norm_g, w_out1)

    conv0 = jnp.zeros((Bp, CONV_W - 1, B_WIDTH), F32)
    h0 = jnp.zeros((Bp, B_WIDTH), F32)
    yp0, lru_conv_p, lru_h_p = _layer0_prompt(x_prompt, conv0, h0, w)
    ys0, gmlp_v_s, conv_s, lru_h_s = _layer0_sample(
        x_sample.reshape(Bs, D_MODEL), state_lru_conv, state_lru_h, w)

    qaug, kaug, kt_p, vt_p, gt_p, lft_p = _fox_proj_prompt(yp0, w)
    q_s, k_s, v_s, g_s, logf_s = _fox_proj_sample(ys0, w)
    att_t, att_s = _fox_attention(qaug, kaug, vt_p, q_s, k_s, v_s, logf_s,
                                  cache_k, cache_v, cache_logf, page_table)
    yp = _gate_out_t(att_t.reshape(Bp, C_WIDTH, L), gt_p, yp0, w["wout1_t"])
    k_p = jnp.transpose(kt_p, (0, 3, 1, 2))
    v_p = jnp.transpose(vt_p, (0, 3, 1, 2))
    logf_p = jnp.transpose(lft_p, (0, 2, 1))
    ys = _gate_out(att_s, g_s, ys0, w["wout1"])

    return (yp, ys.reshape(Bs, 1, D_MODEL), lru_conv_p, lru_h_p, k_p, v_p, logf_p,
            gmlp_v_s.reshape(Bs, 1, A_WIDTH), conv_s.reshape(Bs, CONV_W - 1, B_WIDTH), lru_h_s,
            k_s.reshape(Bs, 1, C_HEADS, C_HEAD_DIM), v_s.reshape(Bs, 1, C_HEADS, C_HEAD_DIM),
            logf_s.reshape(Bs, 1, C_HEADS))
```

```python
import jax
import jax.numpy as jnp
import numpy as np
from jax import lax
from jax.experimental import pallas as pl
from jax.experimental.pallas import tpu as pltpu

D_MODEL = 1024
A_WIDTH = 512
A_GROUPS = 4
A_GROUP_DIM = 128
CHUNK = 128
B_WIDTH = 512
B_BLOCKS = 8
B_BLOCK_DIM = 64
CONV_W = 4
LRU_C = 8.0
C_HEADS = 16
C_HEAD_DIM = 64
C_WIDTH = 1024
PAGE_SIZE = 128
ATTN_SCALE = C_HEAD_DIM ** -0.5
EPS = 1e-6

LANES = 128
SUBLANES = 8
NEG_BIG = -1e30

F32 = jnp.float32
BF16 = jnp.bfloat16

L0_TILE = 1024
PROJ_TILE = 512
ATT_T = 512
OUT_TILE = 1024
PAGES_PER_STEP = 16
DECODE_SLOTS = 2
GATE_GROUPS = 2
ATT_BUFFERS = 3
MAX_SLABS = 8
AUG_ROWS = 128
V_AUG_ROWS = 80
LOG2E = 1.4426950408889634
N_BIAS = 3
VMEM_LIMIT = 56 * 1024 * 1024


def _cparams(n_grid_dims):
    return pltpu.CompilerParams(
        dimension_semantics=("arbitrary",) * n_grid_dims,
        vmem_limit_bytes=VMEM_LIMIT,
    )


def _dot(a, b):
    return jnp.dot(a, b, preferred_element_type=F32)


def _dot_nt(a, b):
    return lax.dot_general(a, b, (((1,), (1,)), ((), ())), preferred_element_type=F32)


def _split_bf16(x):
    hi = x.astype(BF16)
    lo = (x - hi.astype(F32)).astype(BF16)
    return hi, lo


def _rms_rows(x, gain):
    ms = jnp.mean(x * x, axis=-1, keepdims=True)
    return x * lax.rsqrt(ms + EPS) * gain


def _gelu_tanh(x):
    c1 = 0.7978845608028654
    hx = 0.5 * x
    return hx + hx * jnp.tanh(x * (c1 + (c1 * 0.044715) * (x * x)))


def _gmlp_v_rows(pv, vg_ref):
    v = _gelu_tanh(pv)
    parts = []
    for g in range(A_GROUPS):
        sl = slice(g * A_GROUP_DIM, (g + 1) * A_GROUP_DIM)
        parts.append(_rms_rows(v[:, sl], vg_ref[:, sl]))
    return jnp.concatenate(parts, axis=-1)


def _lru_gates(xc, wr_ref, br_ref, wi_ref, bi_ref, lam_ref):
    xcb = xc.astype(BF16)
    def block_gate(w_ref, b_ref):
        gw = B_WIDTH // GATE_GROUPS
        parts = [_dot(xcb[:, g * gw:(g + 1) * gw], w_ref[g]) for g in range(GATE_GROUPS)]
        return jax.nn.sigmoid(jnp.concatenate(parts, axis=1) + b_ref[...])

    r = block_gate(wr_ref, br_ref)
    gi = block_gate(wi_ref, bi_ref)
    log_a = -LRU_C * r * jax.nn.softplus(-lam_ref[...])
    a = jnp.exp(log_a)
    bterm = jnp.sqrt(-jnp.tanh(log_a) * (1.0 + a * a)) * (gi * xc)
    return a, bterm


def _layer0_prompt_kernel(x_ref, g_ref, win_ref, vg_ref, ws_ref, bst_ref, cw_ref, cb_ref,
                          wr_ref, br_ref, wi_ref, bi_ref, lam_ref, wout_ref, conv0_ref, h0_ref,
                          y_ref, convo_ref, ho_ref,
                          xbuf, hcar, a8, b8, s_scr):
    T = L0_TILE
    i = pl.program_id(1)

    @pl.when(i == 0)
    def _():
        xbuf[0:SUBLANES, :] = jnp.zeros((SUBLANES, B_WIDTH), F32)
        xbuf[SUBLANES - (CONV_W - 1):SUBLANES, :] = conv0_ref[0]
        hcar[...] = h0_ref[0]

    x = x_ref[0]
    xn = _rms_rows(x, g_ref[...]).astype(BF16)

    def proj(k):
        return _dot(xn, win_ref[:, k * 512:(k + 1) * 512])

    vn = _gmlp_v_rows(proj(1), vg_ref).astype(BF16)
    tri = (lax.broadcasted_iota(jnp.int32, (CHUNK, CHUNK), 0)
           >= lax.broadcasted_iota(jnp.int32, (CHUNK, CHUNK), 1))
    for g in range(A_GROUPS):
        wg = jnp.where(tri, ws_ref[g], 0.0).astype(BF16)
        bias = bst_ref[:, g:g + 1]
        for c in range(T // CHUNK):
            blk = vn[c * CHUNK:(c + 1) * CHUNK, g * A_GROUP_DIM:(g + 1) * A_GROUP_DIM]
            s_scr[c * CHUNK:(c + 1) * CHUNK, g * A_GROUP_DIM:(g + 1) * A_GROUP_DIM] = (
                _dot(wg, blk) + bias)
    u = _gelu_tanh(proj(0))
    mix_a = (u * s_scr[...] * jax.nn.silu(proj(2))).astype(BF16)

    xb = proj(3)
    xbuf[SUBLANES:SUBLANES + T, :] = xb
    xc = cb_ref[...] + cw_ref[3:4, :] * xb
    for tap in range(CONV_W - 1):
        off = SUBLANES - (CONV_W - 1) + tap
        xc = xc + cw_ref[tap:tap + 1, :] * xbuf[off:off + T, :]
    convo_ref[0] = xbuf[T + SUBLANES - (CONV_W - 1):T + SUBLANES, :]
    xbuf[0:SUBLANES, :] = xbuf[T:T + SUBLANES, :]

    a, bt = _lru_gates(xc, wr_ref, br_ref, wi_ref, bi_ref, lam_ref)

    a = a.reshape(T // SUBLANES, SUBLANES, B_WIDTH)
    bt = bt.reshape(T // SUBLANES, SUBLANES, B_WIDTH)
    row = lax.broadcasted_iota(jnp.int32, (1, SUBLANES, B_WIDTH), 1)
    shift = 1
    while shift < SUBLANES:
        a_sh = pltpu.roll(a, shift, axis=1)
        b_sh = pltpu.roll(bt, shift, axis=1)
        m = row >= shift
        bt = jnp.where(m, a * b_sh + bt, bt)
        a = jnp.where(m, a * a_sh, a)
        shift *= 2
    a8[...] = a.reshape(T, B_WIDTH)
    b8[...] = bt.reshape(T, B_WIDTH)

    def group_step(j, h):
        off = pl.multiple_of(j * SUBLANES, SUBLANES)
        rows = a8[pl.ds(off, SUBLANES), :] * h + b8[pl.ds(off, SUBLANES), :]
        b8[pl.ds(off, SUBLANES), :] = rows
        return rows[SUBLANES - 1:SUBLANES, :]

    h_last = lax.fori_loop(0, T // SUBLANES, group_step, hcar[...], unroll=True)
    hcar[...] = h_last
    ho_ref[0] = h_last

    mix_b = (b8[...] * jax.nn.silu(proj(4))).astype(BF16)

    y = _dot(mix_a, wout_ref[0:A_WIDTH, :]) + _dot(mix_b, wout_ref[A_WIDTH:, :]) + x
    y_ref[0] = y


def _layer0_prompt(x, conv0, h0, w):
    B, L, _ = x.shape
    T = L0_TILE
    assert L % T == 0 and T % CHUNK == 0
    nt = L // T
    full = lambda shape: pl.BlockSpec(shape, lambda b, i: (0,) * len(shape))
    in_specs = [
        pl.BlockSpec((1, T, D_MODEL), lambda b, i: (b, i, 0)),
        full((1, D_MODEL)),
        full(w["win0"].shape),
        full((1, A_WIDTH)),
        full((A_GROUPS, CHUNK, CHUNK)),
        full((CHUNK, A_GROUPS)),
        full((CONV_W, B_WIDTH)),
        full((1, B_WIDTH)),
        full((GATE_GROUPS, B_WIDTH // GATE_GROUPS, B_WIDTH // GATE_GROUPS)), full((1, B_WIDTH)),
        full((GATE_GROUPS, B_WIDTH // GATE_GROUPS, B_WIDTH // GATE_GROUPS)), full((1, B_WIDTH)),
        full((1, B_WIDTH)),
        full((D_MODEL, D_MODEL)),
        pl.BlockSpec((1, CONV_W - 1, B_WIDTH), lambda b, i: (b, 0, 0)),
        pl.BlockSpec((1, 1, B_WIDTH), lambda b, i: (b, 0, 0)),
    ]
    out_specs = [
        pl.BlockSpec((1, T, D_MODEL), lambda b, i: (b, i, 0)),
        pl.BlockSpec((1, CONV_W - 1, B_WIDTH), lambda b, i: (b, 0, 0)),
        pl.BlockSpec((1, 1, B_WIDTH), lambda b, i: (b, 0, 0)),
    ]
    out_shape = [
        jax.ShapeDtypeStruct((B, L, D_MODEL), F32),
        jax.ShapeDtypeStruct((B, CONV_W - 1, B_WIDTH), F32),
        jax.ShapeDtypeStruct((B, 1, B_WIDTH), F32),
    ]
    y, convo, ho = pl.pallas_call(
        _layer0_prompt_kernel,
        grid=(B, nt),
        in_specs=in_specs,
        out_specs=out_specs,
        out_shape=out_shape,
        scratch_shapes=[
            pltpu.VMEM((T + SUBLANES, B_WIDTH), F32),
            pltpu.VMEM((1, B_WIDTH), F32),
            pltpu.VMEM((T, B_WIDTH), F32),
            pltpu.VMEM((T, B_WIDTH), F32),
            pltpu.VMEM((T, A_WIDTH), F32),
        ],
        compiler_params=_cparams(2),
        name="layer0_prompt",
    )(x, w["norm0_g"], w["win0"], w["vg"], w["ws"], w["bst"], w["cw"], w["cb"],
      w["wr"], w["br"], w["wi"], w["bi"], w["lam"], w["wout0"], conv0, h0.reshape(B, 1, B_WIDTH))
    return y, convo, ho.reshape(B, B_WIDTH)


def _layer0_sample_kernel(x_ref, g_ref, win_ref, vg_ref, ws_ref, bst_ref, cw_ref, cb_ref,
                          wr_ref, br_ref, wi_ref, bi_ref, lam_ref, wout_ref, conv_ref, h_ref,
                          y_ref, v_ref, convo_ref, ho_ref):
    x = x_ref[...]
    xn = _rms_rows(x, g_ref[...]).astype(BF16)

    def proj(k):
        return _dot(xn, win_ref[:, k * 512:(k + 1) * 512])

    vn = _gmlp_v_rows(proj(1), vg_ref)
    v_ref[...] = vn
    s_parts = []
    for g in range(A_GROUPS):
        sl = slice(g * A_GROUP_DIM, (g + 1) * A_GROUP_DIM)
        s_parts.append(ws_ref[g, 0:1, 0:1] * vn[:, sl] + bst_ref[0:1, g:g + 1])
    s = jnp.concatenate(s_parts, axis=-1)
    mix_a = (_gelu_tanh(proj(0)) * s * jax.nn.silu(proj(2))).astype(BF16)

    xb = proj(3)
    xc = cb_ref[...] + cw_ref[3:4, :] * xb
    for tap in range(CONV_W - 1):
        xc = xc + cw_ref[tap:tap + 1, :] * conv_ref[:, tap * B_WIDTH:(tap + 1) * B_WIDTH]
    for tap in range(CONV_W - 2):
        convo_ref[:, tap * B_WIDTH:(tap + 1) * B_WIDTH] = (
            conv_ref[:, (tap + 1) * B_WIDTH:(tap + 2) * B_WIDTH])
    convo_ref[:, (CONV_W - 2) * B_WIDTH:] = xb

    a, bt = _lru_gates(xc, wr_ref, br_ref, wi_ref, bi_ref, lam_ref)
    h = a * h_ref[...] + bt
    ho_ref[...] = h
    mix_b = (h * jax.nn.silu(proj(4))).astype(BF16)
    y_ref[...] = _dot(mix_a, wout_ref[0:A_WIDTH, :]) + _dot(mix_b, wout_ref[A_WIDTH:, :]) + x


def _layer0_sample(x, conv, h, w):
    n = x.shape[0]
    out_shape = [
        jax.ShapeDtypeStruct((n, D_MODEL), F32),
        jax.ShapeDtypeStruct((n, A_WIDTH), F32),
        jax.ShapeDtypeStruct((n, (CONV_W - 1) * B_WIDTH), F32),
        jax.ShapeDtypeStruct((n, B_WIDTH), F32),
    ]
    return pl.pallas_call(
        _layer0_sample_kernel,
        out_shape=out_shape,
        compiler_params=pltpu.CompilerParams(vmem_limit_bytes=VMEM_LIMIT),
        name="layer0_sample",
    )(x, w["norm0_g"], w["win0"], w["vg"], w["ws"], w["bst"], w["cw"], w["cb"],
      w["wr"], w["br"], w["wi"], w["bi"], w["lam"], w["wout0"],
      conv.reshape(n, (CONV_W - 1) * B_WIDTH), h)


def _head_norm_cols(t, gain_col):
    ms = jnp.mean(t * t, axis=0, keepdims=True)
    return t * lax.rsqrt(ms + EPS) * gain_col


def _fox_proj_prompt_kernel(x_ref, g_ref, wt_ref, bf_ref, qg_ref, kg_ref,
                            qaug_ref, kaug_ref, kt_ref, vt_ref, gt_ref, lf_ref, carry):
    T = PROJ_TILE
    i = pl.program_id(1)

    @pl.when(i == 0)
    def _():
        carry[...] = jnp.zeros_like(carry)

    xn = _rms_rows(x_ref[0], g_ref[...])
    xnt = xn.T.astype(BF16)

    def proj_t(k):
        return _dot(wt_ref[k * C_WIDTH:(k + 1) * C_WIDTH, :], xnt)

    wf_t = wt_ref[4 * C_WIDTH:4 * C_WIDTH + C_HEADS, :]
    lf = jax.nn.log_sigmoid(_dot(wf_t, xnt) + bf_ref[...])
    lf_ref[0] = lf
    upper = (lax.broadcasted_iota(jnp.int32, (T, T), 0)
             <= lax.broadcasted_iota(jnp.int32, (T, T), 1)).astype(BF16)
    hi, lo = _split_bf16(lf)
    c = _dot(hi, upper) + _dot(lo, upper) + carry[...]
    carry[...] = c[:, T - 1:T]
    pieces, rest = [], c * (-LOG2E)
    for _ in range(N_BIAS):
        p = rest.astype(BF16).astype(F32)
        pieces.append(p)
        rest = rest - p

    sub = lax.broadcasted_iota(jnp.int32, (SUBLANES, T), 0)
    ones_rows = jnp.where(sub < N_BIAS, 1.0, 0.0)
    pad = jnp.zeros((AUG_ROWS - C_HEAD_DIM - SUBLANES, T), F32)
    qt = proj_t(0)
    kt = proj_t(1)
    for h in range(C_HEADS):
        sl = slice(h * C_HEAD_DIM, (h + 1) * C_HEAD_DIM)
        qn = _head_norm_cols(qt[sl], qg_ref[...]) * (ATTN_SCALE * LOG2E)
        kn = _head_norm_cols(kt[sl], kg_ref[...])
        kt_ref[0, h] = kn
        qaug_ref[0, h, 0] = jnp.concatenate([qn, ones_rows, pad], axis=0).astype(BF16)
        bias_rows = jnp.zeros((SUBLANES, T), F32)
        for j in range(N_BIAS):
            bias_rows = jnp.where(sub == j, pieces[j][h:h + 1, :], bias_rows)
        kaug_t = jnp.concatenate([kn, bias_rows, pad], axis=0)
        kaug_ref[0, h] = kaug_t.T.astype(BF16)
    vt_ref[0] = proj_t(2).reshape(C_HEADS, C_HEAD_DIM, T)
    gt_ref[0] = proj_t(3).astype(BF16)


def _fox_proj_prompt(x, w):
    B, L, _ = x.shape
    T = PROJ_TILE
    assert L % ATT_T == 0 and ATT_T % T == 0
    per_q = ATT_T // T
    full = lambda shape: pl.BlockSpec(shape, lambda b, i: (0,) * len(shape))
    in_specs = [
        pl.BlockSpec((1, T, D_MODEL), lambda b, i: (b, i, 0)),
        full((1, D_MODEL)), full((4 * C_WIDTH + C_HEADS, D_MODEL)),
        full((C_HEADS, 1)), full((C_HEAD_DIM, 1)), full((C_HEAD_DIM, 1)),
    ]
    out_specs = [
        pl.BlockSpec((1, C_HEADS, 1, AUG_ROWS, T), lambda b, i: (b, 0, i // per_q, 0, i % per_q)),
        pl.BlockSpec((1, C_HEADS, T, AUG_ROWS), lambda b, i: (b, 0, i, 0)),
        pl.BlockSpec((1, C_HEADS, C_HEAD_DIM, T), lambda b, i: (b, 0, 0, i)),
        pl.BlockSpec((1, C_HEADS, C_HEAD_DIM, T), lambda b, i: (b, 0, 0, i)),
        pl.BlockSpec((1, C_WIDTH, T), lambda b, i: (b, 0, i)),
        pl.BlockSpec((1, C_HEADS, T), lambda b, i: (b, 0, i)),
    ]
    out_shape = [
        jax.ShapeDtypeStruct((B, C_HEADS, L // ATT_T, AUG_ROWS, ATT_T), BF16),
        jax.ShapeDtypeStruct((B, C_HEADS, L, AUG_ROWS), BF16),
        jax.ShapeDtypeStruct((B, C_HEADS, C_HEAD_DIM, L), F32),
        jax.ShapeDtypeStruct((B, C_HEADS, C_HEAD_DIM, L), F32),
        jax.ShapeDtypeStruct((B, C_WIDTH, L), BF16),
        jax.ShapeDtypeStruct((B, C_HEADS, L), F32),
    ]
    return pl.pallas_call(
        _fox_proj_prompt_kernel,
        grid=(B, L // T),
        in_specs=in_specs,
        out_specs=out_specs,
        out_shape=out_shape,
        scratch_shapes=[pltpu.VMEM((C_HEADS, 1), F32)],
        compiler_params=_cparams(2),
        name="fox_proj_prompt",
    )(x, w["norm1_g"], w["win1_t"], w["bf_col"], w["qg_col"], w["kg_col"])


def _fox_attn_kernel(pt_ref,
                     qaug_ref, kaug_ref, vt_ref, qcol_ref, kcol_ref, vnew_ref, lfn_ref,
                     kcache, vcache, lfcache,
                     o_ref, dec_ref,
                     m_scr, acc_scr, vaug_scr, sblk_scr, mblk_scr, ablk_scr,
                     pages, lfpages, page_sem, lf_sem, qb, s_scr, r_scr, ps_scr, l_scr, dacc):
    nq = qaug_ref.shape[2]
    t = ATT_T
    blocks = [(k, q, False) for k in range(nq) for q in range(k + 1, nq)]
    blocks += [(k, k, True) for k in range(nq)]

    d = _DecodeRefs(pt_ref, qcol_ref, kcol_ref, vnew_ref, lfn_ref, kcache, vcache, lfcache, dec_ref,
                    pages, lfpages, page_sem, lf_sem, qb, s_scr, r_scr, ps_scr, l_scr, dacc)
    step = pl.program_id(0) * pl.num_programs(1) + pl.program_id(1)
    n_steps = pl.num_programs(0) * pl.num_programs(1)
    assert d.n_units % DECODE_SLOTS == 0

    @pl.when(step == 0)
    def _():
        for unit in range(DECODE_SLOTS):
            _decode_dma(d, 0, False, unit, unit, start=True)

    m_scr[...] = jnp.full(m_scr.shape, NEG_BIG, F32)
    acc_scr[...] = jnp.zeros(acc_scr.shape, F32)
    extra = lax.broadcasted_iota(jnp.int32, (V_AUG_ROWS - C_HEAD_DIM, t), 0)
    one_row = jnp.where(extra == 0, 1.0, 0.0)
    for kb in range(nq):
        vaug_scr[kb] = jnp.concatenate(
            [vt_ref[0, 0, :, kb * t:(kb + 1) * t], one_row], axis=0).astype(BF16)

    def pieces(masked):
        return [(t // 2, 0, t // 2), (t, t // 2, t)] if masked else [(t, 0, t)]

    def col_max(s):
        slab = jnp.max(s.reshape(MAX_SLABS, s.shape[0] // MAX_SLABS, s.shape[1]), axis=0)
        return jnp.max(slab, axis=0, keepdims=True)

    def start(block, buf):
        s_ref, m_ref, alpha_ref = buf
        ki, qi, masked = block
        maxes = []
        for nk, q0, q1 in pieces(masked):
            keys = kaug_ref[0, 0, ki * t:ki * t + nk, :]
            s = _dot(keys, qaug_ref[0, 0, qi, :, q0:q1])
            if masked:
                causal = (lax.broadcasted_iota(jnp.int32, s.shape, 0)
                          <= lax.broadcasted_iota(jnp.int32, s.shape, 1) + q0)
                s = jnp.where(causal, s, NEG_BIG)
            s_ref[0:nk, q0:q1] = s
            maxes.append(col_max(s))
        m_old = m_scr[qi]
        m_new = jnp.maximum(m_old, jnp.concatenate(maxes, axis=1))
        alpha_ref[...] = jnp.exp2(m_old - m_new)
        m_ref[...] = m_new
        m_scr[qi] = m_new

    def finish(block, buf):
        s_ref, m_ref, alpha_ref = buf
        ki, qi, masked = block
        outs = []
        for nk, q0, q1 in pieces(masked):
            p = jnp.exp2(s_ref[0:nk, q0:q1] - m_ref[:, q0:q1]).astype(BF16)
            outs.append(_dot(vaug_scr[ki, :, 0:nk], p))
        acc_scr[qi] = alpha_ref[...] * acc_scr[qi] + jnp.concatenate(outs, axis=1)

    bufs = [(sblk_scr.at[k], mblk_scr.at[k], ablk_scr.at[k]) for k in range(ATT_BUFFERS)]
    depth = ATT_BUFFERS - 1

    assert len(blocks) % d.n_units == 0
    blocks_per_unit = len(blocks) // d.n_units
    for j in range(depth):
        start(blocks[j], bufs[j])
    for i, block in enumerate(blocks):
        if i % blocks_per_unit == 0:
            _decode_unit(d, step, n_steps, i // blocks_per_unit)
        if i + depth < len(blocks):
            start(blocks[i + depth], bufs[(i + depth) % len(bufs)])
        finish(block, bufs[i % len(bufs)])

    for qi in range(nq):
        o_ref[0, 0, :, qi * t:(qi + 1) * t] = (
            acc_scr[qi, 0:C_HEAD_DIM, :] / acc_scr[qi, C_HEAD_DIM:C_HEAD_DIM + 1, :]).astype(BF16)


def _fox_attention(qaug, kaug, vt, q_s, k_s, v_s, lf_s, cache_k, cache_v, cache_logf, page_table):
    B, H, nq, _, t = qaug.shape
    L = nq * t
    n, n_pages = page_table.shape
    PP = PAGES_PER_STEP
    assert nq % 2 == 0 and n_pages % PP == 0
    assert B * H == 2 * n, "one decode (batch, phase) per prompt (batch, head) grid step"
    kt_cache = jnp.transpose(cache_k, (0, 2, 3, 1))
    vt_cache = jnp.transpose(cache_v, (0, 2, 3, 1))
    lft_cache = jnp.transpose(cache_logf, (0, 2, 1))
    to_cols = lambda a: jnp.transpose(a.reshape(n, C_HEADS, C_HEAD_DIM), (0, 2, 1))
    dec_block = lambda shape: pl.BlockSpec(
        (1,) + shape, lambda b, h, pt: ((b * H + h) // 2, 0, 0))
    hbm = pl.BlockSpec(memory_space=pl.ANY)
    grid_spec = pltpu.PrefetchScalarGridSpec(
        num_scalar_prefetch=1,
        grid=(B, H),
        in_specs=[
            pl.BlockSpec((1, 1, nq, AUG_ROWS, t), lambda b, h, pt: (b, h, 0, 0, 0)),
            pl.BlockSpec((1, 1, L, AUG_ROWS), lambda b, h, pt: (b, h, 0, 0)),
            pl.BlockSpec((1, 1, C_HEAD_DIM, L), lambda b, h, pt: (b, h, 0, 0)),
            dec_block((C_HEAD_DIM, C_HEADS)), dec_block((C_HEAD_DIM, C_HEADS)),
            dec_block((C_HEADS, C_HEAD_DIM)), dec_block((C_HEADS, 1)),
            hbm, hbm, hbm,
        ],
        out_specs=[
            pl.BlockSpec((1, 1, C_HEAD_DIM, L), lambda b, h, pt: (b, h, 0, 0)),
            dec_block((C_HEADS, C_HEAD_DIM)),
        ],
        scratch_shapes=[
            pltpu.VMEM((nq, 1, t), F32),
            pltpu.VMEM((nq, V_AUG_ROWS, t), F32),
            pltpu.VMEM((nq, V_AUG_ROWS, t), BF16),
            pltpu.VMEM((ATT_BUFFERS, t, t), F32),
            pltpu.VMEM((ATT_BUFFERS, 1, t), F32),
            pltpu.VMEM((ATT_BUFFERS, 1, t), F32),
            pltpu.VMEM((DECODE_SLOTS, PP, C_HEADS, C_HEAD_DIM, PAGE_SIZE), F32),
            pltpu.VMEM((DECODE_SLOTS, PP, C_HEADS, PAGE_SIZE), F32),
            pltpu.SemaphoreType.DMA((DECODE_SLOTS,)),
            pltpu.SemaphoreType.DMA((DECODE_SLOTS,)),
            pltpu.VMEM((C_HEADS, C_HEAD_DIM, PAGE_SIZE), F32),
            pltpu.VMEM((n_pages, C_HEADS, PAGE_SIZE), F32),
            pltpu.VMEM((C_HEADS, PAGE_SIZE), F32),
            pltpu.VMEM((C_HEADS, 1), F32),
            pltpu.VMEM((C_HEADS, 1), F32),
            pltpu.VMEM((C_HEADS, C_HEAD_DIM, PAGE_SIZE), F32),
        ],
    )
    att_t, att_s = pl.pallas_call(
        _fox_attn_kernel,
        grid_spec=grid_spec,
        out_shape=[jax.ShapeDtypeStruct((B, H, C_HEAD_DIM, L), BF16),
                   jax.ShapeDtypeStruct((n, C_HEADS, C_HEAD_DIM), F32)],
        compiler_params=_cparams(2),
        name="fox_attention",
    )(page_table, qaug, kaug, vt,
      to_cols(q_s), to_cols(k_s), v_s.reshape(n, C_HEADS, C_HEAD_DIM), lf_s.reshape(n, C_HEADS, 1),
      kt_cache, vt_cache, lft_cache)
    return att_t, att_s.reshape(n, C_WIDTH)


def _gate_out_t_kernel(att_ref, gate_ref, x_ref, wt_ref, y_ref):
    mixed = (att_ref[0].astype(F32) * jax.nn.silu(gate_ref[0].astype(F32))).astype(BF16)
    y_ref[0] = _dot(wt_ref[...], mixed).T + x_ref[0]


def _gate_out_t(att_t, gate_t, x, wout_t):
    B, L, _ = x.shape
    T = OUT_TILE
    assert L % T == 0
    col = pl.BlockSpec((1, C_WIDTH, T), lambda b, i: (b, 0, i))
    row = pl.BlockSpec((1, T, D_MODEL), lambda b, i: (b, i, 0))
    return pl.pallas_call(
        _gate_out_t_kernel,
        grid=(B, L // T),
        in_specs=[col, col, row, pl.BlockSpec((D_MODEL, C_WIDTH), lambda b, i: (0, 0))],
        out_specs=row,
        out_shape=jax.ShapeDtypeStruct((B, L, D_MODEL), F32),
        compiler_params=_cparams(2),
        name="gate_out_prompt",
    )(att_t, gate_t, x, wout_t)


def _fox_proj_sample_kernel(x_ref, g_ref, wt_ref, bf_ref, qg_ref, kg_ref, seg_ref, exp_ref,
                            q_ref, k_ref, v_ref, gate_ref, lf_ref):
    xn = _rms_rows(x_ref[...], g_ref[...]).astype(BF16)

    def head_norm(t, gain):
        ssq = _dot((t * t).astype(BF16), seg_ref[...])
        rs = lax.rsqrt(ssq * (1.0 / C_HEAD_DIM) + EPS)
        hi, lo = _split_bf16(rs)
        rs_full = _dot(hi, exp_ref[...]) + _dot(lo, exp_ref[...])
        return t * rs_full * gain

    def proj(k):
        return _dot_nt(xn, wt_ref[k * C_WIDTH:(k + 1) * C_WIDTH, :])

    q_ref[...] = head_norm(proj(0), qg_ref[...]) * ATTN_SCALE
    k_ref[...] = head_norm(proj(1), kg_ref[...])
    v_ref[...] = proj(2)
    gate_ref[...] = proj(3)
    wf_t = wt_ref[4 * C_WIDTH:4 * C_WIDTH + C_HEADS, :]
    lf_ref[...] = jax.nn.log_sigmoid(_dot_nt(xn, wf_t) + bf_ref[...])


def _fox_proj_sample(x, w):
    n = x.shape[0]
    out_shape = [
        jax.ShapeDtypeStruct((n, C_WIDTH), F32),
        jax.ShapeDtypeStruct((n, C_WIDTH), F32),
        jax.ShapeDtypeStruct((n, C_WIDTH), F32),
        jax.ShapeDtypeStruct((n, C_WIDTH), F32),
        jax.ShapeDtypeStruct((n, C_HEADS), F32),
    ]
    return pl.pallas_call(
        _fox_proj_sample_kernel,
        out_shape=out_shape,
        compiler_params=pltpu.CompilerParams(vmem_limit_bytes=VMEM_LIMIT),
        name="fox_proj_sample",
    )(x, w["norm1_g"], w["win1_t"], w["bf_row"], w["qg"], w["kg"], w["seg"], w["exp"])


def _sublane_total(x):
    acc = x[0:SUBLANES]
    for r in range(1, x.shape[0] // SUBLANES):
        acc = acc + x[r * SUBLANES:(r + 1) * SUBLANES]
    shift = SUBLANES // 2
    while shift >= 1:
        acc = acc + pltpu.roll(acc, shift, axis=0)
        shift //= 2
    return acc


class _DecodeRefs:
    def __init__(self, pt, qcol, kcol, vnew, lfn, kcache, vcache, lfcache, out,
                 pages, lfpages, page_sem, lf_sem, qb, s_scr, r_scr, ps_scr, l_scr, acc):
        self.pt, self.qcol, self.kcol, self.vnew, self.lfn = pt, qcol, kcol, vnew, lfn
        self.kcache, self.vcache, self.lfcache, self.out = kcache, vcache, lfcache, out
        self.pages, self.lfpages, self.page_sem, self.lf_sem = pages, lfpages, page_sem, lf_sem
        self.qb, self.s_scr, self.r_scr, self.ps_scr, self.l_scr, self.acc = (
            qb, s_scr, r_scr, ps_scr, l_scr, acc)
        self.n_pages = pt.shape[1]
        self.n_units = self.n_pages // PAGES_PER_STEP


def _decode_copies(d, batch, phase_is_v, unit, slot):
    src = d.vcache if phase_is_v else d.kcache
    copies = []
    for i in range(PAGES_PER_STEP):
        page = d.pt[batch, d.n_pages - 1 - (unit * PAGES_PER_STEP + i)]
        copies.append(pltpu.make_async_copy(src.at[page], d.pages.at[slot, i], d.page_sem.at[slot]))
        if not phase_is_v:
            copies.append(pltpu.make_async_copy(d.lfcache.at[page], d.lfpages.at[slot, i],
                                                d.lf_sem.at[slot]))
    return copies


def _decode_dma(d, batch, phase_is_v, unit, slot, start):
    for c in _decode_copies(d, batch, phase_is_v, unit, slot):
        if start:
            c.start()
        else:
            c.wait()


def _decode_k_unit(d, unit, slot):
    PP = PAGES_PER_STEP

    @pl.when(unit == 0)
    def _():
        qc = d.qcol[0]
        for h in range(C_HEADS):
            d.qb[h] = jnp.broadcast_to(qc[:, h:h + 1], (C_HEAD_DIM, PAGE_SIZE))
        d.r_scr[...] = jnp.broadcast_to(d.lfn[0], (C_HEADS, PAGE_SIZE))

    slot_i = lax.broadcasted_iota(jnp.int32, (PAGE_SIZE, 2 * PAGE_SIZE), 0)
    slot_j = lax.broadcasted_iota(jnp.int32, (PAGE_SIZE, 2 * PAGE_SIZE), 1)
    sum_mat = jnp.where((slot_j >= PAGE_SIZE) | (slot_i > slot_j), 1.0, 0.0).astype(BF16)
    lf_all = jnp.concatenate([d.lfpages[slot, i] for i in range(PP)], axis=0)
    hi, lo = _split_bf16(lf_all)
    sums = _dot(hi, sum_mat) + _dot(lo, sum_mat)

    sub = lax.broadcasted_iota(jnp.int32, (SUBLANES, PAGE_SIZE), 0)
    groups = C_HEADS // SUBLANES
    s_parts = [[jnp.zeros((SUBLANES, PAGE_SIZE), F32) for _ in range(groups)] for _ in range(PP)]
    for h in range(C_HEADS):
        qh = d.qb[h]
        for i in range(PP):
            tot = _sublane_total(d.pages[slot, i, h] * qh)
            g = h // SUBLANES
            s_parts[i][g] = jnp.where(sub == h % SUBLANES, tot, s_parts[i][g])
    r = d.r_scr[...]
    for i in range(PP):
        s = jnp.concatenate(s_parts[i], axis=0)
        page = sums[i * C_HEADS:(i + 1) * C_HEADS]
        d.s_scr[unit * PP + i] = s + page[:, 0:PAGE_SIZE] + r
        r = r + page[:, PAGE_SIZE:]
    d.r_scr[...] = r


def _decode_v_unit(d, unit, slot):
    PP = PAGES_PER_STEP

    @pl.when(unit == 0)
    def _():
        eye = (lax.broadcasted_iota(jnp.int32, (C_HEADS, C_HEADS), 0)
               == lax.broadcasted_iota(jnp.int32, (C_HEADS, C_HEADS), 1))
        self_row = jnp.sum(d.qcol[0] * d.kcol[0], axis=0, keepdims=True)
        s_self = jnp.sum(jnp.where(eye, self_row, 0.0), axis=1, keepdims=True)
        s_all = d.s_scr[...]
        m = jnp.max(jnp.max(s_all, axis=0), axis=1, keepdims=True)
        m = jnp.maximum(m, s_self)
        p_all = jnp.exp(s_all - m)
        d.s_scr[...] = p_all
        p_self = jnp.exp(s_self - m)
        d.ps_scr[...] = p_self
        d.l_scr[...] = jnp.sum(jnp.sum(p_all, axis=0), axis=1, keepdims=True) + p_self
        d.acc[...] = jnp.zeros(d.acc.shape, F32)

    p_pages = [d.s_scr[unit * PP + i] for i in range(PP)]
    for h in range(C_HEADS):
        a = d.acc[h]
        for i in range(PP):
            a = a + p_pages[i][h:h + 1, :] * d.pages[slot, i, h]
        d.acc[h] = a

    @pl.when(unit == d.n_units - 1)
    def _():
        ones = jnp.ones((SUBLANES, PAGE_SIZE), BF16)
        rows = lax.broadcasted_iota(jnp.int32, (C_HEADS, C_HEAD_DIM), 0)
        out = jnp.zeros((C_HEADS, C_HEAD_DIM), F32)
        for h in range(C_HEADS):
            hi, lo = _split_bf16(d.acc[h])
            tot = _dot_nt(ones, hi) + _dot_nt(ones, lo)
            out = jnp.where(rows == h, tot[0:1, :], out)
        d.out[0] = (out + d.ps_scr[...] * d.vnew[0]) / d.l_scr[...]


def _decode_unit(d, step, n_steps, unit):
    batch = step // 2
    ahead = DECODE_SLOTS
    slot = unit % ahead
    for phase_is_v in (False, True):
        @pl.when(step % 2 == int(phase_is_v))
        def _():
            _decode_dma(d, batch, phase_is_v, unit, slot, start=False)
            if phase_is_v:
                _decode_v_unit(d, unit, slot)
            else:
                _decode_k_unit(d, unit, slot)

            @pl.when(unit + ahead < d.n_units)
            def _():
                _decode_dma(d, batch, phase_is_v, unit + ahead, slot, start=True)

            @pl.when((unit + ahead >= d.n_units) & (step < n_steps - 1))
            def _():
                _decode_dma(d, (step + 1) // 2, not phase_is_v, unit + ahead - d.n_units, slot, start=True)


def _gate_out_kernel(att_ref, gate_ref, x_ref, w_ref, y_ref):
    mixed = (att_ref[...] * jax.nn.silu(gate_ref[...])).astype(BF16)
    y_ref[...] = _dot(mixed, w_ref[...]) + x_ref[...]


def _gate_out(att, gate, x, wout):
    return pl.pallas_call(
        _gate_out_kernel,
        out_shape=jax.ShapeDtypeStruct(x.shape, F32),
        compiler_params=pltpu.CompilerParams(vmem_limit_bytes=VMEM_LIMIT),
        name="gate_out_sample",
    )(att, gate, x, wout)


def _block_diag(w):
    nb, d, _ = w.shape
    per = nb // GATE_GROUPS
    wg = w.reshape(GATE_GROUPS, per, d, d)
    eye = jnp.eye(per, dtype=w.dtype)
    return (eye[None, :, None, :, None] * wg[:, :, :, None, :]).reshape(GATE_GROUPS, per * d, per * d)


def _prepare_weights(norm0_g, w_in0, gmlp_v_g, gmlp_w_s, gmlp_b_s, lru_conv_w, lru_conv_b,
                     lru_w_r, lru_b_r, lru_w_i, lru_b_i, lru_lambda, w_out0, norm1_g, w_in1,
                     fox_b_f, q_norm_g, k_norm_g, w_out1):
    lane = np.arange(C_WIDTH) // C_HEAD_DIM
    seg = (lane[:, None] == np.arange(LANES)[None, :]).astype(np.float32)
    win1_t = w_in1.T.astype(BF16)
    wout1 = w_out1.astype(BF16)
    return {
        "norm0_g": norm0_g.reshape(1, D_MODEL),
        "win0": w_in0.astype(BF16),
        "vg": gmlp_v_g.reshape(1, A_WIDTH),
        "ws": gmlp_w_s,
        "bst": gmlp_b_s.T,
        "cw": lru_conv_w,
        "cb": lru_conv_b.reshape(1, B_WIDTH),
        "wr": _block_diag(lru_w_r).astype(BF16),
        "br": lru_b_r.reshape(1, B_WIDTH),
        "wi": _block_diag(lru_w_i).astype(BF16),
        "bi": lru_b_i.reshape(1, B_WIDTH),
        "lam": lru_lambda.reshape(1, B_WIDTH),
        "wout0": w_out0.astype(BF16),
        "norm1_g": norm1_g.reshape(1, D_MODEL),
        "win1_t": win1_t,
        "bf_row": fox_b_f.reshape(1, C_HEADS),
        "bf_col": fox_b_f.reshape(C_HEADS, 1),
        "qg": jnp.tile(q_norm_g, C_HEADS).reshape(1, C_WIDTH),
        "kg": jnp.tile(k_norm_g, C_HEADS).reshape(1, C_WIDTH),
        "qg_col": q_norm_g.reshape(C_HEAD_DIM, 1),
        "kg_col": k_norm_g.reshape(C_HEAD_DIM, 1),
        "seg": jnp.asarray(seg, BF16),
        "exp": jnp.asarray(seg.T, BF16),
        "wout1": wout1,
        "wout1_t": wout1.T,
    }


def kernel(x_prompt, x_sample, state_lru_conv, state_lru_h, cache_k, cache_v, cache_logf, page_table, norm0_g, w_in0, gmlp_v_g, gmlp_w_s, gmlp_b_s, lru_conv_w, lru_conv_b, lru_w_r, lru_b_r, lru_w_i, lru_b_i, lru_lambda, w_out0, norm1_g, w_in1, fox_b_f, q_norm_g, k_norm_g, w_out1):
    Bp, L, _ = x_prompt.shape
    Bs = x_sample.shape[0]
    w = _prepare_weights(norm0_g, w_in0, gmlp_v_g, gmlp_w_s, gmlp_b_s, lru_conv_w, lru_conv_b,
                         lru_w_r, lru_b_r, lru_w_i, lru_b_i, lru_lambda, w_out0, norm1_g, w_in1,
                         fox_b_f, q_norm_g, k_norm_g, w_out1)

    conv0 = jnp.zeros((Bp, CONV_W - 1, B_WIDTH), F32)
    h0 = jnp.zeros((Bp, B_WIDTH), F32)
    yp0, lru_conv_p, lru_h_p = _layer0_prompt(x_prompt, conv0, h0, w)
    ys0, gmlp_v_s, conv_s, lru_h_s = _layer0_sample(
        x_sample.reshape(Bs, D_MODEL), state_lru_conv, state_lru_h, w)

    qaug, kaug, kt_p, vt_p, gt_p, lft_p = _fox_proj_prompt(yp0, w)
    q_s, k_s, v_s, g_s, logf_s = _fox_proj_sample(ys0, w)
    att_t, att_s = _fox_attention(qaug, kaug, vt_p, q_s, k_s, v_s, logf_s,
                                  cache_k, cache_v, cache_logf, page_table)
    yp = _gate_out_t(att_t.reshape(Bp, C_WIDTH, L), gt_p, yp0, w["wout1_t"])
    k_p = jnp.transpose(kt_p, (0, 3, 1, 2))
    v_p = jnp.transpose(vt_p, (0, 3, 1, 2))
    logf_p = jnp.transpose(lft_p, (0, 2, 1))
    ys = _gate_out(att_s, g_s, ys0, w["wout1"])

    return (yp, ys.reshape(Bs, 1, D_MODEL), lru_conv_p, lru_h_p, k_p, v_p, logf_p,
            gmlp_v_s.reshape(Bs, 1, A_WIDTH), conv_s.reshape(Bs, CONV_W - 1, B_WIDTH), lru_h_s,
            k_s.reshape(Bs, 1, C_HEADS, C_HEAD_DIM), v_s.reshape(Bs, 1, C_HEADS, C_HEAD_DIM),
            logf_s.reshape(Bs, 1, C_HEADS))
```

```python
import jax
import jax.numpy as jnp
import numpy as np
from jax import lax
from jax.experimental import pallas as pl
from jax.experimental.pallas import tpu as pltpu

D_MODEL = 1024
A_WIDTH = 512
A_GROUPS = 4
A_GROUP_DIM = 128
CHUNK = 128
B_WIDTH = 512
B_BLOCKS = 8
B_BLOCK_DIM = 64
CONV_W = 4
LRU_C = 8.0
C_HEADS = 16
C_HEAD_DIM = 64
C_WIDTH = 1024
PAGE_SIZE = 128
ATTN_SCALE = C_HEAD_DIM ** -0.5
EPS = 1e-6

LANES = 128
SUBLANES = 8
NEG_BIG = -1e30

F32 = jnp.float32
BF16 = jnp.bfloat16

L0_TILE = 1024
PROJ_TILE = 512
ATT_T = 512
OUT_TILE = 1024
PAGES_PER_STEP = 16
DECODE_SLOTS = 2
GATE_GROUPS = 2
ATT_BUFFERS = 4
MAX_SLABS = 8
AUG_ROWS = 128
V_AUG_ROWS = 80
LOG2E = 1.4426950408889634
N_BIAS = 3
VMEM_LIMIT = 56 * 1024 * 1024


def _cparams(n_grid_dims):
    return pltpu.CompilerParams(
        dimension_semantics=("arbitrary",) * n_grid_dims,
        vmem_limit_bytes=VMEM_LIMIT,
    )


def _dot(a, b):
    return jnp.dot(a, b, preferred_element_type=F32)


def _dot_nt(a, b):
    return lax.dot_general(a, b, (((1,), (1,)), ((), ())), preferred_element_type=F32)


def _split_bf16(x):
    hi = x.astype(BF16)
    lo = (x - hi.astype(F32)).astype(BF16)
    return hi, lo


def _rms_rows(x, gain):
    ms = jnp.mean(x * x, axis=-1, keepdims=True)
    return x * lax.rsqrt(ms + EPS) * gain


def _gelu_tanh(x):
    c1 = 0.7978845608028654
    hx = 0.5 * x
    return hx + hx * jnp.tanh(x * (c1 + (c1 * 0.044715) * (x * x)))


def _gmlp_v_rows(pv, vg_ref):
    v = _gelu_tanh(pv)
    parts = []
    for g in range(A_GROUPS):
        sl = slice(g * A_GROUP_DIM, (g + 1) * A_GROUP_DIM)
        parts.append(_rms_rows(v[:, sl], vg_ref[:, sl]))
    return jnp.concatenate(parts, axis=-1)


def _lru_gates(xc, wr_ref, br_ref, wi_ref, bi_ref, lam_ref):
    xcb = xc.astype(BF16)
    def block_gate(w_ref, b_ref):
        gw = B_WIDTH // GATE_GROUPS
        parts = [_dot(xcb[:, g * gw:(g + 1) * gw], w_ref[g]) for g in range(GATE_GROUPS)]
        return jax.nn.sigmoid(jnp.concatenate(parts, axis=1) + b_ref[...])

    r = block_gate(wr_ref, br_ref)
    gi = block_gate(wi_ref, bi_ref)
    log_a = -LRU_C * r * jax.nn.softplus(-lam_ref[...])
    a = jnp.exp(log_a)
    bterm = jnp.sqrt(-jnp.tanh(log_a) * (1.0 + a * a)) * (gi * xc)
    return a, bterm


def _layer0_prompt_kernel(x_ref, g_ref, win_ref, vg_ref, ws_ref, bst_ref, cw_ref, cb_ref,
                          wr_ref, br_ref, wi_ref, bi_ref, lam_ref, wout_ref, conv0_ref, h0_ref,
                          y_ref, convo_ref, ho_ref,
                          xbuf, hcar, a8, b8, s_scr):
    T = L0_TILE
    i = pl.program_id(1)

    @pl.when(i == 0)
    def _():
        xbuf[0:SUBLANES, :] = jnp.zeros((SUBLANES, B_WIDTH), F32)
        xbuf[SUBLANES - (CONV_W - 1):SUBLANES, :] = conv0_ref[0]
        hcar[...] = h0_ref[0]

    x = x_ref[0]
    xn = _rms_rows(x, g_ref[...]).astype(BF16)

    def proj(k):
        return _dot(xn, win_ref[:, k * 512:(k + 1) * 512])

    vn = _gmlp_v_rows(proj(1), vg_ref).astype(BF16)
    tri = (lax.broadcasted_iota(jnp.int32, (CHUNK, CHUNK), 0)
           >= lax.broadcasted_iota(jnp.int32, (CHUNK, CHUNK), 1))
    for g in range(A_GROUPS):
        wg = jnp.where(tri, ws_ref[g], 0.0).astype(BF16)
        bias = bst_ref[:, g:g + 1]
        for c in range(T // CHUNK):
            blk = vn[c * CHUNK:(c + 1) * CHUNK, g * A_GROUP_DIM:(g + 1) * A_GROUP_DIM]
            s_scr[c * CHUNK:(c + 1) * CHUNK, g * A_GROUP_DIM:(g + 1) * A_GROUP_DIM] = (
                _dot(wg, blk) + bias)
    u = _gelu_tanh(proj(0))
    mix_a = (u * s_scr[...] * jax.nn.silu(proj(2))).astype(BF16)

    xb = proj(3)
    xbuf[SUBLANES:SUBLANES + T, :] = xb
    xc = cb_ref[...] + cw_ref[3:4, :] * xb
    for tap in range(CONV_W - 1):
        off = SUBLANES - (CONV_W - 1) + tap
        xc = xc + cw_ref[tap:tap + 1, :] * xbuf[off:off + T, :]
    convo_ref[0] = xbuf[T + SUBLANES - (CONV_W - 1):T + SUBLANES, :]
    xbuf[0:SUBLANES, :] = xbuf[T:T + SUBLANES, :]

    a, bt = _lru_gates(xc, wr_ref, br_ref, wi_ref, bi_ref, lam_ref)

    a = a.reshape(T // SUBLANES, SUBLANES, B_WIDTH)
    bt = bt.reshape(T // SUBLANES, SUBLANES, B_WIDTH)
    row = lax.broadcasted_iota(jnp.int32, (1, SUBLANES, B_WIDTH), 1)
    shift = 1
    while shift < SUBLANES:
        a_sh = pltpu.roll(a, shift, axis=1)
        b_sh = pltpu.roll(bt, shift, axis=1)
        m = row >= shift
        bt = jnp.where(m, a * b_sh + bt, bt)
        a = jnp.where(m, a * a_sh, a)
        shift *= 2
    a8[...] = a.reshape(T, B_WIDTH)
    b8[...] = bt.reshape(T, B_WIDTH)

    def group_step(j, h):
        off = pl.multiple_of(j * SUBLANES, SUBLANES)
        rows = a8[pl.ds(off, SUBLANES), :] * h + b8[pl.ds(off, SUBLANES), :]
        b8[pl.ds(off, SUBLANES), :] = rows
        return rows[SUBLANES - 1:SUBLANES, :]

    h_last = lax.fori_loop(0, T // SUBLANES, group_step, hcar[...], unroll=True)
    hcar[...] = h_last
    ho_ref[0] = h_last

    mix_b = (b8[...] * jax.nn.silu(proj(4))).astype(BF16)

    y = _dot(mix_a, wout_ref[0:A_WIDTH, :]) + _dot(mix_b, wout_ref[A_WIDTH:, :]) + x
    y_ref[0] = y


def _layer0_prompt(x, conv0, h0, w):
    B, L, _ = x.shape
    T = L0_TILE
    assert L % T == 0 and T % CHUNK == 0
    nt = L // T
    full = lambda shape: pl.BlockSpec(shape, lambda b, i: (0,) * len(shape))
    in_specs = [
        pl.BlockSpec((1, T, D_MODEL), lambda b, i: (b, i, 0)),
        full((1, D_MODEL)),
        full(w["win0"].shape),
        full((1, A_WIDTH)),
        full((A_GROUPS, CHUNK, CHUNK)),
        full((CHUNK, A_GROUPS)),
        full((CONV_W, B_WIDTH)),
        full((1, B_WIDTH)),
        full((GATE_GROUPS, B_WIDTH // GATE_GROUPS, B_WIDTH // GATE_GROUPS)), full((1, B_WIDTH)),
        full((GATE_GROUPS, B_WIDTH // GATE_GROUPS, B_WIDTH // GATE_GROUPS)), full((1, B_WIDTH)),
        full((1, B_WIDTH)),
        full((D_MODEL, D_MODEL)),
        pl.BlockSpec((1, CONV_W - 1, B_WIDTH), lambda b, i: (b, 0, 0)),
        pl.BlockSpec((1, 1, B_WIDTH), lambda b, i: (b, 0, 0)),
    ]
    out_specs = [
        pl.BlockSpec((1, T, D_MODEL), lambda b, i: (b, i, 0)),
        pl.BlockSpec((1, CONV_W - 1, B_WIDTH), lambda b, i: (b, 0, 0)),
        pl.BlockSpec((1, 1, B_WIDTH), lambda b, i: (b, 0, 0)),
    ]
    out_shape = [
        jax.ShapeDtypeStruct((B, L, D_MODEL), F32),
        jax.ShapeDtypeStruct((B, CONV_W - 1, B_WIDTH), F32),
        jax.ShapeDtypeStruct((B, 1, B_WIDTH), F32),
    ]
    y, convo, ho = pl.pallas_call(
        _layer0_prompt_kernel,
        grid=(B, nt),
        in_specs=in_specs,
        out_specs=out_specs,
        out_shape=out_shape,
        scratch_shapes=[
            pltpu.VMEM((T + SUBLANES, B_WIDTH), F32),
            pltpu.VMEM((1, B_WIDTH), F32),
            pltpu.VMEM((T, B_WIDTH), F32),
            pltpu.VMEM((T, B_WIDTH), F32),
            pltpu.VMEM((T, A_WIDTH), F32),
        ],
        compiler_params=_cparams(2),
        name="layer0_prompt",
    )(x, w["norm0_g"], w["win0"], w["vg"], w["ws"], w["bst"], w["cw"], w["cb"],
      w["wr"], w["br"], w["wi"], w["bi"], w["lam"], w["wout0"], conv0, h0.reshape(B, 1, B_WIDTH))
    return y, convo, ho.reshape(B, B_WIDTH)


def _layer0_sample_kernel(x_ref, g_ref, win_ref, vg_ref, ws_ref, bst_ref, cw_ref, cb_ref,
                          wr_ref, br_ref, wi_ref, bi_ref, lam_ref, wout_ref, conv_ref, h_ref,
                          y_ref, v_ref, convo_ref, ho_ref):
    x = x_ref[...]
    xn = _rms_rows(x, g_ref[...]).astype(BF16)

    def proj(k):
        return _dot(xn, win_ref[:, k * 512:(k + 1) * 512])

    vn = _gmlp_v_rows(proj(1), vg_ref)
    v_ref[...] = vn
    s_parts = []
    for g in range(A_GROUPS):
        sl = slice(g * A_GROUP_DIM, (g + 1) * A_GROUP_DIM)
        s_parts.append(ws_ref[g, 0:1, 0:1] * vn[:, sl] + bst_ref[0:1, g:g + 1])
    s = jnp.concatenate(s_parts, axis=-1)
    mix_a = (_gelu_tanh(proj(0)) * s * jax.nn.silu(proj(2))).astype(BF16)

    xb = proj(3)
    xc = cb_ref[...] + cw_ref[3:4, :] * xb
    for tap in range(CONV_W - 1):
        xc = xc + cw_ref[tap:tap + 1, :] * conv_ref[:, tap * B_WIDTH:(tap + 1) * B_WIDTH]
    for tap in range(CONV_W - 2):
        convo_ref[:, tap * B_WIDTH:(tap + 1) * B_WIDTH] = (
            conv_ref[:, (tap + 1) * B_WIDTH:(tap + 2) * B_WIDTH])
    convo_ref[:, (CONV_W - 2) * B_WIDTH:] = xb

    a, bt = _lru_gates(xc, wr_ref, br_ref, wi_ref, bi_ref, lam_ref)
    h = a * h_ref[...] + bt
    ho_ref[...] = h
    mix_b = (h * jax.nn.silu(proj(4))).astype(BF16)
    y_ref[...] = _dot(mix_a, wout_ref[0:A_WIDTH, :]) + _dot(mix_b, wout_ref[A_WIDTH:, :]) + x


def _layer0_sample(x, conv, h, w):
    n = x.shape[0]
    out_shape = [
        jax.ShapeDtypeStruct((n, D_MODEL), F32),
        jax.ShapeDtypeStruct((n, A_WIDTH), F32),
        jax.ShapeDtypeStruct((n, (CONV_W - 1) * B_WIDTH), F32),
        jax.ShapeDtypeStruct((n, B_WIDTH), F32),
    ]
    return pl.pallas_call(
        _layer0_sample_kernel,
        out_shape=out_shape,
        compiler_params=pltpu.CompilerParams(vmem_limit_bytes=VMEM_LIMIT),
        name="layer0_sample",
    )(x, w["norm0_g"], w["win0"], w["vg"], w["ws"], w["bst"], w["cw"], w["cb"],
      w["wr"], w["br"], w["wi"], w["bi"], w["lam"], w["wout0"],
      conv.reshape(n, (CONV_W - 1) * B_WIDTH), h)


def _head_norm_cols(t, gain_col):
    ms = jnp.mean(t * t, axis=0, keepdims=True)
    return t * lax.rsqrt(ms + EPS) * gain_col


def _fox_proj_prompt_kernel(x_ref, g_ref, wt_ref, bf_ref, qg_ref, kg_ref,
                            qaug_ref, kaug_ref, kt_ref, vt_ref, gt_ref, lf_ref, carry):
    T = PROJ_TILE
    i = pl.program_id(1)

    @pl.when(i == 0)
    def _():
        carry[...] = jnp.zeros_like(carry)

    xn = _rms_rows(x_ref[0], g_ref[...])
    xnt = xn.T.astype(BF16)

    def proj_t(k):
        return _dot(wt_ref[k * C_WIDTH:(k + 1) * C_WIDTH, :], xnt)

    wf_t = wt_ref[4 * C_WIDTH:4 * C_WIDTH + C_HEADS, :]
    lf = jax.nn.log_sigmoid(_dot(wf_t, xnt) + bf_ref[...])
    lf_ref[0] = lf
    upper = (lax.broadcasted_iota(jnp.int32, (T, T), 0)
             <= lax.broadcasted_iota(jnp.int32, (T, T), 1)).astype(BF16)
    hi, lo = _split_bf16(lf)
    c = _dot(hi, upper) + _dot(lo, upper) + carry[...]
    carry[...] = c[:, T - 1:T]
    pieces, rest = [], c * (-LOG2E)
    for _ in range(N_BIAS):
        p = rest.astype(BF16).astype(F32)
        pieces.append(p)
        rest = rest - p

    sub = lax.broadcasted_iota(jnp.int32, (SUBLANES, T), 0)
    ones_rows = jnp.where(sub < N_BIAS, 1.0, 0.0)
    pad = jnp.zeros((AUG_ROWS - C_HEAD_DIM - SUBLANES, T), F32)
    qt = proj_t(0)
    kt = proj_t(1)
    for h in range(C_HEADS):
        sl = slice(h * C_HEAD_DIM, (h + 1) * C_HEAD_DIM)
        qn = _head_norm_cols(qt[sl], qg_ref[...]) * (ATTN_SCALE * LOG2E)
        kn = _head_norm_cols(kt[sl], kg_ref[...])
        kt_ref[0, h] = kn
        qaug_ref[0, h, 0] = jnp.concatenate([qn, ones_rows, pad], axis=0).astype(BF16)
        bias_rows = jnp.zeros((SUBLANES, T), F32)
        for j in range(N_BIAS):
            bias_rows = jnp.where(sub == j, pieces[j][h:h + 1, :], bias_rows)
        kaug_t = jnp.concatenate([kn, bias_rows, pad], axis=0)
        kaug_ref[0, h] = kaug_t.T.astype(BF16)
    vt_ref[0] = proj_t(2).reshape(C_HEADS, C_HEAD_DIM, T)
    gt_ref[0] = proj_t(3).astype(BF16)


def _fox_proj_prompt(x, w):
    B, L, _ = x.shape
    T = PROJ_TILE
    assert L % ATT_T == 0 and ATT_T % T == 0
    per_q = ATT_T // T
    full = lambda shape: pl.BlockSpec(shape, lambda b, i: (0,) * len(shape))
    in_specs = [
        pl.BlockSpec((1, T, D_MODEL), lambda b, i: (b, i, 0)),
        full((1, D_MODEL)), full((4 * C_WIDTH + C_HEADS, D_MODEL)),
        full((C_HEADS, 1)), full((C_HEAD_DIM, 1)), full((C_HEAD_DIM, 1)),
    ]
    out_specs = [
        pl.BlockSpec((1, C_HEADS, 1, AUG_ROWS, T), lambda b, i: (b, 0, i // per_q, 0, i % per_q)),
        pl.BlockSpec((1, C_HEADS, T, AUG_ROWS), lambda b, i: (b, 0, i, 0)),
        pl.BlockSpec((1, C_HEADS, C_HEAD_DIM, T), lambda b, i: (b, 0, 0, i)),
        pl.BlockSpec((1, C_HEADS, C_HEAD_DIM, T), lambda b, i: (b, 0, 0, i)),
        pl.BlockSpec((1, C_WIDTH, T), lambda b, i: (b, 0, i)),
        pl.BlockSpec((1, C_HEADS, T), lambda b, i: (b, 0, i)),
    ]
    out_shape = [
        jax.ShapeDtypeStruct((B, C_HEADS, L // ATT_T, AUG_ROWS, ATT_T), BF16),
        jax.ShapeDtypeStruct((B, C_HEADS, L, AUG_ROWS), BF16),
        jax.ShapeDtypeStruct((B, C_HEADS, C_HEAD_DIM, L), F32),
        jax.ShapeDtypeStruct((B, C_HEADS, C_HEAD_DIM, L), F32),
        jax.ShapeDtypeStruct((B, C_WIDTH, L), BF16),
        jax.ShapeDtypeStruct((B, C_HEADS, L), F32),
    ]
    return pl.pallas_call(
        _fox_proj_prompt_kernel,
        grid=(B, L // T),
        in_specs=in_specs,
        out_specs=out_specs,
        out_shape=out_shape,
        scratch_shapes=[pltpu.VMEM((C_HEADS, 1), F32)],
        compiler_params=_cparams(2),
        name="fox_proj_prompt",
    )(x, w["norm1_g"], w["win1_t"], w["bf_col"], w["qg_col"], w["kg_col"])


def _fox_attn_kernel(pt_ref,
                     qaug_ref, kaug_ref, vt_ref, qcol_ref, kcol_ref, vnew_ref, lfn_ref,
                     kcache, vcache, lfcache,
                     o_ref, dec_ref,
                     m_scr, acc_scr, vaug_scr, sblk_scr, mblk_scr, ablk_scr,
                     pages, lfpages, page_sem, lf_sem, qb, s_scr, r_scr, ps_scr, l_scr, dacc):
    nq = qaug_ref.shape[2]
    t = ATT_T
    blocks = [(k, q, False) for k in range(nq) for q in range(k + 1, nq)]
    blocks += [(k, k, True) for k in range(nq)]

    d = _DecodeRefs(pt_ref, qcol_ref, kcol_ref, vnew_ref, lfn_ref, kcache, vcache, lfcache, dec_ref,
                    pages, lfpages, page_sem, lf_sem, qb, s_scr, r_scr, ps_scr, l_scr, dacc)
    step = pl.program_id(0) * pl.num_programs(1) + pl.program_id(1)
    n_steps = pl.num_programs(0) * pl.num_programs(1)
    assert d.n_units % DECODE_SLOTS == 0

    @pl.when(step == 0)
    def _():
        for unit in range(DECODE_SLOTS):
            _decode_dma(d, 0, False, unit, unit, start=True)

    m_scr[...] = jnp.full(m_scr.shape, NEG_BIG, F32)
    acc_scr[...] = jnp.zeros(acc_scr.shape, F32)
    extra = lax.broadcasted_iota(jnp.int32, (V_AUG_ROWS - C_HEAD_DIM, t), 0)
    one_row = jnp.where(extra == 0, 1.0, 0.0)
    for kb in range(nq):
        vaug_scr[kb] = jnp.concatenate(
            [vt_ref[0, 0, :, kb * t:(kb + 1) * t], one_row], axis=0).astype(BF16)

    def pieces(masked):
        return [(t // 2, 0, t // 2), (t, t // 2, t)] if masked else [(t, 0, t)]

    def col_max(s):
        slab = jnp.max(s.reshape(MAX_SLABS, s.shape[0] // MAX_SLABS, s.shape[1]), axis=0)
        return jnp.max(slab, axis=0, keepdims=True)

    def start(block, buf):
        s_ref, m_ref, alpha_ref = buf
        ki, qi, masked = block
        maxes = []
        for nk, q0, q1 in pieces(masked):
            keys = kaug_ref[0, 0, ki * t:ki * t + nk, :]
            s = _dot(keys, qaug_ref[0, 0, qi, :, q0:q1])
            if masked:
                causal = (lax.broadcasted_iota(jnp.int32, s.shape, 0)
                          <= lax.broadcasted_iota(jnp.int32, s.shape, 1) + q0)
                s = jnp.where(causal, s, NEG_BIG)
            s_ref[0:nk, q0:q1] = s
            maxes.append(col_max(s))
        m_old = m_scr[qi]
        m_new = jnp.maximum(m_old, jnp.concatenate(maxes, axis=1))
        alpha_ref[...] = jnp.exp2(m_old - m_new)
        m_ref[...] = m_new
        m_scr[qi] = m_new

    def finish(block, buf):
        s_ref, m_ref, alpha_ref = buf
        ki, qi, masked = block
        outs = []
        for nk, q0, q1 in pieces(masked):
            p = jnp.exp2(s_ref[0:nk, q0:q1] - m_ref[:, q0:q1]).astype(BF16)
            outs.append(_dot(vaug_scr[ki, :, 0:nk], p))
        acc_scr[qi] = alpha_ref[...] * acc_scr[qi] + jnp.concatenate(outs, axis=1)

    bufs = [(sblk_scr.at[k], mblk_scr.at[k], ablk_scr.at[k]) for k in range(ATT_BUFFERS)]
    depth = ATT_BUFFERS - 1

    assert len(blocks) % d.n_units == 0
    blocks_per_unit = len(blocks) // d.n_units
    for j in range(depth):
        start(blocks[j], bufs[j])
    for i, block in enumerate(blocks):
        if i % blocks_per_unit == 0:
            _decode_unit(d, step, n_steps, i // blocks_per_unit)
        if i + depth < len(blocks):
            start(blocks[i + depth], bufs[(i + depth) % len(bufs)])
        finish(block, bufs[i % len(bufs)])

    for qi in range(nq):
        o_ref[0, 0, :, qi * t:(qi + 1) * t] = (
            acc_scr[qi, 0:C_HEAD_DIM, :] / acc_scr[qi, C_HEAD_DIM:C_HEAD_DIM + 1, :]).astype(BF16)


def _fox_attention(qaug, kaug, vt, q_s, k_s, v_s, lf_s, cache_k, cache_v, cache_logf, page_table):
    B, H, nq, _, t = qaug.shape
    L = nq * t
    n, n_pages = page_table.shape
    PP = PAGES_PER_STEP
    assert nq % 2 == 0 and n_pages % PP == 0
    assert B * H == 2 * n, "one decode (batch, phase) per prompt (batch, head) grid step"
    kt_cache = jnp.transpose(cache_k, (0, 2, 3, 1))
    vt_cache = jnp.transpose(cache_v, (0, 2, 3, 1))
    lft_cache = jnp.transpose(cache_logf, (0, 2, 1))
    to_cols = lambda a: jnp.transpose(a.reshape(n, C_HEADS, C_HEAD_DIM), (0, 2, 1))
    dec_block = lambda shape: pl.BlockSpec(
        (1,) + shape, lambda b, h, pt: ((b * H + h) // 2, 0, 0))
    hbm = pl.BlockSpec(memory_space=pl.ANY)
    grid_spec = pltpu.PrefetchScalarGridSpec(
        num_scalar_prefetch=1,
        grid=(B, H),
        in_specs=[
            pl.BlockSpec((1, 1, nq, AUG_ROWS, t), lambda b, h, pt: (b, h, 0, 0, 0)),
            pl.BlockSpec((1, 1, L, AUG_ROWS), lambda b, h, pt: (b, h, 0, 0)),
            pl.BlockSpec((1, 1, C_HEAD_DIM, L), lambda b, h, pt: (b, h, 0, 0)),
            dec_block((C_HEAD_DIM, C_HEADS)), dec_block((C_HEAD_DIM, C_HEADS)),
            dec_block((C_HEADS, C_HEAD_DIM)), dec_block((C_HEADS, 1)),
            hbm, hbm, hbm,
        ],
        out_specs=[
            pl.BlockSpec((1, 1, C_HEAD_DIM, L), lambda b, h, pt: (b, h, 0, 0)),
            dec_block((C_HEADS, C_HEAD_DIM)),
        ],
        scratch_shapes=[
            pltpu.VMEM((nq, 1, t), F32),
            pltpu.VMEM((nq, V_AUG_ROWS, t), F32),
            pltpu.VMEM((nq, V_AUG_ROWS, t), BF16),
            pltpu.VMEM((ATT_BUFFERS, t, t), F32),
            pltpu.VMEM((ATT_BUFFERS, 1, t), F32),
            pltpu.VMEM((ATT_BUFFERS, 1, t), F32),
            pltpu.VMEM((DECODE_SLOTS, PP, C_HEADS, C_HEAD_DIM, PAGE_SIZE), F32),
            pltpu.VMEM((DECODE_SLOTS, PP, C_HEADS, PAGE_SIZE), F32),
            pltpu.SemaphoreType.DMA((DECODE_SLOTS,)),
            pltpu.SemaphoreType.DMA((DECODE_SLOTS,)),
            pltpu.VMEM((C_HEADS, C_HEAD_DIM, PAGE_SIZE), F32),
            pltpu.VMEM((n_pages, C_HEADS, PAGE_SIZE), F32),
            pltpu.VMEM((C_HEADS, PAGE_SIZE), F32),
            pltpu.VMEM((C_HEADS, 1), F32),
            pltpu.VMEM((C_HEADS, 1), F32),
            pltpu.VMEM((C_HEADS, C_HEAD_DIM, PAGE_SIZE), F32),
        ],
    )
    att_t, att_s = pl.pallas_call(
        _fox_attn_kernel,
        grid_spec=grid_spec,
        out_shape=[jax.ShapeDtypeStruct((B, H, C_HEAD_DIM, L), BF16),
                   jax.ShapeDtypeStruct((n, C_HEADS, C_HEAD_DIM), F32)],
        compiler_params=_cparams(2),
        name="fox_attention",
    )(page_table, qaug, kaug, vt,
      to_cols(q_s), to_cols(k_s), v_s.reshape(n, C_HEADS, C_HEAD_DIM), lf_s.reshape(n, C_HEADS, 1),
      kt_cache, vt_cache, lft_cache)
    return att_t, att_s.reshape(n, C_WIDTH)


def _gate_out_t_kernel(att_ref, gate_ref, x_ref, wt_ref, y_ref):
    mixed = (att_ref[0].astype(F32) * jax.nn.silu(gate_ref[0].astype(F32))).astype(BF16)
    y_ref[0] = _dot(wt_ref[...], mixed).T + x_ref[0]


def _gate_out_t(att_t, gate_t, x, wout_t):
    B, L, _ = x.shape
    T = OUT_TILE
    assert L % T == 0
    col = pl.BlockSpec((1, C_WIDTH, T), lambda b, i: (b, 0, i))
    row = pl.BlockSpec((1, T, D_MODEL), lambda b, i: (b, i, 0))
    return pl.pallas_call(
        _gate_out_t_kernel,
        grid=(B, L // T),
        in_specs=[col, col, row, pl.BlockSpec((D_MODEL, C_WIDTH), lambda b, i: (0, 0))],
        out_specs=row,
        out_shape=jax.ShapeDtypeStruct((B, L, D_MODEL), F32),
        compiler_params=_cparams(2),
        name="gate_out_prompt",
    )(att_t, gate_t, x, wout_t)


def _fox_proj_sample_kernel(x_ref, g_ref, wt_ref, bf_ref, qg_ref, kg_ref, seg_ref, exp_ref,
                            q_ref, k_ref, v_ref, gate_ref, lf_ref):
    xn = _rms_rows(x_ref[...], g_ref[...]).astype(BF16)

    def head_norm(t, gain):
        ssq = _dot((t * t).astype(BF16), seg_ref[...])
        rs = lax.rsqrt(ssq * (1.0 / C_HEAD_DIM) + EPS)
        hi, lo = _split_bf16(rs)
        rs_full = _dot(hi, exp_ref[...]) + _dot(lo, exp_ref[...])
        return t * rs_full * gain

    def proj(k):
        return _dot_nt(xn, wt_ref[k * C_WIDTH:(k + 1) * C_WIDTH, :])

    q_ref[...] = head_norm(proj(0), qg_ref[...]) * ATTN_SCALE
    k_ref[...] = head_norm(proj(1), kg_ref[...])
    v_ref[...] = proj(2)
    gate_ref[...] = proj(3)
    wf_t = wt_ref[4 * C_WIDTH:4 * C_WIDTH + C_HEADS, :]
    lf_ref[...] = jax.nn.log_sigmoid(_dot_nt(xn, wf_t) + bf_ref[...])


def _fox_proj_sample(x, w):
    n = x.shape[0]
    out_shape = [
        jax.ShapeDtypeStruct((n, C_WIDTH), F32),
        jax.ShapeDtypeStruct((n, C_WIDTH), F32),
        jax.ShapeDtypeStruct((n, C_WIDTH), F32),
        jax.ShapeDtypeStruct((n, C_WIDTH), F32),
        jax.ShapeDtypeStruct((n, C_HEADS), F32),
    ]
    return pl.pallas_call(
        _fox_proj_sample_kernel,
        out_shape=out_shape,
        compiler_params=pltpu.CompilerParams(vmem_limit_bytes=VMEM_LIMIT),
        name="fox_proj_sample",
    )(x, w["norm1_g"], w["win1_t"], w["bf_row"], w["qg"], w["kg"], w["seg"], w["exp"])


def _sublane_total(x):
    acc = x[0:SUBLANES]
    for r in range(1, x.shape[0] // SUBLANES):
        acc = acc + x[r * SUBLANES:(r + 1) * SUBLANES]
    shift = SUBLANES // 2
    while shift >= 1:
        acc = acc + pltpu.roll(acc, shift, axis=0)
        shift //= 2
    return acc


class _DecodeRefs:
    def __init__(self, pt, qcol, kcol, vnew, lfn, kcache, vcache, lfcache, out,
                 pages, lfpages, page_sem, lf_sem, qb, s_scr, r_scr, ps_scr, l_scr, acc):
        self.pt, self.qcol, self.kcol, self.vnew, self.lfn = pt, qcol, kcol, vnew, lfn
        self.kcache, self.vcache, self.lfcache, self.out = kcache, vcache, lfcache, out
        self.pages, self.lfpages, self.page_sem, self.lf_sem = pages, lfpages, page_sem, lf_sem
        self.qb, self.s_scr, self.r_scr, self.ps_scr, self.l_scr, self.acc = (
            qb, s_scr, r_scr, ps_scr, l_scr, acc)
        self.n_pages = pt.shape[1]
        self.n_units = self.n_pages // PAGES_PER_STEP


def _decode_copies(d, batch, phase_is_v, unit, slot):
    src = d.vcache if phase_is_v else d.kcache
    copies = []
    for i in range(PAGES_PER_STEP):
        page = d.pt[batch, d.n_pages - 1 - (unit * PAGES_PER_STEP + i)]
        copies.append(pltpu.make_async_copy(src.at[page], d.pages.at[slot, i], d.page_sem.at[slot]))
        if not phase_is_v:
            copies.append(pltpu.make_async_copy(d.lfcache.at[page], d.lfpages.at[slot, i],
                                                d.lf_sem.at[slot]))
    return copies


def _decode_dma(d, batch, phase_is_v, unit, slot, start):
    for c in _decode_copies(d, batch, phase_is_v, unit, slot):
        if start:
            c.start()
        else:
            c.wait()


def _decode_k_unit(d, unit, slot):
    PP = PAGES_PER_STEP

    @pl.when(unit == 0)
    def _():
        qc = d.qcol[0]
        for h in range(C_HEADS):
            d.qb[h] = jnp.broadcast_to(qc[:, h:h + 1], (C_HEAD_DIM, PAGE_SIZE))
        d.r_scr[...] = jnp.broadcast_to(d.lfn[0], (C_HEADS, PAGE_SIZE))

    slot_i = lax.broadcasted_iota(jnp.int32, (PAGE_SIZE, 2 * PAGE_SIZE), 0)
    slot_j = lax.broadcasted_iota(jnp.int32, (PAGE_SIZE, 2 * PAGE_SIZE), 1)
    sum_mat = jnp.where((slot_j >= PAGE_SIZE) | (slot_i > slot_j), 1.0, 0.0).astype(BF16)
    lf_all = jnp.concatenate([d.lfpages[slot, i] for i in range(PP)], axis=0)
    hi, lo = _split_bf16(lf_all)
    sums = _dot(hi, sum_mat) + _dot(lo, sum_mat)

    sub = lax.broadcasted_iota(jnp.int32, (SUBLANES, PAGE_SIZE), 0)
    groups = C_HEADS // SUBLANES
    s_parts = [[jnp.zeros((SUBLANES, PAGE_SIZE), F32) for _ in range(groups)] for _ in range(PP)]
    for h in range(C_HEADS):
        qh = d.qb[h]
        for i in range(PP):
            tot = _sublane_total(d.pages[slot, i, h] * qh)
            g = h // SUBLANES
            s_parts[i][g] = jnp.where(sub == h % SUBLANES, tot, s_parts[i][g])
    r = d.r_scr[...]
    for i in range(PP):
        s = jnp.concatenate(s_parts[i], axis=0)
        page = sums[i * C_HEADS:(i + 1) * C_HEADS]
        d.s_scr[unit * PP + i] = s + page[:, 0:PAGE_SIZE] + r
        r = r + page[:, PAGE_SIZE:]
    d.r_scr[...] = r


def _decode_v_unit(d, unit, slot):
    PP = PAGES_PER_STEP

    @pl.when(unit == 0)
    def _():
        eye = (lax.broadcasted_iota(jnp.int32, (C_HEADS, C_HEADS), 0)
               == lax.broadcasted_iota(jnp.int32, (C_HEADS, C_HEADS), 1))
        self_row = jnp.sum(d.qcol[0] * d.kcol[0], axis=0, keepdims=True)
        s_self = jnp.sum(jnp.where(eye, self_row, 0.0), axis=1, keepdims=True)
        s_all = d.s_scr[...]
        m = jnp.max(jnp.max(s_all, axis=0), axis=1, keepdims=True)
        m = jnp.maximum(m, s_self)
        p_all = jnp.exp(s_all - m)
        d.s_scr[...] = p_all
        p_self = jnp.exp(s_self - m)
        d.ps_scr[...] = p_self
        d.l_scr[...] = jnp.sum(jnp.sum(p_all, axis=0), axis=1, keepdims=True) + p_self
        d.acc[...] = jnp.zeros(d.acc.shape, F32)

    p_pages = [d.s_scr[unit * PP + i] for i in range(PP)]
    for h in range(C_HEADS):
        a = d.acc[h]
        for i in range(PP):
            a = a + p_pages[i][h:h + 1, :] * d.pages[slot, i, h]
        d.acc[h] = a

    @pl.when(unit == d.n_units - 1)
    def _():
        ones = jnp.ones((SUBLANES, PAGE_SIZE), BF16)
        rows = lax.broadcasted_iota(jnp.int32, (C_HEADS, C_HEAD_DIM), 0)
        out = jnp.zeros((C_HEADS, C_HEAD_DIM), F32)
        for h in range(C_HEADS):
            hi, lo = _split_bf16(d.acc[h])
            tot = _dot_nt(ones, hi) + _dot_nt(ones, lo)
            out = jnp.where(rows == h, tot[0:1, :], out)
        d.out[0] = (out + d.ps_scr[...] * d.vnew[0]) / d.l_scr[...]


def _decode_unit(d, step, n_steps, unit):
    batch = step // 2
    ahead = DECODE_SLOTS
    slot = unit % ahead
    for phase_is_v in (False, True):
        @pl.when(step % 2 == int(phase_is_v))
        def _():
            _decode_dma(d, batch, phase_is_v, unit, slot, start=False)
            if phase_is_v:
                _decode_v_unit(d, unit, slot)
            else:
                _decode_k_unit(d, unit, slot)

            @pl.when(unit + ahead < d.n_units)
            def _():
                _decode_dma(d, batch, phase_is_v, unit + ahead, slot, start=True)

            @pl.when((unit + ahead >= d.n_units) & (step < n_steps - 1))
            def _():
                _decode_dma(d, (step + 1) // 2, not phase_is_v, unit + ahead - d.n_units, slot, start=True)


def _gate_out_kernel(att_ref, gate_ref, x_ref, w_ref, y_ref):
    mixed = (att_ref[...] * jax.nn.silu(gate_ref[...])).astype(BF16)
    y_ref[...] = _dot(mixed, w_ref[...]) + x_ref[...]


def _gate_out(att, gate, x, wout):
    return pl.pallas_call(
        _gate_out_kernel,
        out_shape=jax.ShapeDtypeStruct(x.shape, F32),
        compiler_params=pltpu.CompilerParams(vmem_limit_bytes=VMEM_LIMIT),
        name="gate_out_sample",
    )(att, gate, x, wout)


def _block_diag(w):
    nb, d, _ = w.shape
    per = nb // GATE_GROUPS
    wg = w.reshape(GATE_GROUPS, per, d, d)
    eye = jnp.eye(per, dtype=w.dtype)
    return (eye[None, :, None, :, None] * wg[:, :, :, None, :]).reshape(GATE_GROUPS, per * d, per * d)


def _prepare_weights(norm0_g, w_in0, gmlp_v_g, gmlp_w_s, gmlp_b_s, lru_conv_w, lru_conv_b,
                     lru_w_r, lru_b_r, lru_w_i, lru_b_i, lru_lambda, w_out0, norm1_g, w_in1,
                     fox_b_f, q_norm_g, k_norm_g, w_out1):
    lane = np.arange(C_WIDTH) // C_HEAD_DIM
    seg = (lane[:, None] == np.arange(LANES)[None, :]).astype(np.float32)
    win1_t = w_in1.T.astype(BF16)
    wout1 = w_out1.astype(BF16)
    return {
        "norm0_g": norm0_g.reshape(1, D_MODEL),
        "win0": w_in0.astype(BF16),
        "vg": gmlp_v_g.reshape(1, A_WIDTH),
        "ws": gmlp_w_s,
        "bst": gmlp_b_s.T,
        "cw": lru_conv_w,
        "cb": lru_conv_b.reshape(1, B_WIDTH),
        "wr": _block_diag(lru_w_r).astype(BF16),
        "br": lru_b_r.reshape(1, B_WIDTH),
        "wi": _block_diag(lru_w_i).astype(BF16),
        "bi": lru_b_i.reshape(1, B_WIDTH),
        "lam": lru_lambda.reshape(1, B_WIDTH),
        "wout0": w_out0.astype(BF16),
        "norm1_g": norm1_g.reshape(1, D_MODEL),
        "win1_t": win1_t,
        "bf_row": fox_b_f.reshape(1, C_HEADS),
        "bf_col": fox_b_f.reshape(C_HEADS, 1),
        "qg": jnp.tile(q_norm_g, C_HEADS).reshape(1, C_WIDTH),
        "kg": jnp.tile(k_norm_g, C_HEADS).reshape(1, C_WIDTH),
        "qg_col": q_norm_g.reshape(C_HEAD_DIM, 1),
        "kg_col": k_norm_g.reshape(C_HEAD_DIM, 1),
        "seg": jnp.asarray(seg, BF16),
        "exp": jnp.asarray(seg.T, BF16),
        "wout1": wout1,
        "wout1_t": wout1.T,
    }


def kernel(x_prompt, x_sample, state_lru_conv, state_lru_h, cache_k, cache_v, cache_logf, page_table, norm0_g, w_in0, gmlp_v_g, gmlp_w_s, gmlp_b_s, lru_conv_w, lru_conv_b, lru_w_r, lru_b_r, lru_w_i, lru_b_i, lru_lambda, w_out0, norm1_g, w_in1, fox_b_f, q_norm_g, k_norm_g, w_out1):
    Bp, L, _ = x_prompt.shape
    Bs = x_sample.shape[0]
    w = _prepare_weights(norm0_g, w_in0, gmlp_v_g, gmlp_w_s, gmlp_b_s, lru_conv_w, lru_conv_b,
                         lru_w_r, lru_b_r, lru_w_i, lru_b_i, lru_lambda, w_out0, norm1_g, w_in1,
                         fox_b_f, q_norm_g, k_norm_g, w_out1)

    conv0 = jnp.zeros((Bp, CONV_W - 1, B_WIDTH), F32)
    h0 = jnp.zeros((Bp, B_WIDTH), F32)
    yp0, lru_conv_p, lru_h_p = _layer0_prompt(x_prompt, conv0, h0, w)
    ys0, gmlp_v_s, conv_s, lru_h_s = _layer0_sample(
        x_sample.reshape(Bs, D_MODEL), state_lru_conv, state_lru_h, w)

    qaug, kaug, kt_p, vt_p, gt_p, lft_p = _fox_proj_prompt(yp0, w)
    q_s, k_s, v_s, g_s, logf_s = _fox_proj_sample(ys0, w)
    att_t, att_s = _fox_attention(qaug, kaug, vt_p, q_s, k_s, v_s, logf_s,
                                  cache_k, cache_v, cache_logf, page_table)
    yp = _gate_out_t(att_t.reshape(Bp, C_WIDTH, L), gt_p, yp0, w["wout1_t"])
    k_p = jnp.transpose(kt_p, (0, 3, 1, 2))
    v_p = jnp.transpose(vt_p, (0, 3, 1, 2))
    logf_p = jnp.transpose(lft_p, (0, 2, 1))
    ys = _gate_out(att_s, g_s, ys0, w["wout1"])

    return (yp, ys.reshape(Bs, 1, D_MODEL), lru_conv_p, lru_h_p, k_p, v_p, logf_p,
            gmlp_v_s.reshape(Bs, 1, A_WIDTH), conv_s.reshape(Bs, CONV_W - 1, B_WIDTH), lru_h_s,
            k_s.reshape(Bs, 1, C_HEADS, C_HEAD_DIM), v_s.reshape(Bs, 1, C_HEADS, C_HEAD_DIM),
            logf_s.reshape(Bs, 1, C_HEADS))
```

```python
import jax
import jax.numpy as jnp
import numpy as np
from jax import lax
from jax.experimental import pallas as pl
from jax.experimental.pallas import tpu as pltpu

D_MODEL = 1024
A_WIDTH = 512
A_GROUPS = 4
A_GROUP_DIM = 128
CHUNK = 128
B_WIDTH = 512
B_BLOCKS = 8
B_BLOCK_DIM = 64
CONV_W = 4
LRU_C = 8.0
C_HEADS = 16
C_HEAD_DIM = 64
C_WIDTH = 1024
PAGE_SIZE = 128
ATTN_SCALE = C_HEAD_DIM ** -0.5
EPS = 1e-6

LANES = 128
SUBLANES = 8
NEG_BIG = -1e30

F32 = jnp.float32
BF16 = jnp.bfloat16

L0_TILE = 1024
PROJ_TILE = 512
ATT_T = 512
OUT_TILE = 1024
PAGES_PER_STEP = 16
DECODE_SLOTS = 2
GATE_GROUPS = 2
ATT_BUFFERS = 4
MAX_SLABS = 8
AUG_ROWS = 128
V_AUG_ROWS = 80
LOG2E = 1.4426950408889634
N_BIAS = 3
VMEM_LIMIT = 56 * 1024 * 1024


def _cparams(n_grid_dims, n_operands=None, fused_operands=()):
    fusion = None
    if fused_operands:
        fusion = tuple(i in fused_operands for i in range(n_operands))
    return pltpu.CompilerParams(
        dimension_semantics=("arbitrary",) * n_grid_dims,
        vmem_limit_bytes=VMEM_LIMIT,
        allow_input_fusion=fusion,
    )


def _dot(a, b):
    return jnp.dot(a, b, preferred_element_type=F32)


def _dot_nt(a, b):
    return lax.dot_general(a, b, (((1,), (1,)), ((), ())), preferred_element_type=F32)


def _split_bf16(x):
    hi = x.astype(BF16)
    lo = (x - hi.astype(F32)).astype(BF16)
    return hi, lo


def _rms_rows(x, gain):
    ms = jnp.mean(x * x, axis=-1, keepdims=True)
    return x * lax.rsqrt(ms + EPS) * gain


def _gelu_tanh(x):
    c1 = 0.7978845608028654
    hx = 0.5 * x
    return hx + hx * jnp.tanh(x * (c1 + (c1 * 0.044715) * (x * x)))


def _gmlp_v_rows(pv, vg_ref):
    v = _gelu_tanh(pv)
    parts = []
    for g in range(A_GROUPS):
        sl = slice(g * A_GROUP_DIM, (g + 1) * A_GROUP_DIM)
        parts.append(_rms_rows(v[:, sl], vg_ref[:, sl]))
    return jnp.concatenate(parts, axis=-1)


def _lru_gates(xc, wr_ref, br_ref, wi_ref, bi_ref, lam_ref):
    xcb = xc.astype(BF16)
    def block_gate(w_ref, b_ref):
        gw = B_WIDTH // GATE_GROUPS
        parts = [_dot(xcb[:, g * gw:(g + 1) * gw], w_ref[g]) for g in range(GATE_GROUPS)]
        return jax.nn.sigmoid(jnp.concatenate(parts, axis=1) + b_ref[...])

    r = block_gate(wr_ref, br_ref)
    gi = block_gate(wi_ref, bi_ref)
    log_a = -LRU_C * r * jax.nn.softplus(-lam_ref[...])
    a = jnp.exp(log_a)
    bterm = jnp.sqrt(-jnp.tanh(log_a) * (1.0 + a * a)) * (gi * xc)
    return a, bterm


def _layer0_prompt_kernel(x_ref, g_ref, win_ref, vg_ref, ws_ref, bst_ref, cw_ref, cb_ref,
                          wr_ref, br_ref, wi_ref, bi_ref, lam_ref, wout_ref, conv0_ref, h0_ref,
                          y_ref, convo_ref, ho_ref,
                          xbuf, hcar, a8, b8, s_scr):
    T = L0_TILE
    i = pl.program_id(1)

    @pl.when(i == 0)
    def _():
        xbuf[0:SUBLANES, :] = jnp.zeros((SUBLANES, B_WIDTH), F32)
        xbuf[SUBLANES - (CONV_W - 1):SUBLANES, :] = conv0_ref[0]
        hcar[...] = h0_ref[0]

    x = x_ref[0]
    xn = _rms_rows(x, g_ref[...]).astype(BF16)

    def proj(k):
        return _dot(xn, win_ref[:, k * 512:(k + 1) * 512])

    vn = _gmlp_v_rows(proj(1), vg_ref).astype(BF16)
    tri = (lax.broadcasted_iota(jnp.int32, (CHUNK, CHUNK), 0)
           >= lax.broadcasted_iota(jnp.int32, (CHUNK, CHUNK), 1))
    for g in range(A_GROUPS):
        wg = jnp.where(tri, ws_ref[g], 0.0).astype(BF16)
        bias = bst_ref[:, g:g + 1]
        for c in range(T // CHUNK):
            blk = vn[c * CHUNK:(c + 1) * CHUNK, g * A_GROUP_DIM:(g + 1) * A_GROUP_DIM]
            s_scr[c * CHUNK:(c + 1) * CHUNK, g * A_GROUP_DIM:(g + 1) * A_GROUP_DIM] = (
                _dot(wg, blk) + bias)
    u = _gelu_tanh(proj(0))
    mix_a = (u * s_scr[...] * jax.nn.silu(proj(2))).astype(BF16)

    xb = proj(3)
    xbuf[SUBLANES:SUBLANES + T, :] = xb
    xc = cb_ref[...] + cw_ref[3:4, :] * xb
    for tap in range(CONV_W - 1):
        off = SUBLANES - (CONV_W - 1) + tap
        xc = xc + cw_ref[tap:tap + 1, :] * xbuf[off:off + T, :]
    convo_ref[0] = xbuf[T + SUBLANES - (CONV_W - 1):T + SUBLANES, :]
    xbuf[0:SUBLANES, :] = xbuf[T:T + SUBLANES, :]

    a, bt = _lru_gates(xc, wr_ref, br_ref, wi_ref, bi_ref, lam_ref)

    a = a.reshape(T // SUBLANES, SUBLANES, B_WIDTH)
    bt = bt.reshape(T // SUBLANES, SUBLANES, B_WIDTH)
    row = lax.broadcasted_iota(jnp.int32, (1, SUBLANES, B_WIDTH), 1)
    shift = 1
    while shift < SUBLANES:
        a_sh = pltpu.roll(a, shift, axis=1)
        b_sh = pltpu.roll(bt, shift, axis=1)
        m = row >= shift
        bt = jnp.where(m, a * b_sh + bt, bt)
        a = jnp.where(m, a * a_sh, a)
        shift *= 2
    a8[...] = a.reshape(T, B_WIDTH)
    b8[...] = bt.reshape(T, B_WIDTH)

    def group_step(j, h):
        off = pl.multiple_of(j * SUBLANES, SUBLANES)
        rows = a8[pl.ds(off, SUBLANES), :] * h + b8[pl.ds(off, SUBLANES), :]
        b8[pl.ds(off, SUBLANES), :] = rows
        return rows[SUBLANES - 1:SUBLANES, :]

    h_last = lax.fori_loop(0, T // SUBLANES, group_step, hcar[...], unroll=True)
    hcar[...] = h_last
    ho_ref[0] = h_last

    mix_b = (b8[...] * jax.nn.silu(proj(4))).astype(BF16)

    y = _dot(mix_a, wout_ref[0:A_WIDTH, :]) + _dot(mix_b, wout_ref[A_WIDTH:, :]) + x
    y_ref[0] = y


def _layer0_prompt(x, conv0, h0, w):
    B, L, _ = x.shape
    T = L0_TILE
    assert L % T == 0 and T % CHUNK == 0
    nt = L // T
    full = lambda shape: pl.BlockSpec(shape, lambda b, i: (0,) * len(shape))
    in_specs = [
        pl.BlockSpec((1, T, D_MODEL), lambda b, i: (b, i, 0)),
        full((1, D_MODEL)),
        full(w["win0"].shape),
        full((1, A_WIDTH)),
        full((A_GROUPS, CHUNK, CHUNK)),
        full((CHUNK, A_GROUPS)),
        full((CONV_W, B_WIDTH)),
        full((1, B_WIDTH)),
        full((GATE_GROUPS, B_WIDTH // GATE_GROUPS, B_WIDTH // GATE_GROUPS)), full((1, B_WIDTH)),
        full((GATE_GROUPS, B_WIDTH // GATE_GROUPS, B_WIDTH // GATE_GROUPS)), full((1, B_WIDTH)),
        full((1, B_WIDTH)),
        full((D_MODEL, D_MODEL)),
        pl.BlockSpec((1, CONV_W - 1, B_WIDTH), lambda b, i: (b, 0, 0)),
        pl.BlockSpec((1, 1, B_WIDTH), lambda b, i: (b, 0, 0)),
    ]
    out_specs = [
        pl.BlockSpec((1, T, D_MODEL), lambda b, i: (b, i, 0)),
        pl.BlockSpec((1, CONV_W - 1, B_WIDTH), lambda b, i: (b, 0, 0)),
        pl.BlockSpec((1, 1, B_WIDTH), lambda b, i: (b, 0, 0)),
    ]
    out_shape = [
        jax.ShapeDtypeStruct((B, L, D_MODEL), F32),
        jax.ShapeDtypeStruct((B, CONV_W - 1, B_WIDTH), F32),
        jax.ShapeDtypeStruct((B, 1, B_WIDTH), F32),
    ]
    y, convo, ho = pl.pallas_call(
        _layer0_prompt_kernel,
        grid=(B, nt),
        in_specs=in_specs,
        out_specs=out_specs,
        out_shape=out_shape,
        scratch_shapes=[
            pltpu.VMEM((T + SUBLANES, B_WIDTH), F32),
            pltpu.VMEM((1, B_WIDTH), F32),
            pltpu.VMEM((T, B_WIDTH), F32),
            pltpu.VMEM((T, B_WIDTH), F32),
            pltpu.VMEM((T, A_WIDTH), F32),
        ],
        compiler_params=_cparams(2, n_operands=16, fused_operands=(2, 13)),
        name="layer0_prompt",
    )(x, w["norm0_g"], w["win0"], w["vg"], w["ws"], w["bst"], w["cw"], w["cb"],
      w["wr"], w["br"], w["wi"], w["bi"], w["lam"], w["wout0"], conv0, h0.reshape(B, 1, B_WIDTH))
    return y, convo, ho.reshape(B, B_WIDTH)


def _layer0_sample_kernel(x_ref, g_ref, win_ref, vg_ref, ws_ref, bst_ref, cw_ref, cb_ref,
                          wr_ref, br_ref, wi_ref, bi_ref, lam_ref, wout_ref, conv_ref, h_ref,
                          y_ref, v_ref, convo_ref, ho_ref):
    x = x_ref[...]
    xn = _rms_rows(x, g_ref[...]).astype(BF16)

    def proj(k):
        return _dot(xn, win_ref[:, k * 512:(k + 1) * 512])

    vn = _gmlp_v_rows(proj(1), vg_ref)
    v_ref[...] = vn
    s_parts = []
    for g in range(A_GROUPS):
        sl = slice(g * A_GROUP_DIM, (g + 1) * A_GROUP_DIM)
        s_parts.append(ws_ref[g, 0:1, 0:1] * vn[:, sl] + bst_ref[0:1, g:g + 1])
    s = jnp.concatenate(s_parts, axis=-1)
    mix_a = (_gelu_tanh(proj(0)) * s * jax.nn.silu(proj(2))).astype(BF16)

    xb = proj(3)
    xc = cb_ref[...] + cw_ref[3:4, :] * xb
    for tap in range(CONV_W - 1):
        xc = xc + cw_ref[tap:tap + 1, :] * conv_ref[:, tap * B_WIDTH:(tap + 1) * B_WIDTH]
    for tap in range(CONV_W - 2):
        convo_ref[:, tap * B_WIDTH:(tap + 1) * B_WIDTH] = (
            conv_ref[:, (tap + 1) * B_WIDTH:(tap + 2) * B_WIDTH])
    convo_ref[:, (CONV_W - 2) * B_WIDTH:] = xb

    a, bt = _lru_gates(xc, wr_ref, br_ref, wi_ref, bi_ref, lam_ref)
    h = a * h_ref[...] + bt
    ho_ref[...] = h
    mix_b = (h * jax.nn.silu(proj(4))).astype(BF16)
    y_ref[...] = _dot(mix_a, wout_ref[0:A_WIDTH, :]) + _dot(mix_b, wout_ref[A_WIDTH:, :]) + x


def _layer0_sample(x, conv, h, w):
    n = x.shape[0]
    out_shape = [
        jax.ShapeDtypeStruct((n, D_MODEL), F32),
        jax.ShapeDtypeStruct((n, A_WIDTH), F32),
        jax.ShapeDtypeStruct((n, (CONV_W - 1) * B_WIDTH), F32),
        jax.ShapeDtypeStruct((n, B_WIDTH), F32),
    ]
    return pl.pallas_call(
        _layer0_sample_kernel,
        out_shape=out_shape,
        compiler_params=pltpu.CompilerParams(vmem_limit_bytes=VMEM_LIMIT),
        name="layer0_sample",
    )(x, w["norm0_g"], w["win0"], w["vg"], w["ws"], w["bst"], w["cw"], w["cb"],
      w["wr"], w["br"], w["wi"], w["bi"], w["lam"], w["wout0"],
      conv.reshape(n, (CONV_W - 1) * B_WIDTH), h)


def _head_norm_cols(t, gain_col):
    ms = jnp.mean(t * t, axis=0, keepdims=True)
    return t * lax.rsqrt(ms + EPS) * gain_col


def _fox_proj_prompt_kernel(x_ref, g_ref, wt_ref, bf_ref, qg_ref, kg_ref,
                            qaug_ref, kaug_ref, kt_ref, vt_ref, gt_ref, lf_ref, carry):
    T = PROJ_TILE
    i = pl.program_id(1)

    @pl.when(i == 0)
    def _():
        carry[...] = jnp.zeros_like(carry)

    xn = _rms_rows(x_ref[0], g_ref[...])
    xnt = xn.T.astype(BF16)

    def proj_t(k):
        return _dot(wt_ref[k * C_WIDTH:(k + 1) * C_WIDTH, :], xnt)

    wf_t = wt_ref[4 * C_WIDTH:4 * C_WIDTH + C_HEADS, :]
    lf = jax.nn.log_sigmoid(_dot(wf_t, xnt) + bf_ref[...])
    lf_ref[0] = lf
    upper = (lax.broadcasted_iota(jnp.int32, (T, T), 0)
             <= lax.broadcasted_iota(jnp.int32, (T, T), 1)).astype(BF16)
    hi, lo = _split_bf16(lf)
    c = _dot(hi, upper) + _dot(lo, upper) + carry[...]
    carry[...] = c[:, T - 1:T]
    pieces, rest = [], c * (-LOG2E)
    for _ in range(N_BIAS):
        p = rest.astype(BF16).astype(F32)
        pieces.append(p)
        rest = rest - p

    sub = lax.broadcasted_iota(jnp.int32, (SUBLANES, T), 0)
    ones_rows = jnp.where(sub < N_BIAS, 1.0, 0.0)
    pad = jnp.zeros((AUG_ROWS - C_HEAD_DIM - SUBLANES, T), F32)
    qt = proj_t(0)
    kt = proj_t(1)
    for h in range(C_HEADS):
        sl = slice(h * C_HEAD_DIM, (h + 1) * C_HEAD_DIM)
        qn = _head_norm_cols(qt[sl], qg_ref[...]) * (ATTN_SCALE * LOG2E)
        kn = _head_norm_cols(kt[sl], kg_ref[...])
        kt_ref[0, h] = kn
        qaug_ref[0, h, 0] = jnp.concatenate([qn, ones_rows, pad], axis=0).astype(BF16)
        bias_rows = jnp.zeros((SUBLANES, T), F32)
        for j in range(N_BIAS):
            bias_rows = jnp.where(sub == j, pieces[j][h:h + 1, :], bias_rows)
        kaug_t = jnp.concatenate([kn, bias_rows, pad], axis=0)
        kaug_ref[0, h] = kaug_t.T.astype(BF16)
    vt_ref[0] = proj_t(2).reshape(C_HEADS, C_HEAD_DIM, T)
    gt_ref[0] = proj_t(3).astype(BF16)


def _fox_proj_prompt(x, w):
    B, L, _ = x.shape
    T = PROJ_TILE
    assert L % ATT_T == 0 and ATT_T % T == 0
    per_q = ATT_T // T
    full = lambda shape: pl.BlockSpec(shape, lambda b, i: (0,) * len(shape))
    in_specs = [
        pl.BlockSpec((1, T, D_MODEL), lambda b, i: (b, i, 0)),
        full((1, D_MODEL)), full((4 * C_WIDTH + C_HEADS, D_MODEL)),
        full((C_HEADS, 1)), full((C_HEAD_DIM, 1)), full((C_HEAD_DIM, 1)),
    ]
    out_specs = [
        pl.BlockSpec((1, C_HEADS, 1, AUG_ROWS, T), lambda b, i: (b, 0, i // per_q, 0, i % per_q)),
        pl.BlockSpec((1, C_HEADS, T, AUG_ROWS), lambda b, i: (b, 0, i, 0)),
        pl.BlockSpec((1, C_HEADS, C_HEAD_DIM, T), lambda b, i: (b, 0, 0, i)),
        pl.BlockSpec((1, C_HEADS, C_HEAD_DIM, T), lambda b, i: (b, 0, 0, i)),
        pl.BlockSpec((1, C_WIDTH, T), lambda b, i: (b, 0, i)),
        pl.BlockSpec((1, C_HEADS, T), lambda b, i: (b, 0, i)),
    ]
    out_shape = [
        jax.ShapeDtypeStruct((B, C_HEADS, L // ATT_T, AUG_ROWS, ATT_T), BF16),
        jax.ShapeDtypeStruct((B, C_HEADS, L, AUG_ROWS), BF16),
        jax.ShapeDtypeStruct((B, C_HEADS, C_HEAD_DIM, L), F32),
        jax.ShapeDtypeStruct((B, C_HEADS, C_HEAD_DIM, L), F32),
        jax.ShapeDtypeStruct((B, C_WIDTH, L), BF16),
        jax.ShapeDtypeStruct((B, C_HEADS, L), F32),
    ]
    return pl.pallas_call(
        _fox_proj_prompt_kernel,
        grid=(B, L // T),
        in_specs=in_specs,
        out_specs=out_specs,
        out_shape=out_shape,
        scratch_shapes=[pltpu.VMEM((C_HEADS, 1), F32)],
        compiler_params=_cparams(2, n_operands=6, fused_operands=(2,)),
        name="fox_proj_prompt",
    )(x, w["norm1_g"], w["win1_t"], w["bf_col"], w["qg_col"], w["kg_col"])


def _fox_attn_kernel(pt_ref,
                     qaug_ref, kaug_ref, vt_ref, qcol_ref, kcol_ref, vnew_ref, lfn_ref,
                     kcache, vcache, lfcache,
                     o_ref, dec_ref,
                     m_scr, acc_scr, vaug_scr, sblk_scr, mblk_scr, ablk_scr,
                     pages, lfpages, page_sem, lf_sem, qb, s_scr, r_scr, ps_scr, l_scr, dacc):
    nq = qaug_ref.shape[2]
    t = ATT_T
    blocks = [(k, q, False) for k in range(nq) for q in range(k + 1, nq)]
    blocks += [(k, k, True) for k in range(nq)]

    d = _DecodeRefs(pt_ref, qcol_ref, kcol_ref, vnew_ref, lfn_ref, kcache, vcache, lfcache, dec_ref,
                    pages, lfpages, page_sem, lf_sem, qb, s_scr, r_scr, ps_scr, l_scr, dacc)
    step = pl.program_id(0) * pl.num_programs(1) + pl.program_id(1)
    n_steps = pl.num_programs(0) * pl.num_programs(1)
    assert d.n_units % DECODE_SLOTS == 0

    @pl.when(step == 0)
    def _():
        for unit in range(DECODE_SLOTS):
            _decode_dma(d, 0, False, unit, unit, start=True)

    m_scr[...] = jnp.full(m_scr.shape, NEG_BIG, F32)
    acc_scr[...] = jnp.zeros(acc_scr.shape, F32)
    extra = lax.broadcasted_iota(jnp.int32, (V_AUG_ROWS - C_HEAD_DIM, t), 0)
    one_row = jnp.where(extra == 0, 1.0, 0.0)
    for kb in range(nq):
        vaug_scr[kb] = jnp.concatenate(
            [vt_ref[0, 0, :, kb * t:(kb + 1) * t], one_row], axis=0).astype(BF16)

    def pieces(masked):
        return [(t // 2, 0, t // 2), (t, t // 2, t)] if masked else [(t, 0, t)]

    def col_max(s):
        slab = jnp.max(s.reshape(MAX_SLABS, s.shape[0] // MAX_SLABS, s.shape[1]), axis=0)
        return jnp.max(slab, axis=0, keepdims=True)

    def start(block, buf):
        s_ref, m_ref, alpha_ref = buf
        ki, qi, masked = block
        maxes = []
        for nk, q0, q1 in pieces(masked):
            keys = kaug_ref[0, 0, ki * t:ki * t + nk, :]
            s = _dot(keys, qaug_ref[0, 0, qi, :, q0:q1])
            if masked:
                causal = (lax.broadcasted_iota(jnp.int32, s.shape, 0)
                          <= lax.broadcasted_iota(jnp.int32, s.shape, 1) + q0)
                s = jnp.where(causal, s, NEG_BIG)
            s_ref[0:nk, q0:q1] = s
            maxes.append(col_max(s))
        m_old = m_scr[qi]
        m_new = jnp.maximum(m_old, jnp.concatenate(maxes, axis=1))
        alpha_ref[...] = jnp.exp2(m_old - m_new)
        m_ref[...] = m_new
        m_scr[qi] = m_new

    def finish(block, buf):
        s_ref, m_ref, alpha_ref = buf
        ki, qi, masked = block
        outs = []
        for nk, q0, q1 in pieces(masked):
            p = jnp.exp2(s_ref[0:nk, q0:q1] - m_ref[:, q0:q1]).astype(BF16)
            outs.append(_dot(vaug_scr[ki, :, 0:nk], p))
        acc_scr[qi] = alpha_ref[...] * acc_scr[qi] + jnp.concatenate(outs, axis=1)

    bufs = [(sblk_scr.at[k], mblk_scr.at[k], ablk_scr.at[k]) for k in range(ATT_BUFFERS)]
    depth = ATT_BUFFERS - 1

    assert len(blocks) % d.n_units == 0
    blocks_per_unit = len(blocks) // d.n_units
    for j in range(depth):
        start(blocks[j], bufs[j])
    for i, block in enumerate(blocks):
        if i % blocks_per_unit == 0:
            _decode_unit(d, step, n_steps, i // blocks_per_unit)
        if i + depth < len(blocks):
            start(blocks[i + depth], bufs[(i + depth) % len(bufs)])
        finish(block, bufs[i % len(bufs)])

    for qi in range(nq):
        o_ref[0, 0, :, qi * t:(qi + 1) * t] = (
            acc_scr[qi, 0:C_HEAD_DIM, :] / acc_scr[qi, C_HEAD_DIM:C_HEAD_DIM + 1, :]).astype(BF16)


def _fox_attention(qaug, kaug, vt, q_s, k_s, v_s, lf_s, cache_k, cache_v, cache_logf, page_table):
    B, H, nq, _, t = qaug.shape
    L = nq * t
    n, n_pages = page_table.shape
    PP = PAGES_PER_STEP
    assert nq % 2 == 0 and n_pages % PP == 0
    assert B * H == 2 * n, "one decode (batch, phase) per prompt (batch, head) grid step"
    kt_cache = jnp.transpose(cache_k, (0, 2, 3, 1))
    vt_cache = jnp.transpose(cache_v, (0, 2, 3, 1))
    lft_cache = jnp.transpose(cache_logf, (0, 2, 1))
    to_cols = lambda a: jnp.transpose(a.reshape(n, C_HEADS, C_HEAD_DIM), (0, 2, 1))
    dec_block = lambda shape: pl.BlockSpec(
        (1,) + shape, lambda b, h, pt: ((b * H + h) // 2, 0, 0))
    hbm = pl.BlockSpec(memory_space=pl.ANY)
    grid_spec = pltpu.PrefetchScalarGridSpec(
        num_scalar_prefetch=1,
        grid=(B, H),
        in_specs=[
            pl.BlockSpec((1, 1, nq, AUG_ROWS, t), lambda b, h, pt: (b, h, 0, 0, 0)),
            pl.BlockSpec((1, 1, L, AUG_ROWS), lambda b, h, pt: (b, h, 0, 0)),
            pl.BlockSpec((1, 1, C_HEAD_DIM, L), lambda b, h, pt: (b, h, 0, 0)),
            dec_block((C_HEAD_DIM, C_HEADS)), dec_block((C_HEAD_DIM, C_HEADS)),
            dec_block((C_HEADS, C_HEAD_DIM)), dec_block((C_HEADS, 1)),
            hbm, hbm, hbm,
        ],
        out_specs=[
            pl.BlockSpec((1, 1, C_HEAD_DIM, L), lambda b, h, pt: (b, h, 0, 0)),
            dec_block((C_HEADS, C_HEAD_DIM)),
        ],
        scratch_shapes=[
            pltpu.VMEM((nq, 1, t), F32),
            pltpu.VMEM((nq, V_AUG_ROWS, t), F32),
            pltpu.VMEM((nq, V_AUG_ROWS, t), BF16),
            pltpu.VMEM((ATT_BUFFERS, t, t), F32),
            pltpu.VMEM((ATT_BUFFERS, 1, t), F32),
            pltpu.VMEM((ATT_BUFFERS, 1, t), F32),
            pltpu.VMEM((DECODE_SLOTS, PP, C_HEADS, C_HEAD_DIM, PAGE_SIZE), F32),
            pltpu.VMEM((DECODE_SLOTS, PP, C_HEADS, PAGE_SIZE), F32),
            pltpu.SemaphoreType.DMA((DECODE_SLOTS,)),
            pltpu.SemaphoreType.DMA((DECODE_SLOTS,)),
            pltpu.VMEM((C_HEADS, C_HEAD_DIM, PAGE_SIZE), F32),
            pltpu.VMEM((n_pages, C_HEADS, PAGE_SIZE), F32),
            pltpu.VMEM((C_HEADS, PAGE_SIZE), F32),
            pltpu.VMEM((C_HEADS, 1), F32),
            pltpu.VMEM((C_HEADS, 1), F32),
            pltpu.VMEM((C_HEADS, C_HEAD_DIM, PAGE_SIZE), F32),
        ],
    )
    att_t, att_s = pl.pallas_call(
        _fox_attn_kernel,
        grid_spec=grid_spec,
        out_shape=[jax.ShapeDtypeStruct((B, H, C_HEAD_DIM, L), BF16),
                   jax.ShapeDtypeStruct((n, C_HEADS, C_HEAD_DIM), F32)],
        compiler_params=_cparams(2),
        name="fox_attention",
    )(page_table, qaug, kaug, vt,
      to_cols(q_s), to_cols(k_s), v_s.reshape(n, C_HEADS, C_HEAD_DIM), lf_s.reshape(n, C_HEADS, 1),
      kt_cache, vt_cache, lft_cache)
    return att_t, att_s.reshape(n, C_WIDTH)


def _gate_out_t_kernel(att_ref, gate_ref, x_ref, wt_ref, y_ref):
    mixed = (att_ref[0].astype(F32) * jax.nn.silu(gate_ref[0].astype(F32))).astype(BF16)
    y_ref[0] = _dot(wt_ref[...], mixed).T + x_ref[0]


def _gate_out_t(att_t, gate_t, x, wout_t):
    B, L, _ = x.shape
    T = OUT_TILE
    assert L % T == 0
    col = pl.BlockSpec((1, C_WIDTH, T), lambda b, i: (b, 0, i))
    row = pl.BlockSpec((1, T, D_MODEL), lambda b, i: (b, i, 0))
    return pl.pallas_call(
        _gate_out_t_kernel,
        grid=(B, L // T),
        in_specs=[col, col, row, pl.BlockSpec((D_MODEL, C_WIDTH), lambda b, i: (0, 0))],
        out_specs=row,
        out_shape=jax.ShapeDtypeStruct((B, L, D_MODEL), F32),
        compiler_params=_cparams(2),
        name="gate_out_prompt",
    )(att_t, gate_t, x, wout_t)


def _fox_proj_sample_kernel(x_ref, g_ref, wt_ref, bf_ref, qg_ref, kg_ref, seg_ref, exp_ref,
                            q_ref, k_ref, v_ref, gate_ref, lf_ref):
    xn = _rms_rows(x_ref[...], g_ref[...]).astype(BF16)

    def head_norm(t, gain):
        ssq = _dot((t * t).astype(BF16), seg_ref[...])
        rs = lax.rsqrt(ssq * (1.0 / C_HEAD_DIM) + EPS)
        hi, lo = _split_bf16(rs)
        rs_full = _dot(hi, exp_ref[...]) + _dot(lo, exp_ref[...])
        return t * rs_full * gain

    def proj(k):
        return _dot_nt(xn, wt_ref[k * C_WIDTH:(k + 1) * C_WIDTH, :])

    q_ref[...] = head_norm(proj(0), qg_ref[...]) * ATTN_SCALE
    k_ref[...] = head_norm(proj(1), kg_ref[...])
    v_ref[...] = proj(2)
    gate_ref[...] = proj(3)
    wf_t = wt_ref[4 * C_WIDTH:4 * C_WIDTH + C_HEADS, :]
    lf_ref[...] = jax.nn.log_sigmoid(_dot_nt(xn, wf_t) + bf_ref[...])


def _fox_proj_sample(x, w):
    n = x.shape[0]
    out_shape = [
        jax.ShapeDtypeStruct((n, C_WIDTH), F32),
        jax.ShapeDtypeStruct((n, C_WIDTH), F32),
        jax.ShapeDtypeStruct((n, C_WIDTH), F32),
        jax.ShapeDtypeStruct((n, C_WIDTH), F32),
        jax.ShapeDtypeStruct((n, C_HEADS), F32),
    ]
    return pl.pallas_call(
        _fox_proj_sample_kernel,
        out_shape=out_shape,
        compiler_params=pltpu.CompilerParams(vmem_limit_bytes=VMEM_LIMIT),
        name="fox_proj_sample",
    )(x, w["norm1_g"], w["win1_t"], w["bf_row"], w["qg"], w["kg"], w["seg"], w["exp"])


def _sublane_total(x):
    acc = x[0:SUBLANES]
    for r in range(1, x.shape[0] // SUBLANES):
        acc = acc + x[r * SUBLANES:(r + 1) * SUBLANES]
    shift = SUBLANES // 2
    while shift >= 1:
        acc = acc + pltpu.roll(acc, shift, axis=0)
        shift //= 2
    return acc


class _DecodeRefs:
    def __init__(self, pt, qcol, kcol, vnew, lfn, kcache, vcache, lfcache, out,
                 pages, lfpages, page_sem, lf_sem, qb, s_scr, r_scr, ps_scr, l_scr, acc):
        self.pt, self.qcol, self.kcol, self.vnew, self.lfn = pt, qcol, kcol, vnew, lfn
        self.kcache, self.vcache, self.lfcache, self.out = kcache, vcache, lfcache, out
        self.pages, self.lfpages, self.page_sem, self.lf_sem = pages, lfpages, page_sem, lf_sem
        self.qb, self.s_scr, self.r_scr, self.ps_scr, self.l_scr, self.acc = (
            qb, s_scr, r_scr, ps_scr, l_scr, acc)
        self.n_pages = pt.shape[1]
        self.n_units = self.n_pages // PAGES_PER_STEP


def _decode_copies(d, batch, phase_is_v, unit, slot):
    src = d.vcache if phase_is_v else d.kcache
    copies = []
    for i in range(PAGES_PER_STEP):
        page = d.pt[batch, d.n_pages - 1 - (unit * PAGES_PER_STEP + i)]
        copies.append(pltpu.make_async_copy(src.at[page], d.pages.at[slot, i], d.page_sem.at[slot]))
        if not phase_is_v:
            copies.append(pltpu.make_async_copy(d.lfcache.at[page], d.lfpages.at[slot, i],
                                                d.lf_sem.at[slot]))
    return copies


def _decode_dma(d, batch, phase_is_v, unit, slot, start):
    for c in _decode_copies(d, batch, phase_is_v, unit, slot):
        if start:
            c.start()
        else:
            c.wait()


def _decode_k_unit(d, unit, slot):
    PP = PAGES_PER_STEP

    @pl.when(unit == 0)
    def _():
        qc = d.qcol[0]
        for h in range(C_HEADS):
            d.qb[h] = jnp.broadcast_to(qc[:, h:h + 1], (C_HEAD_DIM, PAGE_SIZE))
        d.r_scr[...] = jnp.broadcast_to(d.lfn[0], (C_HEADS, PAGE_SIZE))

    slot_i = lax.broadcasted_iota(jnp.int32, (PAGE_SIZE, 2 * PAGE_SIZE), 0)
    slot_j = lax.broadcasted_iota(jnp.int32, (PAGE_SIZE, 2 * PAGE_SIZE), 1)
    sum_mat = jnp.where((slot_j >= PAGE_SIZE) | (slot_i > slot_j), 1.0, 0.0).astype(BF16)
    lf_all = jnp.concatenate([d.lfpages[slot, i] for i in range(PP)], axis=0)
    hi, lo = _split_bf16(lf_all)
    sums = _dot(hi, sum_mat) + _dot(lo, sum_mat)

    sub = lax.broadcasted_iota(jnp.int32, (SUBLANES, PAGE_SIZE), 0)
    groups = C_HEADS // SUBLANES
    s_parts = [[jnp.zeros((SUBLANES, PAGE_SIZE), F32) for _ in range(groups)] for _ in range(PP)]
    for h in range(C_HEADS):
        qh = d.qb[h]
        for i in range(PP):
            tot = _sublane_total(d.pages[slot, i, h] * qh)
            g = h // SUBLANES
            s_parts[i][g] = jnp.where(sub == h % SUBLANES, tot, s_parts[i][g])
    r = d.r_scr[...]
    for i in range(PP):
        s = jnp.concatenate(s_parts[i], axis=0)
        page = sums[i * C_HEADS:(i + 1) * C_HEADS]
        d.s_scr[unit * PP + i] = s + page[:, 0:PAGE_SIZE] + r
        r = r + page[:, PAGE_SIZE:]
    d.r_scr[...] = r


def _decode_v_unit(d, unit, slot):
    PP = PAGES_PER_STEP

    @pl.when(unit == 0)
    def _():
        eye = (lax.broadcasted_iota(jnp.int32, (C_HEADS, C_HEADS), 0)
               == lax.broadcasted_iota(jnp.int32, (C_HEADS, C_HEADS), 1))
        self_row = jnp.sum(d.qcol[0] * d.kcol[0], axis=0, keepdims=True)
        s_self = jnp.sum(jnp.where(eye, self_row, 0.0), axis=1, keepdims=True)
        s_all = d.s_scr[...]
        m = jnp.max(jnp.max(s_all, axis=0), axis=1, keepdims=True)
        m = jnp.maximum(m, s_self)
        p_all = jnp.exp(s_all - m)
        d.s_scr[...] = p_all
        p_self = jnp.exp(s_self - m)
        d.ps_scr[...] = p_self
        d.l_scr[...] = jnp.sum(jnp.sum(p_all, axis=0), axis=1, keepdims=True) + p_self
        d.acc[...] = jnp.zeros(d.acc.shape, F32)

    p_pages = [d.s_scr[unit * PP + i] for i in range(PP)]
    for h in range(C_HEADS):
        a = d.acc[h]
        for i in range(PP):
            a = a + p_pages[i][h:h + 1, :] * d.pages[slot, i, h]
        d.acc[h] = a

    @pl.when(unit == d.n_units - 1)
    def _():
        ones = jnp.ones((SUBLANES, PAGE_SIZE), BF16)
        rows = lax.broadcasted_iota(jnp.int32, (C_HEADS, C_HEAD_DIM), 0)
        out = jnp.zeros((C_HEADS, C_HEAD_DIM), F32)
        for h in range(C_HEADS):
            hi, lo = _split_bf16(d.acc[h])
            tot = _dot_nt(ones, hi) + _dot_nt(ones, lo)
            out = jnp.where(rows == h, tot[0:1, :], out)
        d.out[0] = (out + d.ps_scr[...] * d.vnew[0]) / d.l_scr[...]


def _decode_unit(d, step, n_steps, unit):
    batch = step // 2
    ahead = DECODE_SLOTS
    slot = unit % ahead
    for phase_is_v in (False, True):
        @pl.when(step % 2 == int(phase_is_v))
        def _():
            _decode_dma(d, batch, phase_is_v, unit, slot, start=False)
            if phase_is_v:
                _decode_v_unit(d, unit, slot)
            else:
                _decode_k_unit(d, unit, slot)

            @pl.when(unit + ahead < d.n_units)
            def _():
                _decode_dma(d, batch, phase_is_v, unit + ahead, slot, start=True)

            @pl.when((unit + ahead >= d.n_units) & (step < n_steps - 1))
            def _():
                _decode_dma(d, (step + 1) // 2, not phase_is_v, unit + ahead - d.n_units, slot, start=True)


def _gate_out_kernel(att_ref, gate_ref, x_ref, w_ref, y_ref):
    mixed = (att_ref[...] * jax.nn.silu(gate_ref[...])).astype(BF16)
    y_ref[...] = _dot(mixed, w_ref[...]) + x_ref[...]


def _gate_out(att, gate, x, wout):
    return pl.pallas_call(
        _gate_out_kernel,
        out_shape=jax.ShapeDtypeStruct(x.shape, F32),
        compiler_params=pltpu.CompilerParams(vmem_limit_bytes=VMEM_LIMIT),
        name="gate_out_sample",
    )(att, gate, x, wout)


def _block_diag(w):
    nb, d, _ = w.shape
    per = nb // GATE_GROUPS
    wg = w.reshape(GATE_GROUPS, per, d, d)
    eye = jnp.eye(per, dtype=w.dtype)
    return (eye[None, :, None, :, None] * wg[:, :, :, None, :]).reshape(GATE_GROUPS, per * d, per * d)


def _prepare_weights(norm0_g, w_in0, gmlp_v_g, gmlp_w_s, gmlp_b_s, lru_conv_w, lru_conv_b,
                     lru_w_r, lru_b_r, lru_w_i, lru_b_i, lru_lambda, w_out0, norm1_g, w_in1,
                     fox_b_f, q_norm_g, k_norm_g, w_out1):
    lane = np.arange(C_WIDTH) // C_HEAD_DIM
    seg = (lane[:, None] == np.arange(LANES)[None, :]).astype(np.float32)
    win1_t = w_in1.T.astype(BF16)
    wout1 = w_out1.astype(BF16)
    return {
        "norm0_g": norm0_g.reshape(1, D_MODEL),
        "win0": w_in0.astype(BF16),
        "vg": gmlp_v_g.reshape(1, A_WIDTH),
        "ws": gmlp_w_s,
        "bst": gmlp_b_s.T,
        "cw": lru_conv_w,
        "cb": lru_conv_b.reshape(1, B_WIDTH),
        "wr": _block_diag(lru_w_r).astype(BF16),
        "br": lru_b_r.reshape(1, B_WIDTH),
        "wi": _block_diag(lru_w_i).astype(BF16),
        "bi": lru_b_i.reshape(1, B_WIDTH),
        "lam": lru_lambda.reshape(1, B_WIDTH),
        "wout0": w_out0.astype(BF16),
        "norm1_g": norm1_g.reshape(1, D_MODEL),
        "win1_t": win1_t,
        "bf_row": fox_b_f.reshape(1, C_HEADS),
        "bf_col": fox_b_f.reshape(C_HEADS, 1),
        "qg": jnp.tile(q_norm_g, C_HEADS).reshape(1, C_WIDTH),
        "kg": jnp.tile(k_norm_g, C_HEADS).reshape(1, C_WIDTH),
        "qg_col": q_norm_g.reshape(C_HEAD_DIM, 1),
        "kg_col": k_norm_g.reshape(C_HEAD_DIM, 1),
        "seg": jnp.asarray(seg, BF16),
        "exp": jnp.asarray(seg.T, BF16),
        "wout1": wout1,
        "wout1_t": wout1.T,
    }


def kernel(x_prompt, x_sample, state_lru_conv, state_lru_h, cache_k, cache_v, cache_logf, page_table, norm0_g, w_in0, gmlp_v_g, gmlp_w_s, gmlp_b_s, lru_conv_w, lru_conv_b, lru_w_r, lru_b_r, lru_w_i, lru_b_i, lru_lambda, w_out0, norm1_g, w_in1, fox_b_f, q_norm_g, k_norm_g, w_out1):
    Bp, L, _ = x_prompt.shape
    Bs = x_sample.shape[0]
    w = _prepare_weights(norm0_g, w_in0, gmlp_v_g, gmlp_w_s, gmlp_b_s, lru_conv_w, lru_conv_b,
                         lru_w_r, lru_b_r, lru_w_i, lru_b_i, lru_lambda, w_out0, norm1_g, w_in1,
                         fox_b_f, q_norm_g, k_norm_g, w_out1)

    conv0 = jnp.zeros((Bp, CONV_W - 1, B_WIDTH), F32)
    h0 = jnp.zeros((Bp, B_WIDTH), F32)
    yp0, lru_conv_p, lru_h_p = _layer0_prompt(x_prompt, conv0, h0, w)
    ys0, gmlp_v_s, conv_s, lru_h_s = _layer0_sample(
        x_sample.reshape(Bs, D_MODEL), state_lru_conv, state_lru_h, w)

    qaug, kaug, kt_p, vt_p, gt_p, lft_p = _fox_proj_prompt(yp0, w)
    q_s, k_s, v_s, g_s, logf_s = _fox_proj_sample(ys0, w)
    att_t, att_s = _fox_attention(qaug, kaug, vt_p, q_s, k_s, v_s, logf_s,
                                  cache_k, cache_v, cache_logf, page_table)
    yp = _gate_out_t(att_t.reshape(Bp, C_WIDTH, L), gt_p, yp0, w["wout1_t"])
    k_p = jnp.transpose(kt_p, (0, 3, 1, 2))
    v_p = jnp.transpose(vt_p, (0, 3, 1, 2))
    logf_p = jnp.transpose(lft_p, (0, 2, 1))
    ys = _gate_out(att_s, g_s, ys0, w["wout1"])

    return (yp, ys.reshape(Bs, 1, D_MODEL), lru_conv_p, lru_h_p, k_p, v_p, logf_p,
            gmlp_v_s.reshape(Bs, 1, A_WIDTH), conv_s.reshape(Bs, CONV_W - 1, B_WIDTH), lru_h_s,
            k_s.reshape(Bs, 1, C_HEADS, C_HEAD_DIM), v_s.reshape(Bs, 1, C_HEADS, C_HEAD_DIM),
            logf_s.reshape(Bs, 1, C_HEADS))
```
